```python
import jax, jax.numpy as jnp
from jax import lax
import numpy as np

D_MODEL = 1024
BATCH = 16
SEQ = 2048
DEPTH = 2

GRID_W = 64
CTX_LEN = 256
HEAD_DIM = 64
A_Q_HEADS = 4
A_KV_HEADS = 2
B_Q_HEADS = 4
B_KV_HEADS = 2
WINDOW = 128
Q_BLOCK = 128
LRU_WIDTH = 512
LRU_BLOCKS = 8
CONV_W = 4
CONV_LEFT = CONV_W // 2
LRU_C = 8.0
D_FF = 4 * D_MODEL
ROPE_BASE = 10000.0
EPS = 1e-6
NEG_INF = -1e30

A_Q = A_Q_HEADS * HEAD_DIM
A_KV = A_KV_HEADS * HEAD_DIM
B_Q = B_Q_HEADS * HEAD_DIM
B_KV = B_KV_HEADS * HEAD_DIM
IN_WIDTHS = (A_Q, A_KV, A_KV, B_Q, B_KV, B_KV, LRU_WIDTH, LRU_WIDTH)
IN_WIDTH = sum(IN_WIDTHS)
MIX_WIDTH = A_Q + B_Q + LRU_WIDTH

kernel_name = 'hybrid_headgroup_diffusion_block'


def rms_norm(x, g):
    xf = x.astype(jnp.float32)
    y = xf * lax.rsqrt(jnp.mean(xf * xf, axis=-1, keepdims=True) + EPS)
    return (y * g.astype(jnp.float32)).astype(x.dtype)


def modulate(h, shift, scale):
    return h * (1 + scale) + shift


def axial_rope_tables(n):
    rows = n // GRID_W
    row = jnp.repeat(jnp.arange(rows, dtype=jnp.float32), GRID_W)
    col = jnp.tile(jnp.arange(GRID_W, dtype=jnp.float32), rows)
    half = HEAD_DIM // 2
    inv_freq = ROPE_BASE ** (-jnp.arange(0, half, 2, dtype=jnp.float32) / half)
    ang_r = row[:, None] * inv_freq
    ang_c = col[:, None] * inv_freq
    return jnp.cos(ang_r), jnp.sin(ang_r), jnp.cos(ang_c), jnp.sin(ang_c)


def rope_1d(x, cos, sin):
    h = x.shape[-1] // 2
    x1, x2 = x[..., :h], x[..., h:]
    cos, sin = cos[:, None, :], sin[:, None, :]
    return jnp.concatenate([x1 * cos - x2 * sin, x2 * cos + x1 * sin], axis=-1)


def apply_axial_rope(x, tabs):
    cos_r, sin_r, cos_c, sin_c = tabs
    xf = x.astype(jnp.float32)
    half = HEAD_DIM // 2
    out = jnp.concatenate([rope_1d(xf[..., :half], cos_r, sin_r),
                           rope_1d(xf[..., half:], cos_c, sin_c)], axis=-1)
    return out.astype(x.dtype)


def group_heads(q, n_kv):
    b, n, h, d = q.shape
    return q.reshape(b, n, n_kv, h // n_kv, d)


def sink_softmax(s, sink):
    m = jnp.maximum(jnp.max(s, axis=-1, keepdims=True), sink)
    p = jnp.exp(s - m)
    return p / (jnp.sum(p, axis=-1, keepdims=True) + jnp.exp(sink - m))


def attn_probs(s, sink):
    if sink is None:
        return jax.nn.softmax(s, axis=-1)
    return sink_softmax(s, sink)


def dense_attention(q, k, v, sink=None):
    s = jnp.einsum('bqkgd,bskd->bkgqs', q, k).astype(jnp.float32) * HEAD_DIM ** -0.5
    p = attn_probs(s, sink).astype(v.dtype)
    return jnp.einsum('bkgqs,bskd->bqkgd', p, v)


def global_block_attention(q, k, v):
    b, n, h, d = q.shape
    n_kv = k.shape[2]
    nb = n // Q_BLOCK
    qb = group_heads(q, n_kv).reshape(b, nb, Q_BLOCK, n_kv, h // n_kv, d)
    out = lax.map(lambda blk: dense_attention(blk, k, v), jnp.moveaxis(qb, 1, 0))
    return jnp.moveaxis(out, 0, 1).reshape(b, n, h * d)


def window_sink_attention(q, k, v, k_ctx, v_ctx, sink):
    b, n, h, d = q.shape
    n_kv = k.shape[2]
    nb = n // Q_BLOCK
    band = 3 * Q_BLOCK
    qb = group_heads(q, n_kv).reshape(b, nb, Q_BLOCK, n_kv, h // n_kv, d)

    def banded(t):
        tp = jnp.pad(t, ((0, 0), (WINDOW, WINDOW), (0, 0), (0, 0))).reshape(b, nb + 2, Q_BLOCK, n_kv, d)
        return jnp.concatenate([tp[:, :-2], tp[:, 1:-1], tp[:, 2:]], axis=2)

    kb, vb = banded(k), banded(v)
    start = jnp.arange(nb)[:, None, None] * Q_BLOCK
    qpos = start + jnp.arange(Q_BLOCK)[None, :, None]
    kpos = start - WINDOW + jnp.arange(band)[None, None, :]
    valid = (jnp.abs(qpos - kpos) <= WINDOW) & (kpos >= 0) & (kpos < n)
    scale = HEAD_DIM ** -0.5
    s_band = jnp.einsum('bnqkgd,bnskd->bnkgqs', qb, kb).astype(jnp.float32) * scale
    s_band = jnp.where(valid[None, :, None, None], s_band, NEG_INF)
    s_ctx = jnp.einsum('bnqkgd,bskd->bnkgqs', qb, k_ctx).astype(jnp.float32) * scale
    p = sink_softmax(jnp.concatenate([s_band, s_ctx], axis=-1), sink).astype(v.dtype)
    out = (jnp.einsum('bnkgqs,bnskd->bnqkgd', p[..., :band], vb)
           + jnp.einsum('bnkgqs,bskd->bnqkgd', p[..., band:], v_ctx))
    return out.reshape(b, n, h * d)


def centred_dwconv(x, w, bias):
    n = x.shape[1]
    xp = jnp.pad(x, ((0, 0), (CONV_LEFT, CONV_W - 1 - CONV_LEFT), (0, 0)))
    out = bias
    for j in range(CONV_W):
        out = out + xp[:, j:j + n] * w[j]
    return out


def rglru_coeffs(x, w_a, b_a, w_i, b_i, lam):
    b, n, width = x.shape
    xb = x.reshape(b, n, LRU_BLOCKS, width // LRU_BLOCKS)
    r = jax.nn.sigmoid(jnp.einsum('bnhi,hij->bnhj', xb, w_a).reshape(b, n, width) + b_a)
    i = jax.nn.sigmoid(jnp.einsum('bnhi,hij->bnhj', xb, w_i).reshape(b, n, width) + b_i)
    log_a = -LRU_C * r.astype(jnp.float32) * jax.nn.softplus(-lam.astype(jnp.float32))
    u = jnp.sqrt(-jnp.expm1(2.0 * log_a)) * (i * x).astype(jnp.float32)
    return jnp.exp(log_a), u


def _lin_combine(left, right):
    a1, b1 = left
    a2, b2 = right
    return a1 * a2, a2 * b1 + b2


def linear_scan(a, u, h0):
    a_cum, h_zero = lax.associative_scan(_lin_combine, (a, u), axis=1)
    return a_cum * h0[:, None, :] + h_zero


def directional_rglru(x_lat, x_ctx, w_a, b_a, w_i, b_i, lam):
    a_c, u_c = rglru_coeffs(x_ctx, w_a, b_a, w_i, b_i, lam)
    h_ctx = linear_scan(a_c, u_c, jnp.zeros((x_ctx.shape[0], x_ctx.shape[2]), jnp.float32))
    a_l, u_l = rglru_coeffs(x_lat, w_a, b_a, w_i, b_i, lam)
    h_lat = linear_scan(a_l, u_l, h_ctx[:, -1])
    return h_lat, h_ctx


def split_columns(t):
    parts, start = [], 0
    for w in IN_WIDTHS:
        parts.append(t[..., start:start + w])
        start += w
    return parts


def token_mixers(z, z_ctx, g_q_a, g_k_a, sink_b, conv_w, conv_b, lru_w_a, lru_b_a, lru_w_i, lru_b_i,
                 lru_lambda, rope, with_ctx_out):
    b, n, _ = z.shape
    m = z_ctx.shape[1]
    qa, ka, va, qb, kb, vb, xr, gr = split_columns(z)
    qa_c, ka_c, va_c, qb_c, kb_c, vb_c, xr_c, gr_c = split_columns(z_ctx)

    def heads(t):
        return t.reshape(t.shape[0], t.shape[1], -1, HEAD_DIM)

    qa = apply_axial_rope(rms_norm(heads(qa), g_q_a), rope)
    ka = apply_axial_rope(rms_norm(heads(ka), g_k_a), rope)
    qa_c = rms_norm(heads(qa_c), g_q_a)
    ka_c = rms_norm(heads(ka_c), g_k_a)
    va, va_c = heads(va), heads(va_c)
    out_a = global_block_attention(qa, jnp.concatenate([ka_c, ka], axis=1),
                                   jnp.concatenate([va_c, va], axis=1))

    qb = apply_axial_rope(heads(qb), rope)
    kb = apply_axial_rope(heads(kb), rope)
    vb, kb_c, vb_c = heads(vb), heads(kb_c), heads(vb_c)
    sink = sink_b.astype(jnp.float32).reshape(B_KV_HEADS, B_Q_HEADS // B_KV_HEADS, 1, 1)
    out_b = window_sink_attention(qb, kb, vb, kb_c, vb_c, sink)

    xr = centred_dwconv(xr, conv_w, conv_b)
    xr_c = centred_dwconv(xr_c, conv_w, conv_b)
    h_f, hc_f = directional_rglru(xr, xr_c, lru_w_a[0], lru_b_a[0], lru_w_i[0], lru_b_i[0], lru_lambda[0])
    h_b, hc_b = directional_rglru(jnp.flip(xr, axis=1), jnp.flip(xr_c, axis=1), lru_w_a[1], lru_b_a[1],
                                  lru_w_i[1], lru_b_i[1], lru_lambda[1])
    out_c = (h_f + jnp.flip(h_b, axis=1)).astype(z.dtype) * jax.nn.gelu(gr)

    feat = jnp.concatenate([out_a, out_b, out_c], axis=-1)
    if not with_ctx_out:
        return feat, None
    out_ac = dense_attention(group_heads(qa_c, A_KV_HEADS), ka_c, va_c).reshape(b, m, A_Q)
    out_bc = dense_attention(group_heads(heads(qb_c), B_KV_HEADS), kb_c, vb_c, sink).reshape(b, m, B_Q)
    out_cc = (hc_f + jnp.flip(hc_b, axis=1)).astype(z.dtype) * jax.nn.gelu(gr_c)
    return feat, jnp.concatenate([out_ac, out_bc, out_cc], axis=-1)


def sq_relu_mlp(h, w1, w2):
    return jnp.square(jax.nn.relu(h @ w1)) @ w2


def hybrid_layer(x, ctx, c_act, c_ctx_act, w_mod, b_mod, g_pre_mix, g_post_mix, g_pre_mlp, g_post_mlp,
                 w_in, g_q_a, g_k_a, sink_b, conv_w, conv_b, lru_w_a, lru_b_a, lru_w_i, lru_b_i, lru_lambda,
                 w_out, w_mlp_in, w_mlp_out, rope, update_ctx):
    sh1, sc1, gt1, sh2, sc2, gt2 = jnp.split((c_act @ w_mod + b_mod)[:, None, :], 6, axis=-1)
    csh1, csc1, cgt1, csh2, csc2, cgt2 = jnp.split(c_ctx_act @ w_mod + b_mod, 6, axis=-1)
    h = modulate(rms_norm(x, g_pre_mix), sh1, sc1)
    h_ctx = modulate(rms_norm(ctx, g_pre_mix), csh1, csc1)
    feat, feat_ctx = token_mixers(h @ w_in, h_ctx @ w_in, g_q_a, g_k_a, sink_b, conv_w, conv_b,
                                  lru_w_a, lru_b_a, lru_w_i, lru_b_i, lru_lambda, rope, update_ctx)
    x = x + gt1 * rms_norm(feat @ w_out, g_post_mix)
    h2 = modulate(rms_norm(x, g_pre_mlp), sh2, sc2)
    x = x + gt2 * rms_norm(sq_relu_mlp(h2, w_mlp_in, w_mlp_out), g_post_mlp)
    if update_ctx:
        ctx = ctx + cgt1 * rms_norm(feat_ctx @ w_out, g_post_mix)
        h2c = modulate(rms_norm(ctx, g_pre_mlp), csh2, csc2)
        ctx = ctx + cgt2 * rms_norm(sq_relu_mlp(h2c, w_mlp_in, w_mlp_out), g_post_mlp)
    return x, ctx


def setup_inputs(seed: int = 0) -> dict:
    key = jax.random.key(seed)
    ks = jax.random.split(key, 24)

    def nrm(k, shape, scale):
        return jax.random.normal(k, shape, jnp.float32) * scale

    def gain(k, shape):
        return 1.0 + 0.05 * jax.random.normal(k, shape, jnp.float32)

    hb = LRU_WIDTH // LRU_BLOCKS
    s = jax.random.uniform(ks[19], (DEPTH, 2, LRU_WIDTH), jnp.float32, 0.9, 0.999) ** (1.0 / LRU_C)
    return {
        'x': nrm(ks[0], (BATCH, SEQ, D_MODEL), 1.0),
        'c': nrm(ks[1], (BATCH, D_MODEL), 1.0),
        'ctx': nrm(ks[2], (BATCH, CTX_LEN, D_MODEL), 1.0),
        'c_ctx': nrm(ks[3], (D_MODEL,), 1.0),
        'w_mod': nrm(ks[4], (DEPTH, D_MODEL, 6 * D_MODEL), 0.5 * D_MODEL ** -0.5),
        'b_mod': nrm(ks[5], (DEPTH, 6 * D_MODEL), 0.01),
        'g_pre_mix': gain(ks[6], (DEPTH, D_MODEL)),
        'g_post_mix': gain(ks[7], (DEPTH, D_MODEL)),
        'g_pre_mlp': gain(ks[8], (DEPTH, D_MODEL)),
        'g_post_mlp': gain(ks[9], (DEPTH, D_MODEL)),
        'w_in': nrm(ks[10], (DEPTH, D_MODEL, IN_WIDTH), D_MODEL ** -0.5),
        'g_q_a': gain(ks[11], (DEPTH, HEAD_DIM)),
        'g_k_a': gain(ks[12], (DEPTH, HEAD_DIM)),
        'sink_b': nrm(ks[13], (DEPTH, B_Q_HEADS), 0.5),
        'conv_w': nrm(ks[14], (DEPTH, CONV_W, LRU_WIDTH), CONV_W ** -0.5),
        'conv_b': nrm(ks[15], (DEPTH, LRU_WIDTH), 0.01),
        'lru_w_a': nrm(ks[16], (DEPTH, 2, LRU_BLOCKS, hb, hb), hb ** -0.5),
        'lru_b_a': nrm(ks[17], (DEPTH, 2, LRU_WIDTH), 0.01),
        'lru_w_i': nrm(ks[18], (DEPTH, 2, LRU_BLOCKS, hb, hb), hb ** -0.5),
        'lru_b_i': nrm(ks[20], (DEPTH, 2, LRU_WIDTH), 0.01),
        'lru_lambda': jnp.log(s) - jnp.log1p(-s),
        'w_out': nrm(ks[21], (DEPTH, MIX_WIDTH, D_MODEL), MIX_WIDTH ** -0.5),
        'w_mlp_in': nrm(ks[22], (DEPTH, D_MODEL, D_FF), D_MODEL ** -0.5),
        'w_mlp_out': nrm(ks[23], (DEPTH, D_FF, D_MODEL), D_FF ** -0.5),
    }


def reference(x, c, ctx, c_ctx, w_mod, b_mod, g_pre_mix, g_post_mix, g_pre_mlp, g_post_mlp, w_in,
              g_q_a, g_k_a, sink_b, conv_w, conv_b, lru_w_a, lru_b_a, lru_w_i, lru_b_i, lru_lambda,
              w_out, w_mlp_in, w_mlp_out):
    rope = axial_rope_tables(x.shape[1])
    c_act = jax.nn.silu(c)
    c_ctx_act = jax.nn.silu(c_ctx)
    for l in range(DEPTH):
        x, ctx = hybrid_layer(x, ctx, c_act, c_ctx_act, w_mod[l], b_mod[l], g_pre_mix[l], g_post_mix[l],
                              g_pre_mlp[l], g_post_mlp[l], w_in[l], g_q_a[l], g_k_a[l], sink_b[l],
                              conv_w[l], conv_b[l], lru_w_a[l], lru_b_a[l], lru_w_i[l], lru_b_i[l],
                              lru_lambda[l], w_out[l], w_mlp_in[l], w_mlp_out[l], rope,
                              l < DEPTH - 1)
    return x
```

```python
import functools

import jax
import jax.numpy as jnp
from jax import lax
from jax.experimental import pallas as pl
from jax.experimental.pallas import tpu as pltpu

F32 = jnp.float32
BF16 = jnp.bfloat16

GRID_W = 64
HEAD_DIM = 64
A_Q_HEADS = 4
A_KV_HEADS = 2
B_Q_HEADS = 4
B_KV_HEADS = 2
WINDOW = 128
LRU_WIDTH = 512
LRU_BLOCKS = 8
CONV_W = 4
CONV_LEFT = CONV_W // 2
LRU_C = 8.0
ROPE_BASE = 10000.0
EPS = 1e-6
NEG_INF = -1e30

A_Q = A_Q_HEADS * HEAD_DIM
A_KV = A_KV_HEADS * HEAD_DIM
B_Q = B_Q_HEADS * HEAD_DIM
B_KV = B_KV_HEADS * HEAD_DIM
Q_SCALE = HEAD_DIM ** -0.5

LANES = 128
SUBLANES = 8
VMEM_LIMIT = 56 * 1024 * 1024

ROW_TILE = 512
ATTN_Q_TILE = 256
LRU_CHUNK = 512
SCAN_BLOCK = SUBLANES * SUBLANES

_NT = (((1,), (1,)), ((), ()))


def _const_spec(shape):
    return pl.BlockSpec(shape, lambda *_: (0,) * len(shape), pipeline_mode=pl.Buffered(1))


def _params(*sem):
    return pltpu.CompilerParams(dimension_semantics=sem, vmem_limit_bytes=VMEM_LIMIT)


def _rms(x, g):
    return x * lax.rsqrt(jnp.mean(x * x, axis=-1, keepdims=True) + EPS) * g


def _mod_kernel(c_ref, w_ref, b_ref, o_ref):
    c = c_ref[...]
    act = (c * jax.nn.sigmoid(c)).astype(BF16)
    o_ref[...] = jnp.dot(act, w_ref[...], preferred_element_type=F32) + b_ref[...]


def _modulation(cc, w, b):
    rows, d = cc.shape
    n_out = w.shape[1]
    tn = n_out // 4
    return pl.pallas_call(
        _mod_kernel,
        out_shape=jax.ShapeDtypeStruct((rows, n_out), F32),
        grid=(n_out // tn,),
        in_specs=[
            pl.BlockSpec((rows, d), lambda j: (0, 0)),
            pl.BlockSpec((d, tn), lambda j: (0, j)),
            pl.BlockSpec((1, tn), lambda j: (0, j)),
        ],
        out_specs=pl.BlockSpec((rows, tn), lambda j: (0, j)),
        compiler_params=_params("arbitrary"),
        name="modulation",
    )(cc, w, b)


def _head_rms(t, g, bd):
    sq = t * t
    hi = sq.astype(BF16)
    lo = (sq - hi.astype(F32)).astype(BF16)
    ms = jnp.dot(hi, bd, preferred_element_type=F32) + jnp.dot(lo, bd, preferred_element_type=F32)
    return t * lax.rsqrt(ms + EPS) * g


def _rope(t, cos, sin, low_half):
    outs = []
    for j in range(t.shape[1] // LANES):
        tc = t[:, j * LANES:(j + 1) * LANES]
        partner = jnp.where(low_half, pltpu.roll(tc, LANES - 16, 1), pltpu.roll(tc, 16, 1))
        outs.append(tc * cos + partner * sin)
    return outs[0] if len(outs) == 1 else jnp.concatenate(outs, axis=-1)


def _inproj_kernel(x_ref, mod_ref, gpre_ref, w_ref, gq_ref, gk_ref, bd_ref, cos_ref, sin_ref,
                   qa_ref, ka_ref, va_ref, qb_ref, kb_ref, vb_ref, xr_ref, gr_ref, *, rope):
    x = x_ref[0]
    mod = mod_ref[0]
    h = _rms(x, gpre_ref[...]) * (1.0 + mod[1:2]) + mod[0:1]
    z = jnp.dot(h.astype(BF16), w_ref[...], preferred_element_type=F32)

    bd = bd_ref[...]
    qa = _head_rms(z[:, 0:A_Q], gq_ref[...], bd)
    ka = _head_rms(z[:, A_Q:A_Q + A_KV], gk_ref[...], bd[:A_KV, :A_KV])
    o = A_Q + 2 * A_KV
    qb = z[:, o:o + B_Q]
    kb = z[:, o + B_Q:o + B_Q + B_KV]
    if rope:
        cos = cos_ref[...]
        sin = sin_ref[...]
        lane = lax.broadcasted_iota(jnp.int32, cos.shape, 1)
        low_half = (lane % 32) < 16
        qa = _rope(qa, cos, sin, low_half)
        ka = _rope(ka, cos, sin, low_half)
        qb = _rope(qb, cos, sin, low_half)
        kb = _rope(kb, cos, sin, low_half)
    qa_ref[0] = (qa * Q_SCALE).astype(BF16)
    ka_ref[0] = ka.astype(BF16)
    va_ref[0] = z[:, A_Q + A_KV:o].astype(BF16)
    qb_ref[0] = (qb * Q_SCALE).astype(BF16)
    kb_ref[0] = kb.astype(BF16)
    vb_ref[0] = z[:, o + B_Q + B_KV:o + B_Q + 2 * B_KV].astype(BF16)
    o2 = o + B_Q + 2 * B_KV
    xr_ref[0] = z[:, o2:o2 + LRU_WIDTH]
    gr_ref[0] = z[:, o2 + LRU_WIDTH:o2 + 2 * LRU_WIDTH]


def _inproj(x, mod, gpre, w, gq, gk, bd, cos, sin, *, rope):
    b, n, d = x.shape
    tm = min(ROW_TILE, n)
    nt = n // tm
    widths = (A_Q, A_KV, A_KV, B_Q, B_KV, B_KV, LRU_WIDTH, LRU_WIDTH)
    dtypes = (BF16,) * 6 + (F32, F32)
    row_spec = lambda wd: pl.BlockSpec((1, tm, wd), lambda i, j: (i, j, 0))
    return pl.pallas_call(
        functools.partial(_inproj_kernel, rope=rope),
        out_shape=[jax.ShapeDtypeStruct((b, n, wd), dt) for wd, dt in zip(widths, dtypes)],
        grid=(b, nt),
        in_specs=[
            row_spec(d),
            pl.BlockSpec((1, 6, d), lambda i, j: (i, 0, 0)),
            _const_spec(gpre.shape),
            _const_spec(w.shape),
            _const_spec(gq.shape),
            _const_spec(gk.shape),
            _const_spec(bd.shape),
            pl.BlockSpec((tm, LANES), lambda i, j: (j, 0)),
            pl.BlockSpec((tm, LANES), lambda i, j: (j, 0)),
        ],
        out_specs=[row_spec(wd) for wd in widths],
        compiler_params=_params("parallel", "parallel"),
        name="inproj_rope" if rope else "inproj_ctx",
    )(x, mod, gpre, w, gq, gk, bd, cos, sin)


def _split_heads(qc, low):
    zero = jnp.zeros_like(qc)
    return jnp.where(low, qc, zero), jnp.where(low, zero, qc)


def _dense_attn_kernel(*refs, n_kv, has_sink):
    q_ref = refs[0]
    kv_refs = refs[1:1 + 2 * n_kv]
    sink_ref = refs[1 + 2 * n_kv] if has_sink else None
    o_ref = refs[-1]
    q = q_ref[0]
    tq = q.shape[0]
    low = lax.broadcasted_iota(jnp.int32, (tq, LANES), 1) < HEAD_DIM
    cols = []
    for j in range(2):
        q_lo, q_hi = _split_heads(q[:, j * LANES:(j + 1) * LANES], low)
        qs = jnp.concatenate([q_lo, q_hi], axis=0)
        scores = [lax.dot_general(qs, kv_refs[2 * i][0], _NT, preferred_element_type=F32)
                  for i in range(n_kv)]
        m = jnp.max(scores[0], axis=-1, keepdims=True)
        for s in scores[1:]:
            m = jnp.maximum(m, jnp.max(s, axis=-1, keepdims=True))
        if has_sink:
            sink = jnp.concatenate([jnp.full((tq, 1), sink_ref[j], F32),
                                    jnp.full((tq, 1), sink_ref[2 + j], F32)], axis=0)
            m = jnp.maximum(m, sink)
        denom = jnp.exp(sink - m) if has_sink else None
        acc = None
        for i, s in enumerate(scores):
            p = jnp.exp(s - m)
            ps = jnp.sum(p, axis=-1, keepdims=True)
            denom = ps if denom is None else denom + ps
            pv = jnp.dot(p.astype(BF16), kv_refs[2 * i + 1][0], preferred_element_type=F32)
            acc = pv if acc is None else acc + pv
        out = acc / denom
        cols.append(jnp.where(low, out[:tq], out[tq:]))
    o_ref[0] = jnp.concatenate(cols, axis=-1).astype(o_ref.dtype)


def _dense_attn(q, kvs, sink, name):
    b, n, w = q.shape
    tq = min(ATTN_Q_TILE, n)
    in_specs = [pl.BlockSpec((1, tq, w), lambda i, j: (i, j, 0))]
    args = [q]
    for k, v in kvs:
        for t in (k, v):
            in_specs.append(pl.BlockSpec((1,) + t.shape[1:], lambda i, j: (i, 0, 0)))
            args.append(t)
    if sink is not None:
        in_specs.append(pl.BlockSpec(memory_space=pltpu.SMEM))
        args.append(sink)
    return pl.pallas_call(
        functools.partial(_dense_attn_kernel, n_kv=len(kvs), has_sink=sink is not None),
        out_shape=jax.ShapeDtypeStruct((b, n, w), BF16),
        grid=(b, n // tq),
        in_specs=in_specs,
        out_specs=pl.BlockSpec((1, tq, w), lambda i, j: (i, j, 0)),
        compiler_params=_params("parallel", "parallel"),
        name=name,
    )(*args)


def _win_attn_kernel(q_ref, k_ref, v_ref, kc_ref, vc_ref, sink_ref, o_ref):
    n = q_ref.shape[1]
    qb = WINDOW
    band = 3 * qb
    low = lax.broadcasted_iota(jnp.int32, (qb, LANES), 1) < HEAD_DIM
    row_minus_col = (lax.broadcasted_iota(jnp.int32, (qb, band), 0)
                     - lax.broadcasted_iota(jnp.int32, (qb, band), 1))
    kc = kc_ref[0]
    vc = vc_ref[0]
    sink = jnp.concatenate([jnp.full((qb, 1), sink_ref[h], F32) for h in (0, 2, 1, 3)], axis=0)

    def body(i, carry):
        t0 = pl.multiple_of(i * qb, qb)
        start = pl.multiple_of(jnp.clip(t0 - qb, 0, n - band), qb)
        q = q_ref[0, pl.ds(t0, qb), :]
        parts = []
        for j in range(2):
            parts.extend(_split_heads(q[:, j * LANES:(j + 1) * LANES], low))
        qs = jnp.concatenate(parts, axis=0)
        s_band = lax.dot_general(qs, k_ref[0, pl.ds(start, band), :], _NT, preferred_element_type=F32)
        s_ctx = lax.dot_general(qs, kc, _NT, preferred_element_type=F32)
        valid = jnp.abs(row_minus_col + (t0 - start)) <= WINDOW
        s_band = jnp.where(jnp.concatenate([valid] * 4, axis=0), s_band, NEG_INF)
        m = jnp.maximum(jnp.maximum(jnp.max(s_band, axis=-1, keepdims=True),
                                    jnp.max(s_ctx, axis=-1, keepdims=True)), sink)
        p_band = jnp.exp(s_band - m)
        p_ctx = jnp.exp(s_ctx - m)
        denom = (jnp.sum(p_band, axis=-1, keepdims=True) + jnp.sum(p_ctx, axis=-1, keepdims=True)
                 + jnp.exp(sink - m))
        acc = (jnp.dot(p_band.astype(BF16), v_ref[0, pl.ds(start, band), :], preferred_element_type=F32)
               + jnp.dot(p_ctx.astype(BF16), vc, preferred_element_type=F32))
        out = acc / denom
        o_ref[0, pl.ds(t0, qb), :] = jnp.concatenate(
            [jnp.where(low, out[0:qb], out[qb:2 * qb]),
             jnp.where(low, out[2 * qb:3 * qb], out[3 * qb:4 * qb])], axis=-1).astype(o_ref.dtype)
        return carry

    lax.fori_loop(0, n // qb, body, 0)


def _win_attn(q, k, v, kc, vc, sink):
    b, n, w = q.shape
    full = lambda t: pl.BlockSpec((1,) + t.shape[1:], lambda i: (i, 0, 0))
    return pl.pallas_call(
        _win_attn_kernel,
        out_shape=jax.ShapeDtypeStruct((b, n, w), BF16),
        grid=(b,),
        in_specs=[full(q), full(k), full(v), full(kc), full(vc),
                  pl.BlockSpec(memory_space=pltpu.SMEM)],
        out_specs=full(q),
        compiler_params=_params("parallel"),
        name="window_attn",
    )(q, k, v, kc, vc, sink)


def _scan_block(a_scr, u_scr, h_scr, base, col, c_in, row8, reverse):
    order = range(SUBLANES - 1, -1, -1) if reverse else range(SUBLANES)
    a_cum, h_loc = [], []
    for j in order:
        aj = a_scr[col, pl.ds(base + j, SUBLANES, stride=SUBLANES), :]
        uj = u_scr[col, pl.ds(base + j, SUBLANES, stride=SUBLANES), :]
        if a_cum:
            h_loc.append(aj * h_loc[-1] + uj)
            a_cum.append(aj * a_cum[-1])
        else:
            h_loc.append(uj)
            a_cum.append(aj)
    p, q = a_cum[-1], h_loc[-1]
    for s in (1, 2, 4):
        shift = SUBLANES - s if reverse else s
        valid = (row8 < SUBLANES - s) if reverse else (row8 >= s)
        q = jnp.where(valid, p * pltpu.roll(q, shift, 0) + q, q)
        p = jnp.where(valid, p * pltpu.roll(p, shift, 0), p)
    end = q + p * c_in
    if reverse:
        c_grp = jnp.where(row8 < SUBLANES - 1, pltpu.roll(end, SUBLANES - 1, 0), c_in)
        last = end[0:1]
    else:
        c_grp = jnp.where(row8 >= 1, pltpu.roll(end, 1, 0), c_in)
        last = end[SUBLANES - 1:SUBLANES]
    for idx, j in enumerate(order):
        h_scr[col, pl.ds(base + j, SUBLANES, stride=SUBLANES), :] = h_loc[idx] + a_cum[idx] * c_grp
    return jnp.broadcast_to(last, (SUBLANES, LANES))


def _lru_chunk(src_pad, t0, rows, d, carry, reverse, cw_ref, cb_ref, wg_ref, bg_ref, nsp,
               a_scr, u_scr, h_scr, row8):
    halo = SUBLANES
    win = src_pad[pl.ds(t0, rows + 2 * halo), :]
    xc = cb_ref[...]
    for j in range(CONV_W):
        off = j - CONV_LEFT
        tap = win if off == 0 else pltpu.roll(win, (-off) % (rows + 2 * halo), 0)
        xc = xc + tap[halo:halo + rows] * cw_ref[j:j + 1, :]
    xb = xc.astype(BF16)
    half = LRU_WIDTH // 2
    y0 = jnp.dot(xb[:, :half], wg_ref[d, 0], preferred_element_type=F32)
    y1 = jnp.dot(xb[:, half:], wg_ref[d, 1], preferred_element_type=F32)
    ya = jnp.concatenate([y0[:, :half], y1[:, :half]], axis=-1)
    yi = jnp.concatenate([y0[:, half:], y1[:, half:]], axis=-1)
    r = jax.nn.sigmoid(ya + bg_ref[2 * d:2 * d + 1, :])
    gate_i = jax.nn.sigmoid(yi + bg_ref[2 * d + 1:2 * d + 2, :])
    log_a = r * nsp
    th = jnp.tanh(log_a)
    a = jnp.exp(log_a)
    u = jnp.sqrt(-2.0 * th / (1.0 - th)) * (gate_i * xc)
    for col in range(LRU_WIDTH // LANES):
        a_scr[col, 0:rows, :] = a[:, col * LANES:(col + 1) * LANES]
        u_scr[col, 0:rows, :] = u[:, col * LANES:(col + 1) * LANES]
    n_blk = rows // SCAN_BLOCK
    blocks = range(n_blk - 1, -1, -1) if reverse else range(n_blk)
    carry = list(carry)
    for blk in blocks:
        for col in range(LRU_WIDTH // LANES):
            carry[col] = _scan_block(a_scr, u_scr, h_scr, blk * SCAN_BLOCK, col, carry[col],
                                     row8, reverse)
    return tuple(carry)


def _lru_kernel(xc_ref, gc_ref, x_ref, g_ref, cw_ref, cb_ref, wg_ref, bg_ref, lam_ref, *rest,
                with_ctx):
    if with_ctx:
        o_ref, oc_ref, xpad, xcpad, a_scr, u_scr, h_scr, hf, hfc = rest
    else:
        o_ref, xpad, xcpad, a_scr, u_scr, h_scr, hf = rest
    n = x_ref.shape[1]
    m = xc_ref.shape[1]
    halo = SUBLANES
    zeros = jnp.zeros((halo, LRU_WIDTH), F32)
    for pad, src, length in ((xpad, x_ref, n), (xcpad, xc_ref, m)):
        pad[0:halo, :] = zeros
        pad[halo:halo + length, :] = src[0]
        pad[halo + length:2 * halo + length, :] = zeros

    neg_lam = -lam_ref[...]
    softplus = jnp.maximum(neg_lam, 0.0) + jnp.log1p(jnp.exp(-jnp.abs(neg_lam)))
    nsp_all = -LRU_C * softplus
    row8 = lax.broadcasted_iota(jnp.int32, (SUBLANES, LANES), 0)
    zero_carry = tuple(jnp.zeros((SUBLANES, LANES), F32) for _ in range(LRU_WIDTH // LANES))
    n_chunks = n // LRU_CHUNK

    def chunk(src_pad, t0, rows, d, carry):
        return _lru_chunk(src_pad, t0, rows, d, carry, d == 1, cw_ref, cb_ref, wg_ref, bg_ref,
                          nsp_all[d:d + 1, :], a_scr, u_scr, h_scr, row8)

    def read_h(rows):
        return jnp.concatenate([h_scr[col, 0:rows, :] for col in range(LRU_WIDTH // LANES)], axis=-1)

    carry = chunk(xcpad, 0, m, 0, zero_carry)
    if with_ctx:
        hfc[...] = read_h(m)

    def fwd_body(ci, carry):
        t0 = pl.multiple_of(ci * LRU_CHUNK, LRU_CHUNK)
        carry = chunk(xpad, t0, LRU_CHUNK, 0, carry)
        hf[pl.ds(t0, LRU_CHUNK), :] = read_h(LRU_CHUNK)
        return carry

    lax.fori_loop(0, n_chunks, fwd_body, carry)

    carry = chunk(xcpad, 0, m, 1, zero_carry)
    if with_ctx:
        oc_ref[0] = ((hfc[...] + read_h(m)) * jax.nn.gelu(gc_ref[0])).astype(oc_ref.dtype)

    def bwd_body(ci, carry):
        t0 = pl.multiple_of((n_chunks - 1 - ci) * LRU_CHUNK, LRU_CHUNK)
        carry = chunk(xpad, t0, LRU_CHUNK, 1, carry)
        gate = jax.nn.gelu(g_ref[0, pl.ds(t0, LRU_CHUNK), :])
        o_ref[0, pl.ds(t0, LRU_CHUNK), :] = (
            (hf[pl.ds(t0, LRU_CHUNK), :] + read_h(LRU_CHUNK)) * gate).astype(o_ref.dtype)
        return carry

    lax.fori_loop(0, n_chunks, bwd_body, carry)


def _lru(xc, gc, x, g, cw, cb, wg, bg, lam, *, with_ctx):
    b, n, w = x.shape
    m = xc.shape[1]
    full = lambda t: pl.BlockSpec((1,) + t.shape[1:], lambda i: (i, 0, 0))
    out_shape = [jax.ShapeDtypeStruct((b, n, w), BF16)]
    out_specs = [pl.BlockSpec((1, n, w), lambda i: (i, 0, 0))]
    scratch = [pltpu.VMEM((n + 2 * SUBLANES, w), F32), pltpu.VMEM((m + 2 * SUBLANES, w), F32),
               *[pltpu.VMEM((w // LANES, LRU_CHUNK, LANES), F32) for _ in range(3)],
               pltpu.VMEM((n, w), F32)]
    if with_ctx:
        out_shape.append(jax.ShapeDtypeStruct((b, m, w), BF16))
        out_specs.append(pl.BlockSpec((1, m, w), lambda i: (i, 0, 0)))
        scratch.append(pltpu.VMEM((m, w), F32))
    outs = pl.pallas_call(
        functools.partial(_lru_kernel, with_ctx=with_ctx),
        out_shape=out_shape,
        grid=(b,),
        in_specs=[full(xc), full(gc), full(x), full(g), _const_spec(cw.shape), _const_spec(cb.shape),
                  _const_spec(wg.shape), _const_spec(bg.shape), _const_spec(lam.shape)],
        out_specs=out_specs,
        scratch_shapes=scratch,
        compiler_params=_params("parallel"),
        name="rglru_ctx" if with_ctx else "rglru",
    )(xc, gc, x, g, cw, cb, wg, bg, lam)
    return (outs[0], outs[1]) if with_ctx else (outs[0], None)


def _post_kernel(fa_ref, fb_ref, fc_ref, x_ref, mod_ref, wo_ref, gpost_ref, gpre_ref, w1_ref, w2_ref,
                 gpm_ref, o_ref):
    x = x_ref[0]
    mod = mod_ref[0]
    y = (jnp.dot(fa_ref[0], wo_ref[0:A_Q, :], preferred_element_type=F32)
         + jnp.dot(fb_ref[0], wo_ref[A_Q:A_Q + B_Q, :], preferred_element_type=F32)
         + jnp.dot(fc_ref[0], wo_ref[A_Q + B_Q:, :], preferred_element_type=F32))
    x1 = x + mod[2:3] * _rms(y, gpost_ref[...])
    h2 = (_rms(x1, gpre_ref[...]) * (1.0 + mod[4:5]) + mod[3:4]).astype(BF16)
    d_ff = w1_ref.shape[1]
    ff_chunk = d_ff // 4
    acc = None
    for k in range(d_ff // ff_chunk):
        hk = jnp.dot(h2, w1_ref[:, k * ff_chunk:(k + 1) * ff_chunk], preferred_element_type=F32)
        hk = jnp.square(jnp.maximum(hk, 0.0)).astype(BF16)
        part = jnp.dot(hk, w2_ref[k * ff_chunk:(k + 1) * ff_chunk, :], preferred_element_type=F32)
        acc = part if acc is None else acc + part
    o_ref[0] = x1 + mod[5:6] * _rms(acc, gpm_ref[...])


def _post(fa, fb, fc, x, mod, wo, gpost, gpre, w1, w2, gpm, name):
    b, n, d = x.shape
    tm = min(ROW_TILE, n)
    row_spec = lambda wd: pl.BlockSpec((1, tm, wd), lambda i, j: (i, j, 0))
    return pl.pallas_call(
        _post_kernel,
        out_shape=jax.ShapeDtypeStruct((b, n, d), F32),
        grid=(b, n // tm),
        in_specs=[row_spec(fa.shape[2]), row_spec(fb.shape[2]), row_spec(fc.shape[2]), row_spec(d),
                  pl.BlockSpec((1, 6, d), lambda i, j: (i, 0, 0)),
                  _const_spec(wo.shape), _const_spec(gpost.shape), _const_spec(gpre.shape),
                  _const_spec(w1.shape), _const_spec(w2.shape), _const_spec(gpm.shape)],
        out_specs=row_spec(d),
        compiler_params=_params("parallel", "parallel"),
        name=name,
    )(fa, fb, fc, x, mod, wo, gpost, gpre, w1, w2, gpm)


def _rope_tables(n):
    rows = n // GRID_W
    row = jnp.repeat(jnp.arange(rows, dtype=F32), GRID_W)
    col = jnp.tile(jnp.arange(GRID_W, dtype=F32), rows)
    half = HEAD_DIM // 2
    inv_freq = ROPE_BASE ** (-jnp.arange(0, half, 2, dtype=F32) / half)
    ang_r = row[:, None] * inv_freq
    ang_c = col[:, None] * inv_freq
    cr, sr, cc, sc = jnp.cos(ang_r), jnp.sin(ang_r), jnp.cos(ang_c), jnp.sin(ang_c)
    cos = jnp.concatenate([cr, cr, cc, cc], axis=-1)
    sin = jnp.concatenate([-sr, sr, -sc, sc], axis=-1)
    reps = LANES // HEAD_DIM
    return jnp.tile(cos, (1, reps)), jnp.tile(sin, (1, reps))


def _head_perm(base):
    idx = jnp.arange(HEAD_DIM)
    return jnp.concatenate([base + h * HEAD_DIM + idx for h in (0, 2, 1, 3)])


def _block_diag(w):
    nb, hb, _ = w.shape
    per = nb // 2
    out = jnp.zeros((2, per * hb, per * hb), w.dtype)
    for h in range(2):
        for k in range(per):
            out = out.at[h, k * hb:(k + 1) * hb, k * hb:(k + 1) * hb].set(w[h * per + k])
    return out


def kernel(x, c, ctx, c_ctx, w_mod, b_mod, g_pre_mix, g_post_mix, g_pre_mlp, g_post_mlp, w_in, g_q_a,
           g_k_a, sink_b, conv_w, conv_b, lru_w_a, lru_b_a, lru_w_i, lru_b_i, lru_lambda, w_out,
           w_mlp_in, w_mlp_out):
    b, n, d = x.shape
    depth = w_mod.shape[0]
    cos, sin = _rope_tables(n)
    seg = jnp.arange(A_Q) // HEAD_DIM
    bd = jnp.where(seg[:, None] == seg[None, :], 1.0 / HEAD_DIM, 0.0).astype(BF16)

    in_perm = jnp.concatenate([_head_perm(0), jnp.arange(A_Q, A_Q + 2 * A_KV),
                               _head_perm(A_Q + 2 * A_KV),
                               jnp.arange(A_Q + 2 * A_KV + B_Q, w_in.shape[2])])
    out_perm = jnp.concatenate([_head_perm(0), _head_perm(A_Q), jnp.arange(A_Q + B_Q, w_out.shape[1])])

    mod_rows = -(-(b + 1) // SUBLANES) * SUBLANES
    cc = jnp.zeros((mod_rows, d), F32).at[:b].set(c).at[b].set(c_ctx)

    for l in range(depth):
        last = l == depth - 1
        mod = _modulation(cc, w_mod[l].astype(BF16), b_mod[l][None, :])
        mod_lat = mod[:b].reshape(b, 6, d)
        mod_ctx = jnp.broadcast_to(mod[b].reshape(1, 6, d), (b, 6, d))

        w_in_l = w_in[l][:, in_perm].astype(BF16)
        gq = jnp.tile(g_q_a[l], A_Q_HEADS)[None, :]
        gk = jnp.tile(g_k_a[l], A_KV_HEADS)[None, :]
        gpre = g_pre_mix[l][None, :]
        qa, ka, va, qb, kb, vb, xr, gr = _inproj(x, mod_lat, gpre, w_in_l, gq, gk, bd, cos, sin, rope=True)
        qa_c, ka_c, va_c, qb_c, kb_c, vb_c, xr_c, gr_c = _inproj(
            ctx, mod_ctx, gpre, w_in_l, gq, gk, bd, cos, sin, rope=False)

        feat_a = _dense_attn(qa, [(ka_c, va_c), (ka, va)], None, "global_attn")
        feat_b = _win_attn(qb, kb, vb, kb_c, vb_c, sink_b[l])

        wg = jnp.stack([jnp.concatenate([_block_diag(lru_w_a[l, dd]), _block_diag(lru_w_i[l, dd])], axis=-1)
                        for dd in range(2)]).astype(BF16)
        bg = jnp.stack([lru_b_a[l, 0], lru_b_i[l, 0], lru_b_a[l, 1], lru_b_i[l, 1]])
        feat_c, feat_cc = _lru(xr_c, gr_c, xr, gr, conv_w[l], conv_b[l][None, :], wg, bg, lru_lambda[l],
                               with_ctx=not last)

        wo = w_out[l][out_perm].astype(BF16)
        post_w = (wo, g_post_mix[l][None, :], g_pre_mlp[l][None, :], w_mlp_in[l].astype(BF16),
                  w_mlp_out[l].astype(BF16), g_post_mlp[l][None, :])
        x = _post(feat_a, feat_b, feat_c, x, mod_lat, *post_w, name="post_mlp")
        if not last:
            feat_ac = _dense_attn(qa_c, [(ka_c, va_c)], None, "ctx_attn_a")
            feat_bc = _dense_attn(qb_c, [(kb_c, vb_c)], sink_b[l], "ctx_attn_b")
            ctx = _post(feat_ac, feat_bc, feat_cc, ctx, mod_ctx, *post_w, name="post_mlp_ctx")
    return x
```

```python
import functools

import jax
import jax.numpy as jnp
from jax import lax
from jax.experimental import pallas as pl
from jax.experimental.pallas import tpu as pltpu

F32 = jnp.float32
BF16 = jnp.bfloat16

GRID_W = 64
HEAD_DIM = 64
A_Q_HEADS = 4
A_KV_HEADS = 2
B_Q_HEADS = 4
B_KV_HEADS = 2
WINDOW = 128
LRU_WIDTH = 512
LRU_BLOCKS = 8
CONV_W = 4
CONV_LEFT = CONV_W // 2
LRU_C = 8.0
ROPE_BASE = 10000.0
EPS = 1e-6
NEG_INF = -1e30

A_Q = A_Q_HEADS * HEAD_DIM
A_KV = A_KV_HEADS * HEAD_DIM
B_Q = B_Q_HEADS * HEAD_DIM
B_KV = B_KV_HEADS * HEAD_DIM
Q_SCALE = HEAD_DIM ** -0.5

LANES = 128
SUBLANES = 8
VMEM_LIMIT = 56 * 1024 * 1024

ROW_TILE = 512
ATTN_Q_TILE = 512
ATTN_CHAIN = 512
WIN_GROUP = 4
LRU_CHUNK = 512
SCAN_BLOCK = SUBLANES * SUBLANES

_NT = (((1,), (1,)), ((), ()))


def _const_spec(shape):
    return pl.BlockSpec(shape, lambda *_: (0,) * len(shape), pipeline_mode=pl.Buffered(1))


def _params(*sem):
    return pltpu.CompilerParams(dimension_semantics=sem, vmem_limit_bytes=VMEM_LIMIT)


def _rms(x, g):
    return x * lax.rsqrt(jnp.mean(x * x, axis=-1, keepdims=True) + EPS) * g


def _mod_kernel(c_ref, w_ref, b_ref, o_ref):
    c = c_ref[...]
    act = (c * jax.nn.sigmoid(c)).astype(BF16)
    o_ref[...] = jnp.dot(act, w_ref[...], preferred_element_type=F32) + b_ref[...]


def _modulation(cc, w, b):
    rows, d = cc.shape
    n_out = w.shape[1]
    tn = n_out // 4
    return pl.pallas_call(
        _mod_kernel,
        out_shape=jax.ShapeDtypeStruct((rows, n_out), F32),
        grid=(n_out // tn,),
        in_specs=[
            pl.BlockSpec((rows, d), lambda j: (0, 0)),
            pl.BlockSpec((d, tn), lambda j: (0, j)),
            pl.BlockSpec((1, tn), lambda j: (0, j)),
        ],
        out_specs=pl.BlockSpec((rows, tn), lambda j: (0, j)),
        compiler_params=_params("arbitrary"),
        name="modulation",
    )(cc, w, b)


def _head_rms(t, g, bd):
    sq = t * t
    hi = sq.astype(BF16)
    lo = (sq - hi.astype(F32)).astype(BF16)
    ms = jnp.dot(hi, bd, preferred_element_type=F32) + jnp.dot(lo, bd, preferred_element_type=F32)
    return t * lax.rsqrt(ms + EPS) * g


def _rope(t, cos, sin, low_half):
    outs = []
    for j in range(t.shape[1] // LANES):
        tc = t[:, j * LANES:(j + 1) * LANES]
        partner = jnp.where(low_half, pltpu.roll(tc, LANES - 16, 1), pltpu.roll(tc, 16, 1))
        outs.append(tc * cos + partner * sin)
    return outs[0] if len(outs) == 1 else jnp.concatenate(outs, axis=-1)


def _inproj_kernel(x_ref, mod_ref, gpre_ref, w_ref, gq_ref, gk_ref, bd_ref, cos_ref, sin_ref,
                   qa_ref, ka_ref, va_ref, qb_ref, kb_ref, vb_ref, xr_ref, gr_ref, *, rope):
    x = x_ref[0]
    mod = mod_ref[0]
    h = _rms(x, gpre_ref[...]) * (1.0 + mod[1:2]) + mod[0:1]
    z = jnp.dot(h.astype(BF16), w_ref[...], preferred_element_type=F32)

    bd = bd_ref[...]
    qa = _head_rms(z[:, 0:A_Q], gq_ref[...], bd)
    ka = _head_rms(z[:, A_Q:A_Q + A_KV], gk_ref[...], bd[:A_KV, :A_KV])
    o = A_Q + 2 * A_KV
    qb = z[:, o:o + B_Q]
    kb = z[:, o + B_Q:o + B_Q + B_KV]
    if rope:
        cos = cos_ref[...]
        sin = sin_ref[...]
        lane = lax.broadcasted_iota(jnp.int32, cos.shape, 1)
        low_half = (lane % 32) < 16
        qa = _rope(qa, cos, sin, low_half)
        ka = _rope(ka, cos, sin, low_half)
        qb = _rope(qb, cos, sin, low_half)
        kb = _rope(kb, cos, sin, low_half)
    qa_ref[0] = (qa * Q_SCALE).astype(BF16)
    ka_ref[0] = ka.astype(BF16)
    va_ref[0] = z[:, A_Q + A_KV:o].astype(BF16)
    qb_ref[0] = (qb * Q_SCALE).astype(BF16)
    kb_ref[0] = kb.astype(BF16)
    vb_ref[0] = z[:, o + B_Q + B_KV:o + B_Q + 2 * B_KV].astype(BF16)
    o2 = o + B_Q + 2 * B_KV
    xr_ref[0] = z[:, o2:o2 + LRU_WIDTH]
    gr_ref[0] = z[:, o2 + LRU_WIDTH:o2 + 2 * LRU_WIDTH]


def _inproj(x, mod, gpre, w, gq, gk, bd, cos, sin, *, rope):
    b, n, d = x.shape
    tm = min(ROW_TILE, n)
    nt = n // tm
    widths = (A_Q, A_KV, A_KV, B_Q, B_KV, B_KV, LRU_WIDTH, LRU_WIDTH)
    dtypes = (BF16,) * 6 + (F32, F32)
    row_spec = lambda wd: pl.BlockSpec((1, tm, wd), lambda i, j: (i, j, 0))
    return pl.pallas_call(
        functools.partial(_inproj_kernel, rope=rope),
        out_shape=[jax.ShapeDtypeStruct((b, n, wd), dt) for wd, dt in zip(widths, dtypes)],
        grid=(b, nt),
        in_specs=[
            row_spec(d),
            pl.BlockSpec((1, 6, d), lambda i, j: (i, 0, 0)),
            _const_spec(gpre.shape),
            _const_spec(w.shape),
            _const_spec(gq.shape),
            _const_spec(gk.shape),
            _const_spec(bd.shape),
            pl.BlockSpec((tm, LANES), lambda i, j: (j, 0)),
            pl.BlockSpec((tm, LANES), lambda i, j: (j, 0)),
        ],
        out_specs=[row_spec(wd) for wd in widths],
        compiler_params=_params("parallel", "parallel"),
        name="inproj_rope" if rope else "inproj_ctx",
    )(x, mod, gpre, w, gq, gk, bd, cos, sin)


def _split_heads(qc, low):
    zero = jnp.zeros_like(qc)
    return jnp.where(low, qc, zero), jnp.where(low, zero, qc)


def _stack_heads(q, low):
    parts = []
    for j in range(q.shape[1] // LANES):
        parts.extend(_split_heads(q[:, j * LANES:(j + 1) * LANES], low))
    return jnp.concatenate(parts, axis=0)


def _unstack_heads(o, t, low):
    return jnp.concatenate([jnp.where(low, o[0:t], o[t:2 * t]),
                            jnp.where(low, o[2 * t:3 * t], o[3 * t:4 * t])], axis=-1)


def _with_ones(v):
    return jnp.concatenate([v, jnp.ones_like(v)], axis=-1)


def _chain_sizes(total):
    if total >= 4 * ATTN_CHAIN and total % ATTN_CHAIN == 0:
        edge = (ATTN_CHAIN // 4, 3 * ATTN_CHAIN // 4)
        return edge + (ATTN_CHAIN,) * (total // ATTN_CHAIN - 2) + edge[::-1]
    half = ATTN_CHAIN // 2
    return (half // 2,) + (half,) * (total // half - 1) + (half // 2,)


def _dense_attn_kernel(*refs, n_kv, has_sink):
    q_ref = refs[0]
    kv_refs = refs[1:1 + 2 * n_kv]
    sink_ref = refs[1 + 2 * n_kv] if has_sink else None
    o_ref = refs[-1]
    q = q_ref[0]
    tq = q.shape[0]
    low = lax.broadcasted_iota(jnp.int32, (tq, LANES), 1) < HEAD_DIM
    qs_all = _stack_heads(q, low)
    keys = [kv_refs[2 * i][0] for i in range(n_kv)]
    vals = [_with_ones(kv_refs[2 * i + 1][0]) for i in range(n_kv)]
    if has_sink:
        sink_all = jnp.concatenate([jnp.full((tq, 1), sink_ref[h], F32) for h in (0, 2, 1, 3)], axis=0)
    outs, r0 = [], 0
    for rows in _chain_sizes(4 * tq):
        qs = qs_all[r0:r0 + rows]
        scores = [lax.dot_general(qs, k, _NT, preferred_element_type=F32) for k in keys]
        s = scores[0] if n_kv == 1 else jnp.concatenate(scores, axis=-1)
        m = jnp.max(s, axis=-1, keepdims=True)
        if has_sink:
            sink = sink_all[r0:r0 + rows]
            m = jnp.maximum(m, sink)
        p = jnp.exp(s - m).astype(BF16)
        acc, c0 = None, 0
        for k, v in zip(keys, vals):
            pv = jnp.dot(p[:, c0:c0 + k.shape[0]], v, preferred_element_type=F32)
            acc = pv if acc is None else acc + pv
            c0 += k.shape[0]
        denom = acc[:, LANES:]
        if has_sink:
            denom = denom + jnp.exp(sink - m)
        outs.append(acc[:, :LANES] / denom)
        r0 += rows
    o_ref[0] = _unstack_heads(jnp.concatenate(outs, axis=0), tq, low).astype(o_ref.dtype)


def _dense_attn(q, kvs, sink, name):
    b, n, w = q.shape
    tq = min(ATTN_Q_TILE, n)
    in_specs = [pl.BlockSpec((1, tq, w), lambda i, j: (i, j, 0))]
    args = [q]
    for k, v in kvs:
        for t in (k, v):
            in_specs.append(pl.BlockSpec((1,) + t.shape[1:], lambda i, j: (i, 0, 0)))
            args.append(t)
    if sink is not None:
        in_specs.append(pl.BlockSpec(memory_space=pltpu.SMEM))
        args.append(sink)
    return pl.pallas_call(
        functools.partial(_dense_attn_kernel, n_kv=len(kvs), has_sink=sink is not None),
        out_shape=jax.ShapeDtypeStruct((b, n, w), BF16),
        grid=(b, n // tq),
        in_specs=in_specs,
        out_specs=pl.BlockSpec((1, tq, w), lambda i, j: (i, j, 0)),
        compiler_params=_params("parallel", "parallel"),
        name=name,
    )(*args)


def _win_attn_kernel(q_ref, k_ref, v_ref, kc_ref, vc_ref, sink_ref, o_ref):
    n = q_ref.shape[1]
    qb = WINDOW
    band = 3 * qb
    rows = WIN_GROUP * qb
    n_iter = n // rows
    low = lax.broadcasted_iota(jnp.int32, (qb, LANES), 1) < HEAD_DIM
    row_minus_col = (lax.broadcasted_iota(jnp.int32, (qb, band), 0)
                     - lax.broadcasted_iota(jnp.int32, (qb, band), 1))

    def band_bias(first_key_offset):
        bias = jnp.where(jnp.abs(row_minus_col + first_key_offset) <= WINDOW, 0.0, NEG_INF)
        return jnp.concatenate([bias.astype(F32)] * 4, axis=0)

    bias_first, bias_mid, bias_last = band_bias(0), band_bias(qb), band_bias(2 * qb)
    kc = kc_ref[0]
    vc = _with_ones(vc_ref[0])
    ones_band = jnp.ones((band, LANES), BF16)
    sink_blk = jnp.concatenate([jnp.full((qb, 1), sink_ref[h], F32) for h in (0, 2, 1, 3)], axis=0)
    sink = jnp.concatenate([sink_blk] * WIN_GROUP, axis=0)

    def body(it, carry):
        t0 = pl.multiple_of(it * rows, rows)
        q = q_ref[0, pl.ds(t0, rows), :]
        qs_blk = [_stack_heads(q[g * qb:(g + 1) * qb], low) for g in range(WIN_GROUP)]
        starts = [pl.multiple_of(jnp.clip(t0 + (g - 1) * qb, 0, n - band), qb) for g in range(WIN_GROUP)]
        s_band = []
        for g in range(WIN_GROUP):
            s = lax.dot_general(qs_blk[g], k_ref[0, pl.ds(starts[g], band), :], _NT,
                                preferred_element_type=F32)
            bias = bias_mid
            if g == 0:
                bias = jnp.where(it == 0, bias_first, bias)
            if g == WIN_GROUP - 1:
                bias = jnp.where(it == n_iter - 1, bias_last, bias)
            s_band.append(s + bias)
        s_ctx = lax.dot_general(jnp.concatenate(qs_blk, axis=0), kc, _NT, preferred_element_type=F32)
        s = jnp.concatenate([jnp.concatenate(s_band, axis=0), s_ctx], axis=-1)
        m = jnp.maximum(jnp.max(s, axis=-1, keepdims=True), sink)
        p = jnp.exp(s - m).astype(BF16)
        acc_band = [jnp.dot(p[g * 4 * qb:(g + 1) * 4 * qb, :band],
                            jnp.concatenate([v_ref[0, pl.ds(starts[g], band), :], ones_band], axis=-1),
                            preferred_element_type=F32) for g in range(WIN_GROUP)]
        acc = jnp.concatenate(acc_band, axis=0) + jnp.dot(p[:, band:], vc, preferred_element_type=F32)
        out = acc[:, :LANES] / (acc[:, LANES:] + jnp.exp(sink - m))
        o_ref[0, pl.ds(t0, rows), :] = jnp.concatenate(
            [_unstack_heads(out[g * 4 * qb:(g + 1) * 4 * qb], qb, low) for g in range(WIN_GROUP)],
            axis=0).astype(o_ref.dtype)
        return carry

    lax.fori_loop(0, n_iter, body, 0)


def _win_attn(q, k, v, kc, vc, sink):
    b, n, w = q.shape
    full = lambda t: pl.BlockSpec((1,) + t.shape[1:], lambda i: (i, 0, 0))
    return pl.pallas_call(
        _win_attn_kernel,
        out_shape=jax.ShapeDtypeStruct((b, n, w), BF16),
        grid=(b,),
        in_specs=[full(q), full(k), full(v), full(kc), full(vc),
                  pl.BlockSpec(memory_space=pltpu.SMEM)],
        out_specs=full(q),
        compiler_params=_params("parallel"),
        name="window_attn",
    )(q, k, v, kc, vc, sink)


def _scan_block(a_scr, u_scr, h_scr, base, col, c_in, row8, reverse):
    order = range(SUBLANES - 1, -1, -1) if reverse else range(SUBLANES)
    a_cum, h_loc = [], []
    for j in order:
        aj = a_scr[col, pl.ds(base + j, SUBLANES, stride=SUBLANES), :]
        uj = u_scr[col, pl.ds(base + j, SUBLANES, stride=SUBLANES), :]
        if a_cum:
            h_loc.append(aj * h_loc[-1] + uj)
            a_cum.append(aj * a_cum[-1])
        else:
            h_loc.append(uj)
            a_cum.append(aj)
    p, q = a_cum[-1], h_loc[-1]
    for s in (1, 2, 4):
        shift = SUBLANES - s if reverse else s
        valid = (row8 < SUBLANES - s) if reverse else (row8 >= s)
        q = jnp.where(valid, p * pltpu.roll(q, shift, 0) + q, q)
        p = jnp.where(valid, p * pltpu.roll(p, shift, 0), p)
    end = q + p * c_in
    if reverse:
        c_grp = jnp.where(row8 < SUBLANES - 1, pltpu.roll(end, SUBLANES - 1, 0), c_in)
        last = end[0:1]
    else:
        c_grp = jnp.where(row8 >= 1, pltpu.roll(end, 1, 0), c_in)
        last = end[SUBLANES - 1:SUBLANES]
    for idx, j in enumerate(order):
        h_scr[col, pl.ds(base + j, SUBLANES, stride=SUBLANES), :] = h_loc[idx] + a_cum[idx] * c_grp
    return jnp.broadcast_to(last, (SUBLANES, LANES))


def _lru_chunk(pad_ref, conv_ref, t0, rows, d, carry, reverse, cw_ref, cb_ref, wg_ref, bg_ref,
               quarter_nsp, a_scr, u_scr, h_scr, row8):
    if reverse:
        xc = conv_ref[pl.ds(t0, rows), :]
    else:
        halo = SUBLANES
        win = pad_ref[pl.ds(t0, rows + 2 * halo), :]
        xc = cb_ref[...]
        for j in range(CONV_W):
            off = j - CONV_LEFT
            tap = win if off == 0 else pltpu.roll(win, (-off) % (rows + 2 * halo), 0)
            xc = xc + tap[halo:halo + rows] * cw_ref[j:j + 1, :]
        conv_ref[pl.ds(t0, rows), :] = xc
    xb = xc.astype(BF16)
    half = LRU_WIDTH // 2
    y0 = jnp.dot(xb[:, :half], wg_ref[d, 0], preferred_element_type=F32)
    y1 = jnp.dot(xb[:, half:], wg_ref[d, 1], preferred_element_type=F32)
    ya = jnp.concatenate([y0[:, :half], y1[:, :half]], axis=-1)
    yi = jnp.concatenate([y0[:, half:], y1[:, half:]], axis=-1)
    t_r = jnp.tanh(ya + bg_ref[2 * d:2 * d + 1, :])
    t_i = jnp.tanh(yi + bg_ref[2 * d + 1:2 * d + 2, :])
    t = jnp.tanh(t_r * quarter_nsp + quarter_nsp)
    inv = 1.0 / (1.0 - t)
    a = (1.0 + t) * inv
    neg_t = -t
    root = jnp.where(neg_t > 0.0, neg_t * lax.rsqrt(neg_t), 0.0)
    u = root * inv * ((t_i + 1.0) * xc)
    for col in range(LRU_WIDTH // LANES):
        a_scr[col, 0:rows, :] = a[:, col * LANES:(col + 1) * LANES]
        u_scr[col, 0:rows, :] = u[:, col * LANES:(col + 1) * LANES]
    n_blk = rows // SCAN_BLOCK
    blocks = range(n_blk - 1, -1, -1) if reverse else range(n_blk)
    carry = list(carry)
    for blk in blocks:
        for col in range(LRU_WIDTH // LANES):
            carry[col] = _scan_block(a_scr, u_scr, h_scr, blk * SCAN_BLOCK, col, carry[col],
                                     row8, reverse)
    return tuple(carry)


def _lru_kernel(xc_ref, gc_ref, x_ref, g_ref, cw_ref, cb_ref, wg_ref, bg_ref, lam_ref, *rest,
                with_ctx):
    if with_ctx:
        o_ref, oc_ref, xpad, xcpad, xconv, xcconv, a_scr, u_scr, h_scr, hf, hfc = rest
    else:
        o_ref, xpad, xcpad, xconv, xcconv, a_scr, u_scr, h_scr, hf = rest
    n = x_ref.shape[1]
    m = xc_ref.shape[1]
    halo = SUBLANES
    zeros = jnp.zeros((halo, LRU_WIDTH), F32)
    for pad, src, length in ((xpad, x_ref, n), (xcpad, xc_ref, m)):
        pad[0:halo, :] = zeros
        pad[halo:halo + length, :] = src[0]
        pad[halo + length:2 * halo + length, :] = zeros

    neg_lam = -lam_ref[...]
    softplus = jnp.maximum(neg_lam, 0.0) + jnp.log1p(jnp.exp(-jnp.abs(neg_lam)))
    quarter_nsp = (-0.25 * LRU_C) * softplus
    row8 = lax.broadcasted_iota(jnp.int32, (SUBLANES, LANES), 0)
    zero_carry = tuple(jnp.zeros((SUBLANES, LANES), F32) for _ in range(LRU_WIDTH // LANES))
    n_chunks = n // LRU_CHUNK

    def chunk(ctx_part, t0, rows, d, carry):
        pad_ref, conv_ref = (xcpad, xcconv) if ctx_part else (xpad, xconv)
        return _lru_chunk(pad_ref, conv_ref, t0, rows, d, carry, d == 1, cw_ref, cb_ref, wg_ref, bg_ref,
                          quarter_nsp[d:d + 1, :], a_scr, u_scr, h_scr, row8)

    def read_h(rows):
        return jnp.concatenate([h_scr[col, 0:rows, :] for col in range(LRU_WIDTH // LANES)], axis=-1)

    carry = chunk(True, 0, m, 0, zero_carry)
    if with_ctx:
        hfc[...] = read_h(m)

    def fwd_body(ci, carry):
        t0 = pl.multiple_of(ci * LRU_CHUNK, LRU_CHUNK)
        carry = chunk(False, t0, LRU_CHUNK, 0, carry)
        hf[pl.ds(t0, LRU_CHUNK), :] = read_h(LRU_CHUNK)
        return carry

    lax.fori_loop(0, n_chunks, fwd_body, carry)

    carry = chunk(True, 0, m, 1, zero_carry)
    if with_ctx:
        oc_ref[0] = ((hfc[...] + read_h(m)) * jax.nn.gelu(gc_ref[0])).astype(oc_ref.dtype)

    def bwd_body(ci, carry):
        t0 = pl.multiple_of((n_chunks - 1 - ci) * LRU_CHUNK, LRU_CHUNK)
        carry = chunk(False, t0, LRU_CHUNK, 1, carry)
        gate = jax.nn.gelu(g_ref[0, pl.ds(t0, LRU_CHUNK), :])
        o_ref[0, pl.ds(t0, LRU_CHUNK), :] = (
            (hf[pl.ds(t0, LRU_CHUNK), :] + read_h(LRU_CHUNK)) * gate).astype(o_ref.dtype)
        return carry

    lax.fori_loop(0, n_chunks, bwd_body, carry)


def _lru(xc, gc, x, g, cw, cb, wg, bg, lam, *, with_ctx):
    b, n, w = x.shape
    m = xc.shape[1]
    full = lambda t: pl.BlockSpec((1,) + t.shape[1:], lambda i: (i, 0, 0))
    out_shape = [jax.ShapeDtypeStruct((b, n, w), BF16)]
    out_specs = [pl.BlockSpec((1, n, w), lambda i: (i, 0, 0))]
    scratch = [pltpu.VMEM((n + 2 * SUBLANES, w), F32), pltpu.VMEM((m + 2 * SUBLANES, w), F32),
               pltpu.VMEM((n, w), F32), pltpu.VMEM((m, w), F32),
               *[pltpu.VMEM((w // LANES, LRU_CHUNK, LANES), F32) for _ in range(3)],
               pltpu.VMEM((n, w), F32)]
    if with_ctx:
        out_shape.append(jax.ShapeDtypeStruct((b, m, w), BF16))
        out_specs.append(pl.BlockSpec((1, m, w), lambda i: (i, 0, 0)))
        scratch.append(pltpu.VMEM((m, w), F32))
    outs = pl.pallas_call(
        functools.partial(_lru_kernel, with_ctx=with_ctx),
        out_shape=out_shape,
        grid=(b,),
        in_specs=[full(xc), full(gc), full(x), full(g), _const_spec(cw.shape), _const_spec(cb.shape),
                  _const_spec(wg.shape), _const_spec(bg.shape), _const_spec(lam.shape)],
        out_specs=out_specs,
        scratch_shapes=scratch,
        compiler_params=_params("parallel"),
        name="rglru_ctx" if with_ctx else "rglru",
    )(xc, gc, x, g, cw, cb, wg, bg, lam)
    return (outs[0], outs[1]) if with_ctx else (outs[0], None)


def _post_kernel(fa_ref, fb_ref, fc_ref, x_ref, mod_ref, wo_ref, gpost_ref, gpre_ref, w1_ref, w2_ref,
                 gpm_ref, o_ref):
    x = x_ref[0]
    mod = mod_ref[0]
    y = (jnp.dot(fa_ref[0], wo_ref[0:A_Q, :], preferred_element_type=F32)
         + jnp.dot(fb_ref[0], wo_ref[A_Q:A_Q + B_Q, :], preferred_element_type=F32)
         + jnp.dot(fc_ref[0], wo_ref[A_Q + B_Q:, :], preferred_element_type=F32))
    x1 = x + mod[2:3] * _rms(y, gpost_ref[...])
    h2 = (_rms(x1, gpre_ref[...]) * (1.0 + mod[4:5]) + mod[3:4]).astype(BF16)
    d_ff = w1_ref.shape[1]
    ff_chunk = d_ff // 4
    acc = None
    for k in range(d_ff // ff_chunk):
        hk = jnp.dot(h2, w1_ref[:, k * ff_chunk:(k + 1) * ff_chunk], preferred_element_type=F32)
        hk = jnp.square(jnp.maximum(hk, 0.0)).astype(BF16)
        part = jnp.dot(hk, w2_ref[k * ff_chunk:(k + 1) * ff_chunk, :], preferred_element_type=F32)
        acc = part if acc is None else acc + part
    o_ref[0] = x1 + mod[5:6] * _rms(acc, gpm_ref[...])


def _post(fa, fb, fc, x, mod, wo, gpost, gpre, w1, w2, gpm, name):
    b, n, d = x.shape
    tm = min(ROW_TILE, n)
    row_spec = lambda wd: pl.BlockSpec((1, tm, wd), lambda i, j: (i, j, 0))
    return pl.pallas_call(
        _post_kernel,
        out_shape=jax.ShapeDtypeStruct((b, n, d), F32),
        grid=(b, n // tm),
        in_specs=[row_spec(fa.shape[2]), row_spec(fb.shape[2]), row_spec(fc.shape[2]), row_spec(d),
                  pl.BlockSpec((1, 6, d), lambda i, j: (i, 0, 0)),
                  _const_spec(wo.shape), _const_spec(gpost.shape), _const_spec(gpre.shape),
                  _const_spec(w1.shape), _const_spec(w2.shape), _const_spec(gpm.shape)],
        out_specs=row_spec(d),
        compiler_params=_params("parallel", "parallel"),
        name=name,
    )(fa, fb, fc, x, mod, wo, gpost, gpre, w1, w2, gpm)


def _rope_tables(n):
    rows = n // GRID_W
    row = jnp.repeat(jnp.arange(rows, dtype=F32), GRID_W)
    col = jnp.tile(jnp.arange(GRID_W, dtype=F32), rows)
    half = HEAD_DIM // 2
    inv_freq = ROPE_BASE ** (-jnp.arange(0, half, 2, dtype=F32) / half)
    ang_r = row[:, None] * inv_freq
    ang_c = col[:, None] * inv_freq
    cr, sr, cc, sc = jnp.cos(ang_r), jnp.sin(ang_r), jnp.cos(ang_c), jnp.sin(ang_c)
    cos = jnp.concatenate([cr, cr, cc, cc], axis=-1)
    sin = jnp.concatenate([-sr, sr, -sc, sc], axis=-1)
    reps = LANES // HEAD_DIM
    return jnp.tile(cos, (1, reps)), jnp.tile(sin, (1, reps))


def _head_perm(base):
    idx = jnp.arange(HEAD_DIM)
    return jnp.concatenate([base + h * HEAD_DIM + idx for h in (0, 2, 1, 3)])


def _block_diag(w):
    nb, hb, _ = w.shape
    per = nb // 2
    out = jnp.zeros((2, per * hb, per * hb), w.dtype)
    for h in range(2):
        for k in range(per):
            out = out.at[h, k * hb:(k + 1) * hb, k * hb:(k + 1) * hb].set(w[h * per + k])
    return out


def kernel(x, c, ctx, c_ctx, w_mod, b_mod, g_pre_mix, g_post_mix, g_pre_mlp, g_post_mlp, w_in, g_q_a,
           g_k_a, sink_b, conv_w, conv_b, lru_w_a, lru_b_a, lru_w_i, lru_b_i, lru_lambda, w_out,
           w_mlp_in, w_mlp_out):
    b, n, d = x.shape
    depth = w_mod.shape[0]
    cos, sin = _rope_tables(n)
    seg = jnp.arange(A_Q) // HEAD_DIM
    bd = jnp.where(seg[:, None] == seg[None, :], 1.0 / HEAD_DIM, 0.0).astype(BF16)

    in_perm = jnp.concatenate([_head_perm(0), jnp.arange(A_Q, A_Q + 2 * A_KV),
                               _head_perm(A_Q + 2 * A_KV),
                               jnp.arange(A_Q + 2 * A_KV + B_Q, w_in.shape[2])])
    out_perm = jnp.concatenate([_head_perm(0), _head_perm(A_Q), jnp.arange(A_Q + B_Q, w_out.shape[1])])

    mod_rows = -(-(b + 1) // SUBLANES) * SUBLANES
    cc = jnp.zeros((mod_rows, d), F32).at[:b].set(c).at[b].set(c_ctx)

    for l in range(depth):
        last = l == depth - 1
        mod = _modulation(cc, w_mod[l].astype(BF16), b_mod[l][None, :])
        mod_lat = mod[:b].reshape(b, 6, d)
        mod_ctx = jnp.broadcast_to(mod[b].reshape(1, 6, d), (b, 6, d))

        w_in_l = w_in[l][:, in_perm].astype(BF16)
        gq = jnp.tile(g_q_a[l], A_Q_HEADS)[None, :]
        gk = jnp.tile(g_k_a[l], A_KV_HEADS)[None, :]
        gpre = g_pre_mix[l][None, :]
        qa, ka, va, qb, kb, vb, xr, gr = _inproj(x, mod_lat, gpre, w_in_l, gq, gk, bd, cos, sin, rope=True)
        qa_c, ka_c, va_c, qb_c, kb_c, vb_c, xr_c, gr_c = _inproj(
            ctx, mod_ctx, gpre, w_in_l, gq, gk, bd, cos, sin, rope=False)

        feat_a = _dense_attn(qa, [(ka_c, va_c), (ka, va)], None, "global_attn")
        feat_b = _win_attn(qb, kb, vb, kb_c, vb_c, sink_b[l])

        wg = (0.5 * jnp.stack([jnp.concatenate([_block_diag(lru_w_a[l, dd]), _block_diag(lru_w_i[l, dd])],
                                               axis=-1) for dd in range(2)])).astype(BF16)
        bg = 0.5 * jnp.stack([lru_b_a[l, 0], lru_b_i[l, 0], lru_b_a[l, 1], lru_b_i[l, 1]])
        feat_c, feat_cc = _lru(xr_c, gr_c, xr, gr, conv_w[l], conv_b[l][None, :], wg, bg, lru_lambda[l],
                               with_ctx=not last)

        wo = w_out[l][out_perm].astype(BF16)
        post_w = (wo, g_post_mix[l][None, :], g_pre_mlp[l][None, :], w_mlp_in[l].astype(BF16),
                  w_mlp_out[l].astype(BF16), g_post_mlp[l][None, :])
        x = _post(feat_a, feat_b, feat_c, x, mod_lat, *post_w, name="post_mlp")
        if not last:
            feat_ac = _dense_attn(qa_c, [(ka_c, va_c)], None, "ctx_attn_a")
            feat_bc = _dense_attn(qb_c, [(kb_c, vb_c)], sink_b[l], "ctx_attn_b")
            ctx = _post(feat_ac, feat_bc, feat_cc, ctx, mod_ctx, *post_w, name="post_mlp_ctx")
    return x
```

```python
import functools

import jax
import jax.numpy as jnp
from jax import lax
from jax.experimental import pallas as pl
from jax.experimental.pallas import tpu as pltpu

F32 = jnp.float32
BF16 = jnp.bfloat16

GRID_W = 64
HEAD_DIM = 64
A_Q_HEADS = 4
A_KV_HEADS = 2
B_Q_HEADS = 4
B_KV_HEADS = 2
WINDOW = 128
LRU_WIDTH = 512
LRU_BLOCKS = 8
CONV_W = 4
CONV_LEFT = CONV_W // 2
LRU_C = 8.0
ROPE_BASE = 10000.0
EPS = 1e-6
NEG_INF = -1e30

A_Q = A_Q_HEADS * HEAD_DIM
A_KV = A_KV_HEADS * HEAD_DIM
B_Q = B_Q_HEADS * HEAD_DIM
B_KV = B_KV_HEADS * HEAD_DIM
Q_SCALE = HEAD_DIM ** -0.5

LANES = 128
SUBLANES = 8
VMEM_LIMIT = 56 * 1024 * 1024

ROW_TILE = 512
SUB_TILE = 256
POST_SUB_TILE = 512
ATTN_Q_TILE = 1024
ATTN_CHAIN = 512
WIN_GROUP = 4
LRU_CHUNK = 512
SCAN_BLOCK = SUBLANES * SUBLANES

_NT = (((1,), (1,)), ((), ()))


def _const_spec(shape):
    return pl.BlockSpec(shape, lambda *_: (0,) * len(shape), pipeline_mode=pl.Buffered(1))


def _params(*sem):
    return pltpu.CompilerParams(dimension_semantics=sem, vmem_limit_bytes=VMEM_LIMIT)


def _rms(x, g):
    return x * lax.rsqrt(jnp.mean(x * x, axis=-1, keepdims=True) + EPS) * g


def _mod_kernel(c_ref, w_ref, b_ref, o_ref):
    c = c_ref[...]
    act = (c * jax.nn.sigmoid(c)).astype(BF16)
    o_ref[...] = jnp.dot(act, w_ref[...], preferred_element_type=F32) + b_ref[...]


def _modulation(cc, w, b):
    rows, d = cc.shape
    n_out = w.shape[1]
    tn = n_out // 4
    return pl.pallas_call(
        _mod_kernel,
        out_shape=jax.ShapeDtypeStruct((rows, n_out), F32),
        grid=(n_out // tn,),
        in_specs=[
            pl.BlockSpec((rows, d), lambda j: (0, 0)),
            pl.BlockSpec((d, tn), lambda j: (0, j)),
            pl.BlockSpec((1, tn), lambda j: (0, j)),
        ],
        out_specs=pl.BlockSpec((rows, tn), lambda j: (0, j)),
        compiler_params=_params("arbitrary"),
        name="modulation",
    )(cc, w, b)


def _head_rms(t, g, bd):
    sq = t * t
    hi = sq.astype(BF16)
    lo = (sq - hi.astype(F32)).astype(BF16)
    ms = jnp.dot(hi, bd, preferred_element_type=F32) + jnp.dot(lo, bd, preferred_element_type=F32)
    return t * lax.rsqrt(ms + EPS) * g


def _rope(t, cos, sin, low_half):
    outs = []
    for j in range(t.shape[1] // LANES):
        tc = t[:, j * LANES:(j + 1) * LANES]
        partner = jnp.where(low_half, pltpu.roll(tc, LANES - 16, 1), pltpu.roll(tc, 16, 1))
        outs.append(tc * cos + partner * sin)
    return outs[0] if len(outs) == 1 else jnp.concatenate(outs, axis=-1)


def _inproj_kernel(x_ref, mod_ref, gpre_ref, w_ref, gq_ref, gk_ref, bd_ref, cos_ref, sin_ref,
                   qa_ref, ka_ref, va_ref, qb_ref, kb_ref, vb_ref, xr_ref, gr_ref, *, rope):
    mod = mod_ref[0]
    bd = bd_ref[...]
    tm = x_ref.shape[1]
    sub = min(SUB_TILE, tm)
    for r0 in range(0, tm, sub):
        rows = slice(r0, r0 + sub)
        h = _rms(x_ref[0, rows, :], gpre_ref[...]) * (1.0 + mod[1:2]) + mod[0:1]
        z = jnp.dot(h.astype(BF16), w_ref[...], preferred_element_type=F32)
        qa = _head_rms(z[:, 0:A_Q], gq_ref[...], bd)
        ka = _head_rms(z[:, A_Q:A_Q + A_KV], gk_ref[...], bd[:A_KV, :A_KV])
        o = A_Q + 2 * A_KV
        qb = z[:, o:o + B_Q]
        kb = z[:, o + B_Q:o + B_Q + B_KV]
        if rope:
            cos = cos_ref[rows, :]
            sin = sin_ref[rows, :]
            lane = lax.broadcasted_iota(jnp.int32, cos.shape, 1)
            low_half = (lane % 32) < 16
            qa = _rope(qa, cos, sin, low_half)
            ka = _rope(ka, cos, sin, low_half)
            qb = _rope(qb, cos, sin, low_half)
            kb = _rope(kb, cos, sin, low_half)
        qa_ref[0, rows, :] = (qa * Q_SCALE).astype(BF16)
        ka_ref[0, rows, :] = ka.astype(BF16)
        va_ref[0, rows, :] = z[:, A_Q + A_KV:o].astype(BF16)
        qb_ref[0, rows, :] = (qb * Q_SCALE).astype(BF16)
        kb_ref[0, rows, :] = kb.astype(BF16)
        vb_ref[0, rows, :] = z[:, o + B_Q + B_KV:o + B_Q + 2 * B_KV].astype(BF16)
        o2 = o + B_Q + 2 * B_KV
        xr_ref[0, rows, :] = z[:, o2:o2 + LRU_WIDTH]
        gr_ref[0, rows, :] = z[:, o2 + LRU_WIDTH:o2 + 2 * LRU_WIDTH]


def _inproj(x, mod, gpre, w, gq, gk, bd, cos, sin, *, rope):
    b, n, d = x.shape
    tm = min(ROW_TILE, n)
    nt = n // tm
    widths = (A_Q, A_KV, A_KV, B_Q, B_KV, B_KV, LRU_WIDTH, LRU_WIDTH)
    dtypes = (BF16,) * 6 + (F32, F32)
    row_spec = lambda wd: pl.BlockSpec((1, tm, wd), lambda i, j: (i, j, 0))
    return pl.pallas_call(
        functools.partial(_inproj_kernel, rope=rope),
        out_shape=[jax.ShapeDtypeStruct((b, n, wd), dt) for wd, dt in zip(widths, dtypes)],
        grid=(b, nt),
        in_specs=[
            row_spec(d),
            pl.BlockSpec((1, 6, d), lambda i, j: (i, 0, 0)),
            _const_spec(gpre.shape),
            _const_spec(w.shape),
            _const_spec(gq.shape),
            _const_spec(gk.shape),
            _const_spec(bd.shape),
            pl.BlockSpec((tm, LANES), lambda i, j: (j, 0)),
            pl.BlockSpec((tm, LANES), lambda i, j: (j, 0)),
        ],
        out_specs=[row_spec(wd) for wd in widths],
        compiler_params=_params("parallel", "parallel"),
        name="inproj_rope" if rope else "inproj_ctx",
    )(x, mod, gpre, w, gq, gk, bd, cos, sin)


def _split_heads(qc, low):
    zero = jnp.zeros_like(qc)
    return jnp.where(low, qc, zero), jnp.where(low, zero, qc)


def _stack_heads(q, low):
    parts = []
    for j in range(q.shape[1] // LANES):
        parts.extend(_split_heads(q[:, j * LANES:(j + 1) * LANES], low))
    return jnp.concatenate(parts, axis=0)


def _unstack_heads(o, t, low):
    return jnp.concatenate([jnp.where(low, o[0:t], o[t:2 * t]),
                            jnp.where(low, o[2 * t:3 * t], o[3 * t:4 * t])], axis=-1)


def _with_ones(v):
    return jnp.concatenate([v, jnp.ones_like(v)], axis=-1)


def _chain_sizes(total):
    assert total % ATTN_CHAIN == 0
    half = ATTN_CHAIN // 2
    return (half,) + (ATTN_CHAIN,) * (total // ATTN_CHAIN - 1) + (half,)


def _dense_attn_kernel(*refs, n_kv, has_sink):
    q_ref = refs[0]
    kv_refs = refs[1:1 + 2 * n_kv]
    sink_ref = refs[1 + 2 * n_kv] if has_sink else None
    o_ref = refs[-1]
    q = q_ref[0]
    tq = q.shape[0]
    low = lax.broadcasted_iota(jnp.int32, (tq, LANES), 1) < HEAD_DIM
    qs_all = _stack_heads(q, low)
    keys = [kv_refs[2 * i][0] for i in range(n_kv)]
    vals = [_with_ones(kv_refs[2 * i + 1][0]) for i in range(n_kv)]
    if has_sink:
        sink_all = jnp.concatenate([jnp.full((tq, 1), sink_ref[h], F32) for h in (0, 2, 1, 3)], axis=0)
    outs, r0 = [], 0
    for rows in _chain_sizes(4 * tq):
        qs = qs_all[r0:r0 + rows]
        scores = [lax.dot_general(qs, k, _NT, preferred_element_type=F32) for k in keys]
        s = scores[0] if n_kv == 1 else jnp.concatenate(scores, axis=-1)
        m = jnp.max(s, axis=-1, keepdims=True)
        if has_sink:
            sink = sink_all[r0:r0 + rows]
            m = jnp.maximum(m, sink)
        p = jnp.exp(s - m).astype(BF16)
        acc, c0 = None, 0
        for k, v in zip(keys, vals):
            pv = jnp.dot(p[:, c0:c0 + k.shape[0]], v, preferred_element_type=F32)
            acc = pv if acc is None else acc + pv
            c0 += k.shape[0]
        denom = acc[:, LANES:]
        if has_sink:
            denom = denom + jnp.exp(sink - m)
        outs.append(acc[:, :LANES] / denom)
        r0 += rows
    o_ref[0] = _unstack_heads(jnp.concatenate(outs, axis=0), tq, low).astype(o_ref.dtype)


def _dense_attn(q, kvs, sink, name):
    b, n, w = q.shape
    tq = min(ATTN_Q_TILE, n)
    in_specs = [pl.BlockSpec((1, tq, w), lambda i, j: (i, j, 0))]
    args = [q]
    for k, v in kvs:
        for t in (k, v):
            in_specs.append(pl.BlockSpec((1,) + t.shape[1:], lambda i, j: (i, 0, 0)))
            args.append(t)
    if sink is not None:
        in_specs.append(pl.BlockSpec(memory_space=pltpu.SMEM))
        args.append(sink)
    return pl.pallas_call(
        functools.partial(_dense_attn_kernel, n_kv=len(kvs), has_sink=sink is not None),
        out_shape=jax.ShapeDtypeStruct((b, n, w), BF16),
        grid=(b, n // tq),
        in_specs=in_specs,
        out_specs=pl.BlockSpec((1, tq, w), lambda i, j: (i, j, 0)),
        compiler_params=_params("parallel", "parallel"),
        name=name,
    )(*args)


def _win_attn_kernel(q_ref, k_ref, v_ref, kc_ref, vc_ref, sink_ref, o_ref):
    n = q_ref.shape[1]
    qb = WINDOW
    band = 3 * qb
    rows = WIN_GROUP * qb
    n_iter = n // rows
    low = lax.broadcasted_iota(jnp.int32, (qb, LANES), 1) < HEAD_DIM
    row_minus_col = (lax.broadcasted_iota(jnp.int32, (qb, band), 0)
                     - lax.broadcasted_iota(jnp.int32, (qb, band), 1))

    def band_bias(first_key_offset):
        bias = jnp.where(jnp.abs(row_minus_col + first_key_offset) <= WINDOW, 0.0, NEG_INF)
        return jnp.concatenate([bias.astype(F32)] * 4, axis=0)

    bias_first, bias_mid, bias_last = band_bias(0), band_bias(qb), band_bias(2 * qb)
    kc = kc_ref[0]
    vc = _with_ones(vc_ref[0])
    ones_band = jnp.ones((band, LANES), BF16)
    sink_blk = jnp.concatenate([jnp.full((qb, 1), sink_ref[h], F32) for h in (0, 2, 1, 3)], axis=0)
    sink = jnp.concatenate([sink_blk] * WIN_GROUP, axis=0)

    def body(it, carry):
        t0 = pl.multiple_of(it * rows, rows)
        q = q_ref[0, pl.ds(t0, rows), :]
        qs_blk = [_stack_heads(q[g * qb:(g + 1) * qb], low) for g in range(WIN_GROUP)]
        starts = [pl.multiple_of(jnp.clip(t0 + (g - 1) * qb, 0, n - band), qb) for g in range(WIN_GROUP)]
        s_ctx = lax.dot_general(jnp.concatenate(qs_blk, axis=0), kc, _NT, preferred_element_type=F32)
        s_band = []
        for g in range(WIN_GROUP):
            s = lax.dot_general(qs_blk[g], k_ref[0, pl.ds(starts[g], band), :], _NT,
                                preferred_element_type=F32)
            bias = bias_mid
            if g == 0:
                bias = jnp.where(it == 0, bias_first, bias)
            if g == WIN_GROUP - 1:
                bias = jnp.where(it == n_iter - 1, bias_last, bias)
            s_band.append(s + bias)
        s = jnp.concatenate([jnp.concatenate(s_band, axis=0), s_ctx], axis=-1)
        m = jnp.maximum(jnp.max(s, axis=-1, keepdims=True), sink)
        p = jnp.exp(s - m).astype(BF16)
        acc_band = [jnp.dot(p[g * 4 * qb:(g + 1) * 4 * qb, :band],
                            jnp.concatenate([v_ref[0, pl.ds(starts[g], band), :], ones_band], axis=-1),
                            preferred_element_type=F32) for g in range(WIN_GROUP)]
        acc = jnp.concatenate(acc_band, axis=0) + jnp.dot(p[:, band:], vc, preferred_element_type=F32)
        out = acc[:, :LANES] / (acc[:, LANES:] + jnp.exp(sink - m))
        o_ref[0, pl.ds(t0, rows), :] = jnp.concatenate(
            [_unstack_heads(out[g * 4 * qb:(g + 1) * 4 * qb], qb, low) for g in range(WIN_GROUP)],
            axis=0).astype(o_ref.dtype)
        return carry

    lax.fori_loop(0, n_iter, body, 0)


def _win_attn(q, k, v, kc, vc, sink):
    b, n, w = q.shape
    full = lambda t: pl.BlockSpec((1,) + t.shape[1:], lambda i: (i, 0, 0))
    return pl.pallas_call(
        _win_attn_kernel,
        out_shape=jax.ShapeDtypeStruct((b, n, w), BF16),
        grid=(b,),
        in_specs=[full(q), full(k), full(v), full(kc), full(vc),
                  pl.BlockSpec(memory_space=pltpu.SMEM)],
        out_specs=full(q),
        compiler_params=_params("parallel"),
        name="window_attn",
    )(q, k, v, kc, vc, sink)


def _scan_block(a_scr, u_scr, h_scr, base, col, c_in, row8, reverse):
    order = range(SUBLANES - 1, -1, -1) if reverse else range(SUBLANES)
    a_cum, h_loc = [], []
    for j in order:
        aj = a_scr[col, base + j * SUBLANES:base + (j + 1) * SUBLANES, :]
        uj = u_scr[col, base + j * SUBLANES:base + (j + 1) * SUBLANES, :]
        if a_cum:
            h_loc.append(aj * h_loc[-1] + uj)
            a_cum.append(aj * a_cum[-1])
        else:
            h_loc.append(uj)
            a_cum.append(aj)
    p, q = a_cum[-1], h_loc[-1]
    for s in (1, 2, 4):
        shift = SUBLANES - s if reverse else s
        valid = (row8 < SUBLANES - s) if reverse else (row8 >= s)
        q = jnp.where(valid, p * pltpu.roll(q, shift, 0) + q, q)
        p = jnp.where(valid, p * pltpu.roll(p, shift, 0), p)
    end = q + p * c_in
    if reverse:
        c_grp = jnp.where(row8 < SUBLANES - 1, pltpu.roll(end, SUBLANES - 1, 0), c_in)
        last = end[0:1]
    else:
        c_grp = jnp.where(row8 >= 1, pltpu.roll(end, 1, 0), c_in)
        last = end[SUBLANES - 1:SUBLANES]
    for idx, j in enumerate(order):
        h_scr[col, pl.ds(base + j, SUBLANES, stride=SUBLANES), :] = h_loc[idx] + a_cum[idx] * c_grp
    return jnp.broadcast_to(last, (SUBLANES, LANES))


def _lru_chunk(pad_ref, conv_ref, t0, rows, d, carry, reverse, cw_ref, cb_ref, wg_ref, bg_ref,
               quarter_nsp, a_scr, u_scr, h_scr, row8):
    if reverse:
        xc = conv_ref[pl.ds(t0, rows), :]
    else:
        halo = SUBLANES
        win = pad_ref[pl.ds(t0, rows + 2 * halo), :]
        xc = cb_ref[...]
        for j in range(CONV_W):
            off = j - CONV_LEFT
            tap = win if off == 0 else pltpu.roll(win, (-off) % (rows + 2 * halo), 0)
            xc = xc + tap[halo:halo + rows] * cw_ref[j:j + 1, :]
        conv_ref[pl.ds(t0, rows), :] = xc
    xb = xc.astype(BF16)
    half = LRU_WIDTH // 2
    y0 = jnp.dot(xb[:, :half], wg_ref[d, 0], preferred_element_type=F32)
    y1 = jnp.dot(xb[:, half:], wg_ref[d, 1], preferred_element_type=F32)
    ya = jnp.concatenate([y0[:, :half], y1[:, :half]], axis=-1)
    yi = jnp.concatenate([y0[:, half:], y1[:, half:]], axis=-1)
    t_r = jnp.tanh(ya + bg_ref[2 * d:2 * d + 1, :])
    t_i = jnp.tanh(yi + bg_ref[2 * d + 1:2 * d + 2, :])
    t = jnp.tanh(t_r * quarter_nsp + quarter_nsp)
    inv = 1.0 / (1.0 - t)
    a = (1.0 + t) * inv
    neg_t = -t
    root = jnp.where(neg_t > 0.0, neg_t * lax.rsqrt(neg_t), 0.0)
    u = root * inv * ((t_i + 1.0) * xc)
    for col in range(LRU_WIDTH // LANES):
        for i in range(rows // SUBLANES):
            blk, g = divmod(i, SUBLANES)
            dst = pl.ds(blk * SCAN_BLOCK + g, SUBLANES, stride=SUBLANES)
            src = (slice(i * SUBLANES, (i + 1) * SUBLANES), slice(col * LANES, (col + 1) * LANES))
            a_scr[col, dst, :] = a[src]
            u_scr[col, dst, :] = u[src]
    n_blk = rows // SCAN_BLOCK
    blocks = range(n_blk - 1, -1, -1) if reverse else range(n_blk)
    carry = list(carry)
    for blk in blocks:
        for col in range(LRU_WIDTH // LANES):
            carry[col] = _scan_block(a_scr, u_scr, h_scr, blk * SCAN_BLOCK, col, carry[col],
                                     row8, reverse)
    return tuple(carry)


def _lru_kernel(xc_ref, gc_ref, x_ref, g_ref, cw_ref, cb_ref, wg_ref, bg_ref, lam_ref, *rest,
                with_ctx):
    if with_ctx:
        o_ref, oc_ref, xpad, xcpad, xconv, xcconv, a_scr, u_scr, h_scr, hf, hfc = rest
    else:
        o_ref, xpad, xcpad, xconv, xcconv, a_scr, u_scr, h_scr, hf = rest
    n = x_ref.shape[1]
    m = xc_ref.shape[1]
    halo = SUBLANES
    zeros = jnp.zeros((halo, LRU_WIDTH), F32)
    for pad, src, length in ((xpad, x_ref, n), (xcpad, xc_ref, m)):
        pad[0:halo, :] = zeros
        pad[halo:halo + length, :] = src[0]
        pad[halo + length:2 * halo + length, :] = zeros

    neg_lam = -lam_ref[...]
    softplus = jnp.maximum(neg_lam, 0.0) + jnp.log1p(jnp.exp(-jnp.abs(neg_lam)))
    quarter_nsp = (-0.25 * LRU_C) * softplus
    row8 = lax.broadcasted_iota(jnp.int32, (SUBLANES, LANES), 0)
    zero_carry = tuple(jnp.zeros((SUBLANES, LANES), F32) for _ in range(LRU_WIDTH // LANES))
    n_chunks = n // LRU_CHUNK

    def chunk(ctx_part, t0, rows, d, carry):
        pad_ref, conv_ref = (xcpad, xcconv) if ctx_part else (xpad, xconv)
        return _lru_chunk(pad_ref, conv_ref, t0, rows, d, carry, d == 1, cw_ref, cb_ref, wg_ref, bg_ref,
                          quarter_nsp[d:d + 1, :], a_scr, u_scr, h_scr, row8)

    def read_h(rows):
        return jnp.concatenate([h_scr[col, 0:rows, :] for col in range(LRU_WIDTH // LANES)], axis=-1)

    carry = chunk(True, 0, m, 0, zero_carry)
    if with_ctx:
        hfc[...] = read_h(m)

    def fwd_body(ci, carry):
        t0 = pl.multiple_of(ci * LRU_CHUNK, LRU_CHUNK)
        carry = chunk(False, t0, LRU_CHUNK, 0, carry)
        hf[pl.ds(t0, LRU_CHUNK), :] = read_h(LRU_CHUNK)
        return carry

    lax.fori_loop(0, n_chunks, fwd_body, carry)

    carry = chunk(True, 0, m, 1, zero_carry)
    if with_ctx:
        oc_ref[0] = ((hfc[...] + read_h(m)) * jax.nn.gelu(gc_ref[0])).astype(oc_ref.dtype)

    def bwd_body(ci, carry):
        t0 = pl.multiple_of((n_chunks - 1 - ci) * LRU_CHUNK, LRU_CHUNK)
        carry = chunk(False, t0, LRU_CHUNK, 1, carry)
        gate = jax.nn.gelu(g_ref[0, pl.ds(t0, LRU_CHUNK), :])
        o_ref[0, pl.ds(t0, LRU_CHUNK), :] = (
            (hf[pl.ds(t0, LRU_CHUNK), :] + read_h(LRU_CHUNK)) * gate).astype(o_ref.dtype)
        return carry

    lax.fori_loop(0, n_chunks, bwd_body, carry)


def _lru(xc, gc, x, g, cw, cb, wg, bg, lam, *, with_ctx):
    b, n, w = x.shape
    m = xc.shape[1]
    full = lambda t: pl.BlockSpec((1,) + t.shape[1:], lambda i: (i, 0, 0))
    out_shape = [jax.ShapeDtypeStruct((b, n, w), BF16)]
    out_specs = [pl.BlockSpec((1, n, w), lambda i: (i, 0, 0))]
    scratch = [pltpu.VMEM((n + 2 * SUBLANES, w), F32), pltpu.VMEM((m + 2 * SUBLANES, w), F32),
               pltpu.VMEM((n, w), F32), pltpu.VMEM((m, w), F32),
               *[pltpu.VMEM((w // LANES, LRU_CHUNK, LANES), F32) for _ in range(3)],
               pltpu.VMEM((n, w), F32)]
    if with_ctx:
        out_shape.append(jax.ShapeDtypeStruct((b, m, w), BF16))
        out_specs.append(pl.BlockSpec((1, m, w), lambda i: (i, 0, 0)))
        scratch.append(pltpu.VMEM((m, w), F32))
    outs = pl.pallas_call(
        functools.partial(_lru_kernel, with_ctx=with_ctx),
        out_shape=out_shape,
        grid=(b,),
        in_specs=[full(xc), full(gc), full(x), full(g), _const_spec(cw.shape), _const_spec(cb.shape),
                  _const_spec(wg.shape), _const_spec(bg.shape), _const_spec(lam.shape)],
        out_specs=out_specs,
        scratch_shapes=scratch,
        compiler_params=_params("parallel"),
        name="rglru_ctx" if with_ctx else "rglru",
    )(xc, gc, x, g, cw, cb, wg, bg, lam)
    return (outs[0], outs[1]) if with_ctx else (outs[0], None)


def _post_kernel(fa_ref, fb_ref, fc_ref, x_ref, mod_ref, wo_ref, gpost_ref, gpre_ref, w1_ref, w2_ref,
                 gpm_ref, o_ref):
    mod = mod_ref[0]
    tm = x_ref.shape[1]
    sub = min(POST_SUB_TILE, tm)
    d_ff = w1_ref.shape[1]
    ff_chunk = d_ff // 4
    for r0 in range(0, tm, sub):
        rows = slice(r0, r0 + sub)
        y = (jnp.dot(fa_ref[0, rows, :], wo_ref[0:A_Q, :], preferred_element_type=F32)
             + jnp.dot(fb_ref[0, rows, :], wo_ref[A_Q:A_Q + B_Q, :], preferred_element_type=F32)
             + jnp.dot(fc_ref[0, rows, :], wo_ref[A_Q + B_Q:, :], preferred_element_type=F32))
        x1 = x_ref[0, rows, :] + mod[2:3] * _rms(y, gpost_ref[...])
        h2 = (_rms(x1, gpre_ref[...]) * (1.0 + mod[4:5]) + mod[3:4]).astype(BF16)
        acc = None
        for k in range(d_ff // ff_chunk):
            hk = jnp.dot(h2, w1_ref[:, k * ff_chunk:(k + 1) * ff_chunk], preferred_element_type=F32)
            hk = jnp.square(jnp.maximum(hk, 0.0)).astype(BF16)
            part = jnp.dot(hk, w2_ref[k * ff_chunk:(k + 1) * ff_chunk, :], preferred_element_type=F32)
            acc = part if acc is None else acc + part
        o_ref[0, rows, :] = x1 + mod[5:6] * _rms(acc, gpm_ref[...])


def _post(fa, fb, fc, x, mod, wo, gpost, gpre, w1, w2, gpm, name):
    b, n, d = x.shape
    tm = min(ROW_TILE, n)
    row_spec = lambda wd: pl.BlockSpec((1, tm, wd), lambda i, j: (i, j, 0))
    return pl.pallas_call(
        _post_kernel,
        out_shape=jax.ShapeDtypeStruct((b, n, d), F32),
        grid=(b, n // tm),
        in_specs=[row_spec(fa.shape[2]), row_spec(fb.shape[2]), row_spec(fc.shape[2]), row_spec(d),
                  pl.BlockSpec((1, 6, d), lambda i, j: (i, 0, 0)),
                  _const_spec(wo.shape), _const_spec(gpost.shape), _const_spec(gpre.shape),
                  _const_spec(w1.shape), _const_spec(w2.shape), _const_spec(gpm.shape)],
        out_specs=row_spec(d),
        compiler_params=_params("parallel", "parallel"),
        name=name,
    )(fa, fb, fc, x, mod, wo, gpost, gpre, w1, w2, gpm)


def _rope_tables(n):
    rows = n // GRID_W
    row = jnp.repeat(jnp.arange(rows, dtype=F32), GRID_W)
    col = jnp.tile(jnp.arange(GRID_W, dtype=F32), rows)
    half = HEAD_DIM // 2
    inv_freq = ROPE_BASE ** (-jnp.arange(0, half, 2, dtype=F32) / half)
    ang_r = row[:, None] * inv_freq
    ang_c = col[:, None] * inv_freq
    cr, sr, cc, sc = jnp.cos(ang_r), jnp.sin(ang_r), jnp.cos(ang_c), jnp.sin(ang_c)
    cos = jnp.concatenate([cr, cr, cc, cc], axis=-1)
    sin = jnp.concatenate([-sr, sr, -sc, sc], axis=-1)
    reps = LANES // HEAD_DIM
    return jnp.tile(cos, (1, reps)), jnp.tile(sin, (1, reps))


def _head_perm(base):
    idx = jnp.arange(HEAD_DIM)
    return jnp.concatenate([base + h * HEAD_DIM + idx for h in (0, 2, 1, 3)])


def _block_diag(w):
    nb, hb, _ = w.shape
    per = nb // 2
    eye = jnp.eye(per, dtype=w.dtype)
    blocks = w.reshape(2, per, hb, 1, hb) * eye[None, :, None, :, None]
    return blocks.reshape(2, per * hb, per * hb)


def kernel(x, c, ctx, c_ctx, w_mod, b_mod, g_pre_mix, g_post_mix, g_pre_mlp, g_post_mlp, w_in, g_q_a,
           g_k_a, sink_b, conv_w, conv_b, lru_w_a, lru_b_a, lru_w_i, lru_b_i, lru_lambda, w_out,
           w_mlp_in, w_mlp_out):
    b, n, d = x.shape
    depth = w_mod.shape[0]
    cos, sin = _rope_tables(n)
    seg = jnp.arange(A_Q) // HEAD_DIM
    bd = jnp.where(seg[:, None] == seg[None, :], 1.0 / HEAD_DIM, 0.0).astype(BF16)

    in_perm = jnp.concatenate([_head_perm(0), jnp.arange(A_Q, A_Q + 2 * A_KV),
                               _head_perm(A_Q + 2 * A_KV),
                               jnp.arange(A_Q + 2 * A_KV + B_Q, w_in.shape[2])])
    out_perm = jnp.concatenate([_head_perm(0), _head_perm(A_Q), jnp.arange(A_Q + B_Q, w_out.shape[1])])

    mod_rows = -(-(b + 1) // SUBLANES) * SUBLANES
    cc = jnp.zeros((mod_rows, d), F32).at[:b].set(c).at[b].set(c_ctx)

    for l in range(depth):
        last = l == depth - 1
        mod = _modulation(cc, w_mod[l].astype(BF16), b_mod[l][None, :])
        mod_lat = mod[:b].reshape(b, 6, d)
        mod_ctx = jnp.broadcast_to(mod[b].reshape(1, 6, d), (b, 6, d))

        w_in_l = w_in[l][:, in_perm].astype(BF16)
        gq = jnp.tile(g_q_a[l], A_Q_HEADS)[None, :]
        gk = jnp.tile(g_k_a[l], A_KV_HEADS)[None, :]
        gpre = g_pre_mix[l][None, :]
        qa, ka, va, qb, kb, vb, xr, gr = _inproj(x, mod_lat, gpre, w_in_l, gq, gk, bd, cos, sin, rope=True)
        qa_c, ka_c, va_c, qb_c, kb_c, vb_c, xr_c, gr_c = _inproj(
            ctx, mod_ctx, gpre, w_in_l, gq, gk, bd, cos, sin, rope=False)

        feat_a = _dense_attn(qa, [(ka_c, va_c), (ka, va)], None, "global_attn")
        feat_b = _win_attn(qb, kb, vb, kb_c, vb_c, sink_b[l])

        wg = (0.5 * jnp.stack([jnp.concatenate([_block_diag(lru_w_a[l, dd]), _block_diag(lru_w_i[l, dd])],
                                               axis=-1) for dd in range(2)])).astype(BF16)
        bg = 0.5 * jnp.stack([lru_b_a[l, 0], lru_b_i[l, 0], lru_b_a[l, 1], lru_b_i[l, 1]])
        feat_c, feat_cc = _lru(xr_c, gr_c, xr, gr, conv_w[l], conv_b[l][None, :], wg, bg, lru_lambda[l],
                               with_ctx=not last)

        wo = w_out[l][out_perm].astype(BF16)
        post_w = (wo, g_post_mix[l][None, :], g_pre_mlp[l][None, :], w_mlp_in[l].astype(BF16),
                  w_mlp_out[l].astype(BF16), g_post_mlp[l][None, :])
        x = _post(feat_a, feat_b, feat_c, x, mod_lat, *post_w, name="post_mlp")
        if not last:
            feat_ac = _dense_attn(qa_c, [(ka_c, va_c)], None, "ctx_attn_a")
            feat_bc = _dense_attn(qb_c, [(kb_c, vb_c)], sink_b[l], "ctx_attn_b")
            ctx = _post(feat_ac, feat_bc, feat_cc, ctx, mod_ctx, *post_w, name="post_mlp_ctx")
    return x
```

```python
import functools

import jax
import jax.numpy as jnp
from jax import lax
from jax.experimental import pallas as pl
from jax.experimental.pallas import tpu as pltpu

F32 = jnp.float32
BF16 = jnp.bfloat16

GRID_W = 64
HEAD_DIM = 64
A_Q_HEADS = 4
A_KV_HEADS = 2
B_Q_HEADS = 4
B_KV_HEADS = 2
WINDOW = 128
LRU_WIDTH = 512
LRU_BLOCKS = 8
CONV_W = 4
CONV_LEFT = CONV_W // 2
LRU_C = 8.0
ROPE_BASE = 10000.0
EPS = 1e-6
NEG_INF = -1e30

A_Q = A_Q_HEADS * HEAD_DIM
A_KV = A_KV_HEADS * HEAD_DIM
B_Q = B_Q_HEADS * HEAD_DIM
B_KV = B_KV_HEADS * HEAD_DIM
Q_SCALE = HEAD_DIM ** -0.5

LANES = 128
SUBLANES = 8
VMEM_LIMIT = 56 * 1024 * 1024

ROW_TILE = 1024
INPROJ_TILE = 1024
SUB_TILE = 256
POST_SUB_TILE = 512
ATTN_Q_TILE = 1024
ATTN_CHAIN = 512
WIN_GROUP = 8
WIN_CHAINS = 2
LRU_CHUNK = 1024
SCAN_BLOCK = SUBLANES * SUBLANES

_NT = (((1,), (1,)), ((), ()))


def _const_spec(shape):
    return pl.BlockSpec(shape, lambda *_: (0,) * len(shape), pipeline_mode=pl.Buffered(1))


def _params(*sem):
    return pltpu.CompilerParams(dimension_semantics=sem, vmem_limit_bytes=VMEM_LIMIT)


def _rms(x, g):
    return x * lax.rsqrt(jnp.mean(x * x, axis=-1, keepdims=True) + EPS) * g


def _mod_kernel(c_ref, w_ref, b_ref, o_ref):
    c = c_ref[...]
    act = (c * jax.nn.sigmoid(c)).astype(BF16)
    o_ref[...] = jnp.dot(act, w_ref[...].astype(BF16), preferred_element_type=F32) + b_ref[...]


def _modulation(cc, w, b):
    rows, d = cc.shape
    n_out = w.shape[1]
    tn = n_out // 4
    return pl.pallas_call(
        _mod_kernel,
        out_shape=jax.ShapeDtypeStruct((rows, n_out), F32),
        grid=(n_out // tn,),
        in_specs=[
            pl.BlockSpec((rows, d), lambda j: (0, 0)),
            pl.BlockSpec((d, tn), lambda j: (0, j)),
            pl.BlockSpec((1, tn), lambda j: (0, j)),
        ],
        out_specs=pl.BlockSpec((rows, tn), lambda j: (0, j)),
        compiler_params=_params("arbitrary"),
        name="modulation",
    )(cc, w, b)


def _head_rms(t, g, bd):
    sq = t * t
    hi = sq.astype(BF16)
    lo = (sq - hi.astype(F32)).astype(BF16)
    ms = jnp.dot(hi, bd, preferred_element_type=F32) + jnp.dot(lo, bd, preferred_element_type=F32)
    return t * lax.rsqrt(ms + EPS) * g


def _rope(t, cos, sin, low_half):
    outs = []
    for j in range(t.shape[1] // LANES):
        tc = t[:, j * LANES:(j + 1) * LANES]
        partner = jnp.where(low_half, pltpu.roll(tc, LANES - 16, 1), pltpu.roll(tc, 16, 1))
        outs.append(tc * cos + partner * sin)
    return outs[0] if len(outs) == 1 else jnp.concatenate(outs, axis=-1)


def _inproj_kernel(x_ref, mod_ref, gpre_ref, w_ref, gq_ref, gk_ref, bd_ref, cos_ref, sin_ref,
                   qa_ref, ka_ref, va_ref, qb_ref, kb_ref, vb_ref, xr_ref, gr_ref, *, rope):
    mod = mod_ref[0]
    bd = bd_ref[...]
    tm = x_ref.shape[1]
    sub = min(SUB_TILE, tm)
    for r0 in range(0, tm, sub):
        rows = slice(r0, r0 + sub)
        h = _rms(x_ref[0, rows, :], gpre_ref[...]) * (1.0 + mod[1:2]) + mod[0:1]
        z = jnp.dot(h.astype(BF16), w_ref[...], preferred_element_type=F32)
        qa = _head_rms(z[:, 0:A_Q], gq_ref[...], bd)
        ka = _head_rms(z[:, A_Q:A_Q + A_KV], gk_ref[...], bd[:A_KV, :A_KV])
        o = A_Q + 2 * A_KV
        qb = z[:, o:o + B_Q]
        kb = z[:, o + B_Q:o + B_Q + B_KV]
        if rope:
            cos = cos_ref[rows, :]
            sin = sin_ref[rows, :]
            lane = lax.broadcasted_iota(jnp.int32, cos.shape, 1)
            low_half = (lane % 32) < 16
            qa = _rope(qa, cos, sin, low_half)
            ka = _rope(ka, cos, sin, low_half)
            qb = _rope(qb, cos, sin, low_half)
            kb = _rope(kb, cos, sin, low_half)
        qa_ref[0, rows, :] = (qa * Q_SCALE).astype(BF16)
        ka_ref[0, rows, :] = ka.astype(BF16)
        va_ref[0, rows, :] = z[:, A_Q + A_KV:o].astype(BF16)
        qb_ref[0, rows, :] = (qb * Q_SCALE).astype(BF16)
        kb_ref[0, rows, :] = kb.astype(BF16)
        vb_ref[0, rows, :] = z[:, o + B_Q + B_KV:o + B_Q + 2 * B_KV].astype(BF16)
        o2 = o + B_Q + 2 * B_KV
        xr_ref[0, rows, :] = z[:, o2:o2 + LRU_WIDTH]
        gr_ref[0, rows, :] = z[:, o2 + LRU_WIDTH:o2 + 2 * LRU_WIDTH]


def _inproj(x, mod, gpre, w, gq, gk, bd, cos, sin, *, rope):
    b, n, d = x.shape
    tm = min(INPROJ_TILE, n)
    nt = n // tm
    widths = (A_Q, A_KV, A_KV, B_Q, B_KV, B_KV, LRU_WIDTH, LRU_WIDTH)
    dtypes = (BF16,) * 6 + (F32, F32)
    row_spec = lambda wd: pl.BlockSpec((1, tm, wd), lambda i, j: (i, j, 0))
    return pl.pallas_call(
        functools.partial(_inproj_kernel, rope=rope),
        out_shape=[jax.ShapeDtypeStruct((b, n, wd), dt) for wd, dt in zip(widths, dtypes)],
        grid=(b, nt),
        in_specs=[
            row_spec(d),
            pl.BlockSpec((1, 6, d), lambda i, j: (i, 0, 0)),
            _const_spec(gpre.shape),
            _const_spec(w.shape),
            _const_spec(gq.shape),
            _const_spec(gk.shape),
            _const_spec(bd.shape),
            pl.BlockSpec((tm, LANES), lambda i, j: (j, 0)),
            pl.BlockSpec((tm, LANES), lambda i, j: (j, 0)),
        ],
        out_specs=[row_spec(wd) for wd in widths],
        compiler_params=_params("parallel", "parallel"),
        name="inproj_rope" if rope else "inproj_ctx",
    )(x, mod, gpre, w, gq, gk, bd, cos, sin)


def _split_heads(qc, low):
    zero = jnp.zeros_like(qc)
    return jnp.where(low, qc, zero), jnp.where(low, zero, qc)


def _stack_heads(q, low):
    parts = []
    for j in range(q.shape[1] // LANES):
        parts.extend(_split_heads(q[:, j * LANES:(j + 1) * LANES], low))
    return jnp.concatenate(parts, axis=0)


def _unstack_heads(o, t, low):
    return jnp.concatenate([jnp.where(low, o[0:t], o[t:2 * t]),
                            jnp.where(low, o[2 * t:3 * t], o[3 * t:4 * t])], axis=-1)


def _with_ones(v):
    return jnp.concatenate([v, jnp.ones_like(v)], axis=-1)


def _chain_sizes(total):
    assert total % ATTN_CHAIN == 0
    half = ATTN_CHAIN // 2
    return (half,) + (ATTN_CHAIN,) * (total // ATTN_CHAIN - 1) + (half,)


def _dense_attn_kernel(*refs, n_kv, has_sink):
    q_ref = refs[0]
    kv_refs = refs[1:1 + 2 * n_kv]
    sink_ref = refs[1 + 2 * n_kv] if has_sink else None
    o_ref = refs[-1]
    q = q_ref[0]
    tq = q.shape[0]
    low = lax.broadcasted_iota(jnp.int32, (tq, LANES), 1) < HEAD_DIM
    qs_all = _stack_heads(q, low)
    keys = [kv_refs[2 * i][0] for i in range(n_kv)]
    vals = [_with_ones(kv_refs[2 * i + 1][0]) for i in range(n_kv)]
    if has_sink:
        sink_all = jnp.concatenate([jnp.full((tq, 1), sink_ref[h], F32) for h in (0, 2, 1, 3)], axis=0)
    outs, r0 = [], 0
    for rows in _chain_sizes(4 * tq):
        qs = qs_all[r0:r0 + rows]
        scores = [lax.dot_general(qs, k, _NT, preferred_element_type=F32) for k in keys]
        s = scores[0] if n_kv == 1 else jnp.concatenate(scores, axis=-1)
        m = jnp.max(s, axis=-1, keepdims=True)
        if has_sink:
            sink = sink_all[r0:r0 + rows]
            m = jnp.maximum(m, sink)
        p = jnp.exp(s - m).astype(BF16)
        acc, c0 = None, 0
        for k, v in zip(keys, vals):
            pv = jnp.dot(p[:, c0:c0 + k.shape[0]], v, preferred_element_type=F32)
            acc = pv if acc is None else acc + pv
            c0 += k.shape[0]
        denom = acc[:, LANES:]
        if has_sink:
            denom = denom + jnp.exp(sink - m)
        outs.append(acc[:, :LANES] / denom)
        r0 += rows
    o_ref[0] = _unstack_heads(jnp.concatenate(outs, axis=0), tq, low).astype(o_ref.dtype)


def _dense_attn(q, kvs, sink, name):
    b, n, w = q.shape
    tq = min(ATTN_Q_TILE, n)
    in_specs = [pl.BlockSpec((1, tq, w), lambda i, j: (i, j, 0))]
    args = [q]
    for k, v in kvs:
        for t in (k, v):
            in_specs.append(pl.BlockSpec((1,) + t.shape[1:], lambda i, j: (i, 0, 0)))
            args.append(t)
    if sink is not None:
        in_specs.append(pl.BlockSpec(memory_space=pltpu.SMEM))
        args.append(sink)
    return pl.pallas_call(
        functools.partial(_dense_attn_kernel, n_kv=len(kvs), has_sink=sink is not None),
        out_shape=jax.ShapeDtypeStruct((b, n, w), BF16),
        grid=(b, n // tq),
        in_specs=in_specs,
        out_specs=pl.BlockSpec((1, tq, w), lambda i, j: (i, j, 0)),
        compiler_params=_params("parallel", "parallel"),
        name=name,
    )(*args)


def _win_attn_kernel(q_ref, k_ref, v_ref, kc_ref, vc_ref, sink_ref, o_ref):
    n = q_ref.shape[1]
    qb = WINDOW
    band = 3 * qb
    rows = WIN_GROUP * qb
    n_iter = n // rows
    low = lax.broadcasted_iota(jnp.int32, (qb, LANES), 1) < HEAD_DIM
    row_minus_col = (lax.broadcasted_iota(jnp.int32, (qb, band), 0)
                     - lax.broadcasted_iota(jnp.int32, (qb, band), 1))

    def band_bias(first_key_offset):
        bias = jnp.where(jnp.abs(row_minus_col + first_key_offset) <= WINDOW, 0.0, NEG_INF)
        return jnp.concatenate([bias.astype(F32)] * 4, axis=0)

    bias_first, bias_mid, bias_last = band_bias(0), band_bias(qb), band_bias(2 * qb)
    kc = kc_ref[0]
    vc = _with_ones(vc_ref[0])
    ones_band = jnp.ones((band, LANES), BF16)
    sink_blk = jnp.concatenate([jnp.full((qb, 1), sink_ref[h], F32) for h in (0, 2, 1, 3)], axis=0)
    per_chain = WIN_GROUP // WIN_CHAINS
    sink = jnp.concatenate([sink_blk] * per_chain, axis=0)

    def chain(it, t0, q, blocks):
        qs_blk = [_stack_heads(q[g * qb:(g + 1) * qb], low) for g in blocks]
        starts = [pl.multiple_of(jnp.clip(t0 + (g - 1) * qb, 0, n - band), qb) for g in blocks]
        s_ctx = lax.dot_general(jnp.concatenate(qs_blk, axis=0), kc, _NT, preferred_element_type=F32)
        s_band = []
        for i, g in enumerate(blocks):
            s = lax.dot_general(qs_blk[i], k_ref[0, pl.ds(starts[i], band), :], _NT,
                                preferred_element_type=F32)
            bias = bias_mid
            if g == 0:
                bias = jnp.where(it == 0, bias_first, bias)
            if g == WIN_GROUP - 1:
                bias = jnp.where(it == n_iter - 1, bias_last, bias)
            s_band.append(s + bias)
        s = jnp.concatenate([jnp.concatenate(s_band, axis=0), s_ctx], axis=-1)
        m = jnp.maximum(jnp.max(s, axis=-1, keepdims=True), sink)
        p = jnp.exp(s - m).astype(BF16)
        acc_band = [jnp.dot(p[i * 4 * qb:(i + 1) * 4 * qb, :band],
                            jnp.concatenate([v_ref[0, pl.ds(starts[i], band), :], ones_band], axis=-1),
                            preferred_element_type=F32) for i in range(len(blocks))]
        acc = jnp.concatenate(acc_band, axis=0) + jnp.dot(p[:, band:], vc, preferred_element_type=F32)
        out = acc[:, :LANES] / (acc[:, LANES:] + jnp.exp(sink - m))
        return [_unstack_heads(out[i * 4 * qb:(i + 1) * 4 * qb], qb, low) for i in range(len(blocks))]

    def body(it, carry):
        t0 = pl.multiple_of(it * rows, rows)
        q = q_ref[0, pl.ds(t0, rows), :]
        outs = []
        for c in range(WIN_CHAINS):
            outs.extend(chain(it, t0, q, range(c * per_chain, (c + 1) * per_chain)))
        o_ref[0, pl.ds(t0, rows), :] = jnp.concatenate(outs, axis=0).astype(o_ref.dtype)
        return carry

    lax.fori_loop(0, n_iter, body, 0)


def _win_attn(q, k, v, kc, vc, sink):
    b, n, w = q.shape
    full = lambda t: pl.BlockSpec((1,) + t.shape[1:], lambda i: (i, 0, 0))
    return pl.pallas_call(
        _win_attn_kernel,
        out_shape=jax.ShapeDtypeStruct((b, n, w), BF16),
        grid=(b,),
        in_specs=[full(q), full(k), full(v), full(kc), full(vc),
                  pl.BlockSpec(memory_space=pltpu.SMEM)],
        out_specs=full(q),
        compiler_params=_params("parallel"),
        name="window_attn",
    )(q, k, v, kc, vc, sink)


def _scan_block(a_scr, u_scr, h_scr, base, col, c_in, row8, reverse):
    order = range(SUBLANES - 1, -1, -1) if reverse else range(SUBLANES)
    a_cum, h_loc = [], []
    for j in order:
        aj = a_scr[col, base + j * SUBLANES:base + (j + 1) * SUBLANES, :]
        uj = u_scr[col, base + j * SUBLANES:base + (j + 1) * SUBLANES, :]
        if a_cum:
            h_loc.append(aj * h_loc[-1] + uj)
            a_cum.append(aj * a_cum[-1])
        else:
            h_loc.append(uj)
            a_cum.append(aj)
    p, q = a_cum[-1], h_loc[-1]
    for s in (1, 2, 4):
        shift = SUBLANES - s if reverse else s
        valid = (row8 < SUBLANES - s) if reverse else (row8 >= s)
        q = jnp.where(valid, p * pltpu.roll(q, shift, 0) + q, q)
        p = jnp.where(valid, p * pltpu.roll(p, shift, 0), p)
    end = q + p * c_in
    if reverse:
        c_grp = jnp.where(row8 < SUBLANES - 1, pltpu.roll(end, SUBLANES - 1, 0), c_in)
        last = end[0:1]
    else:
        c_grp = jnp.where(row8 >= 1, pltpu.roll(end, 1, 0), c_in)
        last = end[SUBLANES - 1:SUBLANES]
    for idx, j in enumerate(order):
        h_scr[col, pl.ds(base + j, SUBLANES, stride=SUBLANES), :] = h_loc[idx] + a_cum[idx] * c_grp
    return jnp.broadcast_to(last, (SUBLANES, LANES))


def _lru_chunk(pad_ref, conv_ref, t0, rows, d, carry, reverse, cw_ref, cb_ref, wg_ref, bg_ref,
               quarter_nsp, a_scr, u_scr, h_scr, row8):
    if reverse:
        xc = conv_ref[pl.ds(t0, rows), :]
    else:
        halo = SUBLANES
        win = pad_ref[pl.ds(t0, rows + 2 * halo), :]
        xc = cb_ref[...]
        for j in range(CONV_W):
            off = j - CONV_LEFT
            tap = win if off == 0 else pltpu.roll(win, (-off) % (rows + 2 * halo), 0)
            xc = xc + tap[halo:halo + rows] * cw_ref[j:j + 1, :]
        conv_ref[pl.ds(t0, rows), :] = xc
    xb = xc.astype(BF16)
    half = LRU_WIDTH // 2
    y0 = jnp.dot(xb[:, :half], wg_ref[d, 0], preferred_element_type=F32)
    y1 = jnp.dot(xb[:, half:], wg_ref[d, 1], preferred_element_type=F32)
    ya = jnp.concatenate([y0[:, :half], y1[:, :half]], axis=-1)
    yi = jnp.concatenate([y0[:, half:], y1[:, half:]], axis=-1)
    t_r = jnp.tanh(ya + bg_ref[2 * d:2 * d + 1, :])
    t_i = jnp.tanh(yi + bg_ref[2 * d + 1:2 * d + 2, :])
    t = jnp.tanh(t_r * quarter_nsp + quarter_nsp)
    inv = 1.0 / (1.0 - t)
    a = (1.0 + t) * inv
    neg_t = -t
    root = jnp.where(neg_t > 0.0, neg_t * lax.rsqrt(neg_t), 0.0)
    u = root * inv * ((t_i + 1.0) * xc)
    for col in range(LRU_WIDTH // LANES):
        for i in range(rows // SUBLANES):
            blk, g = divmod(i, SUBLANES)
            dst = pl.ds(blk * SCAN_BLOCK + g, SUBLANES, stride=SUBLANES)
            src = (slice(i * SUBLANES, (i + 1) * SUBLANES), slice(col * LANES, (col + 1) * LANES))
            a_scr[col, dst, :] = a[src]
            u_scr[col, dst, :] = u[src]
    n_blk = rows // SCAN_BLOCK
    blocks = range(n_blk - 1, -1, -1) if reverse else range(n_blk)
    carry = list(carry)
    for blk in blocks:
        for col in range(LRU_WIDTH // LANES):
            carry[col] = _scan_block(a_scr, u_scr, h_scr, blk * SCAN_BLOCK, col, carry[col],
                                     row8, reverse)
    return tuple(carry)


def _lru_kernel(xc_ref, gc_ref, x_ref, g_ref, cw_ref, cb_ref, wg_ref, bg_ref, lam_ref, *rest,
                with_ctx):
    if with_ctx:
        o_ref, oc_ref, xpad, xcpad, xconv, xcconv, a_scr, u_scr, h_scr, hf, hfc = rest
    else:
        o_ref, xpad, xcpad, xconv, xcconv, a_scr, u_scr, h_scr, hf = rest
    n = x_ref.shape[1]
    m = xc_ref.shape[1]
    halo = SUBLANES
    zeros = jnp.zeros((halo, LRU_WIDTH), F32)
    for pad, src, length in ((xpad, x_ref, n), (xcpad, xc_ref, m)):
        pad[0:halo, :] = zeros
        pad[halo:halo + length, :] = src[0]
        pad[halo + length:2 * halo + length, :] = zeros

    neg_lam = -lam_ref[...]
    softplus = jnp.maximum(neg_lam, 0.0) + jnp.log1p(jnp.exp(-jnp.abs(neg_lam)))
    quarter_nsp = (-0.25 * LRU_C) * softplus
    row8 = lax.broadcasted_iota(jnp.int32, (SUBLANES, LANES), 0)
    zero_carry = tuple(jnp.zeros((SUBLANES, LANES), F32) for _ in range(LRU_WIDTH // LANES))
    n_chunks = n // LRU_CHUNK

    def chunk(ctx_part, t0, rows, d, carry):
        pad_ref, conv_ref = (xcpad, xcconv) if ctx_part else (xpad, xconv)
        return _lru_chunk(pad_ref, conv_ref, t0, rows, d, carry, d == 1, cw_ref, cb_ref, wg_ref, bg_ref,
                          quarter_nsp[d:d + 1, :], a_scr, u_scr, h_scr, row8)

    def read_h(rows):
        return jnp.concatenate([h_scr[col, 0:rows, :] for col in range(LRU_WIDTH // LANES)], axis=-1)

    carry = chunk(True, 0, m, 0, zero_carry)
    if with_ctx:
        hfc[...] = read_h(m)

    def fwd_body(ci, carry):
        t0 = pl.multiple_of(ci * LRU_CHUNK, LRU_CHUNK)
        carry = chunk(False, t0, LRU_CHUNK, 0, carry)
        hf[pl.ds(t0, LRU_CHUNK), :] = read_h(LRU_CHUNK)
        return carry

    lax.fori_loop(0, n_chunks, fwd_body, carry)

    carry = chunk(True, 0, m, 1, zero_carry)
    if with_ctx:
        oc_ref[0] = ((hfc[...] + read_h(m)) * jax.nn.gelu(gc_ref[0])).astype(oc_ref.dtype)

    def bwd_body(ci, carry):
        t0 = pl.multiple_of((n_chunks - 1 - ci) * LRU_CHUNK, LRU_CHUNK)
        carry = chunk(False, t0, LRU_CHUNK, 1, carry)
        hf[pl.ds(t0, LRU_CHUNK), :] = hf[pl.ds(t0, LRU_CHUNK), :] + read_h(LRU_CHUNK)
        return carry

    lax.fori_loop(0, n_chunks, bwd_body, carry)

    def gate_body(ci, _):
        t0 = pl.multiple_of(ci * LRU_CHUNK, LRU_CHUNK)
        gate = jax.nn.gelu(g_ref[0, pl.ds(t0, LRU_CHUNK), :])
        o_ref[0, pl.ds(t0, LRU_CHUNK), :] = (hf[pl.ds(t0, LRU_CHUNK), :] * gate).astype(o_ref.dtype)
        return 0

    lax.fori_loop(0, n_chunks, gate_body, 0)


def _lru(xc, gc, x, g, cw, cb, wg, bg, lam, *, with_ctx):
    b, n, w = x.shape
    m = xc.shape[1]
    full = lambda t: pl.BlockSpec((1,) + t.shape[1:], lambda i: (i, 0, 0))
    out_shape = [jax.ShapeDtypeStruct((b, n, w), BF16)]
    out_specs = [pl.BlockSpec((1, n, w), lambda i: (i, 0, 0))]
    scratch = [pltpu.VMEM((n + 2 * SUBLANES, w), F32), pltpu.VMEM((m + 2 * SUBLANES, w), F32),
               pltpu.VMEM((n, w), F32), pltpu.VMEM((m, w), F32),
               *[pltpu.VMEM((w // LANES, LRU_CHUNK, LANES), F32) for _ in range(3)],
               pltpu.VMEM((n, w), F32)]
    if with_ctx:
        out_shape.append(jax.ShapeDtypeStruct((b, m, w), BF16))
        out_specs.append(pl.BlockSpec((1, m, w), lambda i: (i, 0, 0)))
        scratch.append(pltpu.VMEM((m, w), F32))
    outs = pl.pallas_call(
        functools.partial(_lru_kernel, with_ctx=with_ctx),
        out_shape=out_shape,
        grid=(b,),
        in_specs=[full(xc), full(gc), full(x), full(g), _const_spec(cw.shape), _const_spec(cb.shape),
                  _const_spec(wg.shape), _const_spec(bg.shape), _const_spec(lam.shape)],
        out_specs=out_specs,
        scratch_shapes=scratch,
        compiler_params=_params("parallel"),
        name="rglru_ctx" if with_ctx else "rglru",
    )(xc, gc, x, g, cw, cb, wg, bg, lam)
    return (outs[0], outs[1]) if with_ctx else (outs[0], None)


def _post_kernel(fa_ref, fb_ref, fc_ref, x_ref, mod_ref, wo_ref, gpost_ref, gpre_ref, w1_ref, w2_ref,
                 gpm_ref, o_ref):
    mod = mod_ref[0]
    tm = x_ref.shape[1]
    sub = min(POST_SUB_TILE, tm)
    d_ff = w1_ref.shape[1]
    ff_chunk = d_ff // 4
    for r0 in range(0, tm, sub):
        rows = slice(r0, r0 + sub)
        y = (jnp.dot(fa_ref[0, rows, :], wo_ref[0:A_Q, :], preferred_element_type=F32)
             + jnp.dot(fb_ref[0, rows, :], wo_ref[A_Q:A_Q + B_Q, :], preferred_element_type=F32)
             + jnp.dot(fc_ref[0, rows, :], wo_ref[A_Q + B_Q:, :], preferred_element_type=F32))
        x1 = x_ref[0, rows, :] + mod[2:3] * _rms(y, gpost_ref[...])
        h2 = (_rms(x1, gpre_ref[...]) * (1.0 + mod[4:5]) + mod[3:4]).astype(BF16)
        acc = None
        for k in range(d_ff // ff_chunk):
            hk = jnp.dot(h2, w1_ref[:, k * ff_chunk:(k + 1) * ff_chunk], preferred_element_type=F32)
            hk = jnp.square(jnp.maximum(hk, 0.0)).astype(BF16)
            part = jnp.dot(hk, w2_ref[k * ff_chunk:(k + 1) * ff_chunk, :], preferred_element_type=F32)
            acc = part if acc is None else acc + part
        o_ref[0, rows, :] = x1 + mod[5:6] * _rms(acc, gpm_ref[...])


def _post(fa, fb, fc, x, mod, wo, gpost, gpre, w1, w2, gpm, name):
    b, n, d = x.shape
    tm = min(ROW_TILE, n)
    row_spec = lambda wd: pl.BlockSpec((1, tm, wd), lambda i, j: (i, j, 0))
    return pl.pallas_call(
        _post_kernel,
        out_shape=jax.ShapeDtypeStruct((b, n, d), F32),
        grid=(b, n // tm),
        in_specs=[row_spec(fa.shape[2]), row_spec(fb.shape[2]), row_spec(fc.shape[2]), row_spec(d),
                  pl.BlockSpec((1, 6, d), lambda i, j: (i, 0, 0)),
                  _const_spec(wo.shape), _const_spec(gpost.shape), _const_spec(gpre.shape),
                  _const_spec(w1.shape), _const_spec(w2.shape), _const_spec(gpm.shape)],
        out_specs=row_spec(d),
        compiler_params=_params("parallel", "parallel"),
        name=name,
    )(fa, fb, fc, x, mod, wo, gpost, gpre, w1, w2, gpm)


def _rope_tables(n):
    rows = n // GRID_W
    row = jnp.repeat(jnp.arange(rows, dtype=F32), GRID_W)
    col = jnp.tile(jnp.arange(GRID_W, dtype=F32), rows)
    half = HEAD_DIM // 2
    inv_freq = ROPE_BASE ** (-jnp.arange(0, half, 2, dtype=F32) / half)
    ang_r = row[:, None] * inv_freq
    ang_c = col[:, None] * inv_freq
    cr, sr, cc, sc = jnp.cos(ang_r), jnp.sin(ang_r), jnp.cos(ang_c), jnp.sin(ang_c)
    cos = jnp.concatenate([cr, cr, cc, cc], axis=-1)
    sin = jnp.concatenate([-sr, sr, -sc, sc], axis=-1)
    reps = LANES // HEAD_DIM
    return jnp.tile(cos, (1, reps)), jnp.tile(sin, (1, reps))


def _head_perm(base):
    idx = jnp.arange(HEAD_DIM)
    return jnp.concatenate([base + h * HEAD_DIM + idx for h in (0, 2, 1, 3)])


def _block_diag(w):
    nb, hb, _ = w.shape
    per = nb // 2
    eye = jnp.eye(per, dtype=w.dtype)
    blocks = w.reshape(2, per, hb, 1, hb) * eye[None, :, None, :, None]
    return blocks.reshape(2, per * hb, per * hb)


def kernel(x, c, ctx, c_ctx, w_mod, b_mod, g_pre_mix, g_post_mix, g_pre_mlp, g_post_mlp, w_in, g_q_a,
           g_k_a, sink_b, conv_w, conv_b, lru_w_a, lru_b_a, lru_w_i, lru_b_i, lru_lambda, w_out,
           w_mlp_in, w_mlp_out):
    b, n, d = x.shape
    depth = w_mod.shape[0]
    cos, sin = _rope_tables(n)
    seg = jnp.arange(A_Q) // HEAD_DIM
    bd = jnp.where(seg[:, None] == seg[None, :], 1.0 / HEAD_DIM, 0.0).astype(BF16)

    in_perm = jnp.concatenate([_head_perm(0), jnp.arange(A_Q, A_Q + 2 * A_KV),
                               _head_perm(A_Q + 2 * A_KV),
                               jnp.arange(A_Q + 2 * A_KV + B_Q, w_in.shape[2])])
    out_perm = jnp.concatenate([_head_perm(0), _head_perm(A_Q), jnp.arange(A_Q + B_Q, w_out.shape[1])])

    mod_rows = -(-(b + 1) // SUBLANES) * SUBLANES
    cc = jnp.zeros((mod_rows, d), F32).at[:b].set(c).at[b].set(c_ctx)

    for l in range(depth):
        last = l == depth - 1
        mod = _modulation(cc, w_mod[l], b_mod[l][None, :])
        mod_lat = mod[:b].reshape(b, 6, d)
        mod_ctx = jnp.broadcast_to(mod[b].reshape(1, 6, d), (b, 6, d))

        w_in_l = w_in[l][:, in_perm].astype(BF16)
        gq = jnp.tile(g_q_a[l], A_Q_HEADS)[None, :]
        gk = jnp.tile(g_k_a[l], A_KV_HEADS)[None, :]
        gpre = g_pre_mix[l][None, :]
        qa, ka, va, qb, kb, vb, xr, gr = _inproj(x, mod_lat, gpre, w_in_l, gq, gk, bd, cos, sin, rope=True)
        qa_c, ka_c, va_c, qb_c, kb_c, vb_c, xr_c, gr_c = _inproj(
            ctx, mod_ctx, gpre, w_in_l, gq, gk, bd, cos, sin, rope=False)

        feat_a = _dense_attn(qa, [(ka_c, va_c), (ka, va)], None, "global_attn")
        feat_b = _win_attn(qb, kb, vb, kb_c, vb_c, sink_b[l])

        wg = (0.5 * jnp.stack([jnp.concatenate([_block_diag(lru_w_a[l, dd]), _block_diag(lru_w_i[l, dd])],
                                               axis=-1) for dd in range(2)])).astype(BF16)
        bg = 0.5 * jnp.stack([lru_b_a[l, 0], lru_b_i[l, 0], lru_b_a[l, 1], lru_b_i[l, 1]])
        feat_c, feat_cc = _lru(xr_c, gr_c, xr, gr, conv_w[l], conv_b[l][None, :], wg, bg, lru_lambda[l],
                               with_ctx=not last)

        wo = w_out[l][out_perm].astype(BF16)
        post_w = (wo, g_post_mix[l][None, :], g_pre_mlp[l][None, :], w_mlp_in[l].astype(BF16),
                  w_mlp_out[l].astype(BF16), g_post_mlp[l][None, :])
        x = _post(feat_a, feat_b, feat_c, x, mod_lat, *post_w, name="post_mlp")
        if not last:
            feat_ac = _dense_attn(qa_c, [(ka_c, va_c)], None, "ctx_attn_a")
            feat_bc = _dense_attn(qb_c, [(kb_c, vb_c)], sink_b[l], "ctx_attn_b")
            ctx = _post(feat_ac, feat_bc, feat_cc, ctx, mod_ctx, *post_w, name="post_mlp_ctx")
    return x
```

```python
import functools

import jax
import jax.numpy as jnp
from jax import lax
from jax.experimental import pallas as pl
from jax.experimental.pallas import tpu as pltpu

F32 = jnp.float32
BF16 = jnp.bfloat16

GRID_W = 64
HEAD_DIM = 64
A_Q_HEADS = 4
A_KV_HEADS = 2
B_Q_HEADS = 4
B_KV_HEADS = 2
WINDOW = 128
LRU_WIDTH = 512
LRU_BLOCKS = 8
CONV_W = 4
CONV_LEFT = CONV_W // 2
LRU_C = 8.0
ROPE_BASE = 10000.0
EPS = 1e-6
NEG_INF = -1e30

A_Q = A_Q_HEADS * HEAD_DIM
A_KV = A_KV_HEADS * HEAD_DIM
B_Q = B_Q_HEADS * HEAD_DIM
B_KV = B_KV_HEADS * HEAD_DIM
Q_SCALE = HEAD_DIM ** -0.5

LANES = 128
SUBLANES = 8
VMEM_LIMIT = 56 * 1024 * 1024

ROW_TILE = 1024
INPROJ_TILE = 1024
SUB_TILE = 256
POST_SUB_TILE = 512
ATTN_Q_TILE = 1024
ATTN_CHAIN = 512
WIN_GROUP = 8
WIN_CHAINS = 2
LRU_CHUNK = 1024
SCAN_BLOCK = SUBLANES * SUBLANES

_NT = (((1,), (1,)), ((), ()))


def _const_spec(shape):
    return pl.BlockSpec(shape, lambda *_: (0,) * len(shape), pipeline_mode=pl.Buffered(1))


def _params(*sem):
    return pltpu.CompilerParams(dimension_semantics=sem, vmem_limit_bytes=VMEM_LIMIT)


def _rms(x, g):
    return x * lax.rsqrt(jnp.mean(x * x, axis=-1, keepdims=True) + EPS) * g


def _mod_kernel(c_ref, w_ref, b_ref, o_ref):
    c = c_ref[...]
    act = (c * jax.nn.sigmoid(c)).astype(BF16)
    o_ref[...] = jnp.dot(act, w_ref[...].astype(BF16), preferred_element_type=F32) + b_ref[...]


def _modulation(cc, w, b):
    rows, d = cc.shape
    depth, _, n_out = w.shape
    tn = n_out // 4
    return pl.pallas_call(
        _mod_kernel,
        out_shape=jax.ShapeDtypeStruct((depth, rows, n_out), F32),
        grid=(depth, n_out // tn),
        in_specs=[
            pl.BlockSpec((rows, d), lambda l, j: (0, 0)),
            pl.BlockSpec((None, d, tn), lambda l, j: (l, 0, j)),
            pl.BlockSpec((None, 1, tn), lambda l, j: (l, 0, j)),
        ],
        out_specs=pl.BlockSpec((None, rows, tn), lambda l, j: (l, 0, j)),
        compiler_params=_params("arbitrary", "arbitrary"),
        name="modulation",
    )(cc, w, b)


def _head_rms(t, g, bd):
    sq = t * t
    hi = sq.astype(BF16)
    lo = (sq - hi.astype(F32)).astype(BF16)
    ms = jnp.dot(hi, bd, preferred_element_type=F32) + jnp.dot(lo, bd, preferred_element_type=F32)
    return t * lax.rsqrt(ms + EPS) * g


def _rope(t, cos, sin, low_half):
    outs = []
    for j in range(t.shape[1] // LANES):
        tc = t[:, j * LANES:(j + 1) * LANES]
        partner = jnp.where(low_half, pltpu.roll(tc, LANES - 16, 1), pltpu.roll(tc, 16, 1))
        outs.append(tc * cos + partner * sin)
    return outs[0] if len(outs) == 1 else jnp.concatenate(outs, axis=-1)


def _inproj_kernel(x_ref, mod_ref, gpre_ref, w_ref, gq_ref, gk_ref, bd_ref, cos_ref, sin_ref,
                   qa_ref, ka_ref, va_ref, qb_ref, kb_ref, vb_ref, xr_ref, gr_ref, *, rope):
    mod = mod_ref[0]
    bd = bd_ref[...]
    tm = x_ref.shape[1]
    sub = min(SUB_TILE, tm)
    pre = []
    for r0 in range(0, tm, sub):
        rows = slice(r0, r0 + sub)
        h = _rms(x_ref[0, rows, :], gpre_ref[...]) * (1.0 + mod[1:2]) + mod[0:1]
        pre.append((rows, h.astype(BF16)))
    for rows, hb in pre:
        z = jnp.dot(hb, w_ref[...], preferred_element_type=F32)
        qa = _head_rms(z[:, 0:A_Q], gq_ref[...], bd)
        ka = _head_rms(z[:, A_Q:A_Q + A_KV], gk_ref[...], bd[:A_KV, :A_KV])
        o = A_Q + 2 * A_KV
        qb = z[:, o:o + B_Q]
        kb = z[:, o + B_Q:o + B_Q + B_KV]
        if rope:
            cos = cos_ref[rows, :]
            sin = sin_ref[rows, :]
            lane = lax.broadcasted_iota(jnp.int32, cos.shape, 1)
            low_half = (lane % 32) < 16
            qa = _rope(qa, cos, sin, low_half)
            ka = _rope(ka, cos, sin, low_half)
            qb = _rope(qb, cos, sin, low_half)
            kb = _rope(kb, cos, sin, low_half)
        qa_ref[0, rows, :] = (qa * Q_SCALE).astype(BF16)
        ka_ref[0, rows, :] = ka.astype(BF16)
        va_ref[0, rows, :] = z[:, A_Q + A_KV:o].astype(BF16)
        qb_ref[0, rows, :] = (qb * Q_SCALE).astype(BF16)
        kb_ref[0, rows, :] = kb.astype(BF16)
        vb_ref[0, rows, :] = z[:, o + B_Q + B_KV:o + B_Q + 2 * B_KV].astype(BF16)
        o2 = o + B_Q + 2 * B_KV
        xr_ref[0, rows, :] = z[:, o2:o2 + LRU_WIDTH]
        gr_ref[0, rows, :] = z[:, o2 + LRU_WIDTH:o2 + 2 * LRU_WIDTH]


def _inproj(x, mod, gpre, w, gq, gk, bd, cos, sin, *, rope):
    b, n, d = x.shape
    tm = min(INPROJ_TILE, n)
    nt = n // tm
    widths = (A_Q, A_KV, A_KV, B_Q, B_KV, B_KV, LRU_WIDTH, LRU_WIDTH)
    dtypes = (BF16,) * 6 + (F32, F32)
    row_spec = lambda wd: pl.BlockSpec((1, tm, wd), lambda i, j: (i, j, 0))
    table_spec = pl.BlockSpec((tm, LANES), (lambda i, j: (j, 0)) if rope else (lambda i, j: (0, 0)))
    return pl.pallas_call(
        functools.partial(_inproj_kernel, rope=rope),
        out_shape=[jax.ShapeDtypeStruct((b, n, wd), dt) for wd, dt in zip(widths, dtypes)],
        grid=(b, nt),
        in_specs=[
            row_spec(d),
            pl.BlockSpec((1, 6, d), lambda i, j: (i, 0, 0)),
            _const_spec(gpre.shape),
            _const_spec(w.shape),
            _const_spec(gq.shape),
            _const_spec(gk.shape),
            _const_spec(bd.shape),
            table_spec,
            table_spec,
        ],
        out_specs=[row_spec(wd) for wd in widths],
        compiler_params=_params("parallel", "parallel"),
        name="inproj_rope" if rope else "inproj_ctx",
    )(x, mod, gpre, w, gq, gk, bd, cos, sin)


def _split_heads(qc, low):
    zero = jnp.zeros_like(qc)
    return jnp.where(low, qc, zero), jnp.where(low, zero, qc)


def _stack_heads(q, low):
    parts = []
    for j in range(q.shape[1] // LANES):
        parts.extend(_split_heads(q[:, j * LANES:(j + 1) * LANES], low))
    return jnp.concatenate(parts, axis=0)


def _unstack_heads(o, t, low):
    return jnp.concatenate([jnp.where(low, o[0:t], o[t:2 * t]),
                            jnp.where(low, o[2 * t:3 * t], o[3 * t:4 * t])], axis=-1)


def _with_ones(v):
    return jnp.concatenate([v, jnp.ones_like(v)], axis=-1)


def _chain_sizes(total):
    assert total % ATTN_CHAIN == 0
    half = ATTN_CHAIN // 2
    return (half,) + (ATTN_CHAIN,) * (total // ATTN_CHAIN - 1) + (half,)


def _dense_attn_kernel(*refs, n_kv, has_sink):
    q_ref = refs[0]
    kv_refs = refs[1:1 + 2 * n_kv]
    sink_ref = refs[1 + 2 * n_kv] if has_sink else None
    o_ref = refs[-1]
    q = q_ref[0]
    tq = q.shape[0]
    low = lax.broadcasted_iota(jnp.int32, (tq, LANES), 1) < HEAD_DIM
    qs_all = _stack_heads(q, low)
    keys = [kv_refs[2 * i][0] for i in range(n_kv)]
    vals = [_with_ones(kv_refs[2 * i + 1][0]) for i in range(n_kv)]
    if has_sink:
        sink_all = jnp.concatenate([jnp.full((tq, 1), sink_ref[h], F32) for h in (0, 2, 1, 3)], axis=0)
    outs, r0 = [], 0
    for rows in _chain_sizes(4 * tq):
        qs = qs_all[r0:r0 + rows]
        scores = [lax.dot_general(qs, k, _NT, preferred_element_type=F32) for k in keys]
        s = scores[0] if n_kv == 1 else jnp.concatenate(scores, axis=-1)
        m = jnp.max(s, axis=-1, keepdims=True)
        if has_sink:
            sink = sink_all[r0:r0 + rows]
            m = jnp.maximum(m, sink)
        p = jnp.exp(s - m).astype(BF16)
        acc, c0 = None, 0
        for k, v in zip(keys, vals):
            pv = jnp.dot(p[:, c0:c0 + k.shape[0]], v, preferred_element_type=F32)
            acc = pv if acc is None else acc + pv
            c0 += k.shape[0]
        denom = acc[:, LANES:]
        if has_sink:
            denom = denom + jnp.exp(sink - m)
        outs.append(acc[:, :LANES] / denom)
        r0 += rows
    o_ref[0] = _unstack_heads(jnp.concatenate(outs, axis=0), tq, low).astype(o_ref.dtype)


def _dense_attn(q, kvs, sink, name):
    b, n, w = q.shape
    tq = min(ATTN_Q_TILE, n)
    in_specs = [pl.BlockSpec((1, tq, w), lambda i, j: (i, j, 0))]
    args = [q]
    for k, v in kvs:
        for t in (k, v):
            in_specs.append(pl.BlockSpec((1,) + t.shape[1:], lambda i, j: (i, 0, 0)))
            args.append(t)
    if sink is not None:
        in_specs.append(pl.BlockSpec(memory_space=pltpu.SMEM))
        args.append(sink)
    return pl.pallas_call(
        functools.partial(_dense_attn_kernel, n_kv=len(kvs), has_sink=sink is not None),
        out_shape=jax.ShapeDtypeStruct((b, n, w), BF16),
        grid=(b, n // tq),
        in_specs=in_specs,
        out_specs=pl.BlockSpec((1, tq, w), lambda i, j: (i, j, 0)),
        compiler_params=_params("parallel", "parallel"),
        name=name,
    )(*args)


def _win_attn_kernel(q_ref, k_ref, v_ref, kc_ref, vc_ref, sink_ref, o_ref):
    n = q_ref.shape[1]
    qb = WINDOW
    band = 3 * qb
    rows = WIN_GROUP * qb
    n_iter = n // rows
    low = lax.broadcasted_iota(jnp.int32, (qb, LANES), 1) < HEAD_DIM
    row_minus_col = (lax.broadcasted_iota(jnp.int32, (qb, band), 0)
                     - lax.broadcasted_iota(jnp.int32, (qb, band), 1))

    def band_bias(first_key_offset):
        bias = jnp.where(jnp.abs(row_minus_col + first_key_offset) <= WINDOW, 0.0, NEG_INF)
        return jnp.concatenate([bias.astype(F32)] * 4, axis=0)

    bias_first, bias_mid, bias_last = band_bias(0), band_bias(qb), band_bias(2 * qb)
    kc = kc_ref[0]
    vc = _with_ones(vc_ref[0])
    ones_band = jnp.ones((band, LANES), BF16)
    sink_blk = jnp.concatenate([jnp.full((qb, 1), sink_ref[h], F32) for h in (0, 2, 1, 3)], axis=0)
    per_chain = WIN_GROUP // WIN_CHAINS
    sink = jnp.concatenate([sink_blk] * per_chain, axis=0)

    def chain(it, t0, q, blocks):
        qs_blk = [_stack_heads(q[g * qb:(g + 1) * qb], low) for g in blocks]
        starts = [pl.multiple_of(jnp.clip(t0 + (g - 1) * qb, 0, n - band), qb) for g in blocks]
        s_ctx = lax.dot_general(jnp.concatenate(qs_blk, axis=0), kc, _NT, preferred_element_type=F32)
        s_band = []
        for i, g in enumerate(blocks):
            s = lax.dot_general(qs_blk[i], k_ref[0, pl.ds(starts[i], band), :], _NT,
                                preferred_element_type=F32)
            bias = bias_mid
            if g == 0:
                bias = jnp.where(it == 0, bias_first, bias)
            if g == WIN_GROUP - 1:
                bias = jnp.where(it == n_iter - 1, bias_last, bias)
            s_band.append(s + bias)
        s = jnp.concatenate([jnp.concatenate(s_band, axis=0), s_ctx], axis=-1)
        m = jnp.maximum(jnp.max(s, axis=-1, keepdims=True), sink)
        p = jnp.exp(s - m).astype(BF16)
        acc_band = [jnp.dot(p[i * 4 * qb:(i + 1) * 4 * qb, :band],
                            jnp.concatenate([v_ref[0, pl.ds(starts[i], band), :], ones_band], axis=-1),
                            preferred_element_type=F32) for i in range(len(blocks))]
        acc = jnp.concatenate(acc_band, axis=0) + jnp.dot(p[:, band:], vc, preferred_element_type=F32)
        out = acc[:, :LANES] / (acc[:, LANES:] + jnp.exp(sink - m))
        return [_unstack_heads(out[i * 4 * qb:(i + 1) * 4 * qb], qb, low) for i in range(len(blocks))]

    def body(it, carry):
        t0 = pl.multiple_of(it * rows, rows)
        q = q_ref[0, pl.ds(t0, rows), :]
        outs = []
        for c in range(WIN_CHAINS):
            outs.extend(chain(it, t0, q, range(c * per_chain, (c + 1) * per_chain)))
        o_ref[0, pl.ds(t0, rows), :] = jnp.concatenate(outs, axis=0).astype(o_ref.dtype)
        return carry

    lax.fori_loop(0, n_iter, body, 0)


def _win_attn(q, k, v, kc, vc, sink):
    b, n, w = q.shape
    full = lambda t: pl.BlockSpec((1,) + t.shape[1:], lambda i: (i, 0, 0))
    return pl.pallas_call(
        _win_attn_kernel,
        out_shape=jax.ShapeDtypeStruct((b, n, w), BF16),
        grid=(b,),
        in_specs=[full(q), full(k), full(v), full(kc), full(vc),
                  pl.BlockSpec(memory_space=pltpu.SMEM)],
        out_specs=full(q),
        compiler_params=_params("parallel"),
        name="window_attn",
    )(q, k, v, kc, vc, sink)


def _scan_block(a_scr, u_scr, h_scr, base, col, c_in, row8, reverse):
    order = range(SUBLANES - 1, -1, -1) if reverse else range(SUBLANES)
    a_cum, h_loc = [], []
    for j in order:
        aj = a_scr[col, base + j * SUBLANES:base + (j + 1) * SUBLANES, :]
        uj = u_scr[col, base + j * SUBLANES:base + (j + 1) * SUBLANES, :]
        if a_cum:
            h_loc.append(aj * h_loc[-1] + uj)
            a_cum.append(aj * a_cum[-1])
        else:
            h_loc.append(uj)
            a_cum.append(aj)
    p, q = a_cum[-1], h_loc[-1]
    for s in (1, 2, 4):
        shift = SUBLANES - s if reverse else s
        valid = (row8 < SUBLANES - s) if reverse else (row8 >= s)
        q = jnp.where(valid, p * pltpu.roll(q, shift, 0) + q, q)
        p = jnp.where(valid, p * pltpu.roll(p, shift, 0), p)
    end = q + p * c_in
    if reverse:
        c_grp = jnp.where(row8 < SUBLANES - 1, pltpu.roll(end, SUBLANES - 1, 0), c_in)
        last = end[0:1]
    else:
        c_grp = jnp.where(row8 >= 1, pltpu.roll(end, 1, 0), c_in)
        last = end[SUBLANES - 1:SUBLANES]
    for idx, j in enumerate(order):
        h_scr[col, pl.ds(base + j, SUBLANES, stride=SUBLANES), :] = h_loc[idx] + a_cum[idx] * c_grp
    return jnp.broadcast_to(last, (SUBLANES, LANES))


def _lru_chunk(pad_ref, conv_ref, t0, rows, d, carry, reverse, cw_ref, cb_ref, wg_ref, bg_ref,
               quarter_nsp, a_scr, u_scr, h_scr, row8):
    if reverse:
        xc = conv_ref[pl.ds(t0, rows), :]
    else:
        halo = SUBLANES
        win = pad_ref[pl.ds(t0, rows + 2 * halo), :]
        xc = cb_ref[...]
        for j in range(CONV_W):
            off = j - CONV_LEFT
            tap = win if off == 0 else pltpu.roll(win, (-off) % (rows + 2 * halo), 0)
            xc = xc + tap[halo:halo + rows] * cw_ref[j:j + 1, :]
        conv_ref[pl.ds(t0, rows), :] = xc
    xb = xc.astype(BF16)
    half = LRU_WIDTH // 2
    y0 = jnp.dot(xb[:, :half], wg_ref[d, 0], preferred_element_type=F32)
    y1 = jnp.dot(xb[:, half:], wg_ref[d, 1], preferred_element_type=F32)
    ya = jnp.concatenate([y0[:, :half], y1[:, :half]], axis=-1)
    yi = jnp.concatenate([y0[:, half:], y1[:, half:]], axis=-1)
    t_r = jnp.tanh(ya + bg_ref[2 * d:2 * d + 1, :])
    t_i = jnp.tanh(yi + bg_ref[2 * d + 1:2 * d + 2, :])
    t = jnp.tanh(t_r * quarter_nsp + quarter_nsp)
    inv = 1.0 / (1.0 - t)
    a = (1.0 + t) * inv
    neg_t = -t
    root = jnp.where(neg_t > 0.0, neg_t * lax.rsqrt(neg_t), 0.0)
    u = root * inv * ((t_i + 1.0) * xc)
    for col in range(LRU_WIDTH // LANES):
        for i in range(rows // SUBLANES):
            blk, g = divmod(i, SUBLANES)
            dst = pl.ds(blk * SCAN_BLOCK + g, SUBLANES, stride=SUBLANES)
            src = (slice(i * SUBLANES, (i + 1) * SUBLANES), slice(col * LANES, (col + 1) * LANES))
            a_scr[col, dst, :] = a[src]
            u_scr[col, dst, :] = u[src]
    n_blk = rows // SCAN_BLOCK
    blocks = range(n_blk - 1, -1, -1) if reverse else range(n_blk)
    carry = list(carry)
    for blk in blocks:
        for col in range(LRU_WIDTH // LANES):
            carry[col] = _scan_block(a_scr, u_scr, h_scr, blk * SCAN_BLOCK, col, carry[col],
                                     row8, reverse)
    return tuple(carry)


def _lru_kernel(xc_ref, gc_ref, x_ref, g_ref, cw_ref, cb_ref, wg_ref, bg_ref, lam_ref, *rest,
                with_ctx):
    if with_ctx:
        o_ref, oc_ref, xpad, xcpad, xconv, xcconv, a_scr, u_scr, h_scr, hf, hfc = rest
    else:
        o_ref, xpad, xcpad, xconv, xcconv, a_scr, u_scr, h_scr, hf = rest
    n = x_ref.shape[1]
    m = xc_ref.shape[1]
    halo = SUBLANES
    zeros = jnp.zeros((halo, LRU_WIDTH), F32)
    for pad, src, length in ((xpad, x_ref, n), (xcpad, xc_ref, m)):
        pad[0:halo, :] = zeros
        pad[halo:halo + length, :] = src[0]
        pad[halo + length:2 * halo + length, :] = zeros

    neg_lam = -lam_ref[...]
    softplus = jnp.maximum(neg_lam, 0.0) + jnp.log1p(jnp.exp(-jnp.abs(neg_lam)))
    quarter_nsp = (-0.25 * LRU_C) * softplus
    row8 = lax.broadcasted_iota(jnp.int32, (SUBLANES, LANES), 0)
    zero_carry = tuple(jnp.zeros((SUBLANES, LANES), F32) for _ in range(LRU_WIDTH // LANES))
    n_chunks = n // LRU_CHUNK

    def chunk(ctx_part, t0, rows, d, carry):
        pad_ref, conv_ref = (xcpad, xcconv) if ctx_part else (xpad, xconv)
        return _lru_chunk(pad_ref, conv_ref, t0, rows, d, carry, d == 1, cw_ref, cb_ref, wg_ref, bg_ref,
                          quarter_nsp[d:d + 1, :], a_scr, u_scr, h_scr, row8)

    def read_h(rows):
        return jnp.concatenate([h_scr[col, 0:rows, :] for col in range(LRU_WIDTH // LANES)], axis=-1)

    carry = chunk(True, 0, m, 0, zero_carry)
    if with_ctx:
        hfc[...] = read_h(m)

    def fwd_body(ci, carry):
        t0 = pl.multiple_of(ci * LRU_CHUNK, LRU_CHUNK)
        carry = chunk(False, t0, LRU_CHUNK, 0, carry)
        hf[pl.ds(t0, LRU_CHUNK), :] = read_h(LRU_CHUNK)
        return carry

    lax.fori_loop(0, n_chunks, fwd_body, carry)

    carry = chunk(True, 0, m, 1, zero_carry)
    if with_ctx:
        oc_ref[0] = ((hfc[...] + read_h(m)) * jax.nn.gelu(gc_ref[0])).astype(oc_ref.dtype)

    def bwd_body(ci, carry):
        t0 = pl.multiple_of((n_chunks - 1 - ci) * LRU_CHUNK, LRU_CHUNK)
        carry = chunk(False, t0, LRU_CHUNK, 1, carry)
        hf[pl.ds(t0, LRU_CHUNK), :] = hf[pl.ds(t0, LRU_CHUNK), :] + read_h(LRU_CHUNK)
        return carry

    lax.fori_loop(0, n_chunks, bwd_body, carry)

    def gate_body(ci, _):
        t0 = pl.multiple_of(ci * LRU_CHUNK, LRU_CHUNK)
        gate = jax.nn.gelu(g_ref[0, pl.ds(t0, LRU_CHUNK), :])
        o_ref[0, pl.ds(t0, LRU_CHUNK), :] = (hf[pl.ds(t0, LRU_CHUNK), :] * gate).astype(o_ref.dtype)
        return 0

    lax.fori_loop(0, n_chunks, gate_body, 0)


def _lru(xc, gc, x, g, cw, cb, wg, bg, lam, *, with_ctx):
    b, n, w = x.shape
    m = xc.shape[1]
    full = lambda t: pl.BlockSpec((1,) + t.shape[1:], lambda i: (i, 0, 0))
    out_shape = [jax.ShapeDtypeStruct((b, n, w), BF16)]
    out_specs = [pl.BlockSpec((1, n, w), lambda i: (i, 0, 0))]
    scratch = [pltpu.VMEM((n + 2 * SUBLANES, w), F32), pltpu.VMEM((m + 2 * SUBLANES, w), F32),
               pltpu.VMEM((n, w), F32), pltpu.VMEM((m, w), F32),
               *[pltpu.VMEM((w // LANES, LRU_CHUNK, LANES), F32) for _ in range(3)],
               pltpu.VMEM((n, w), F32)]
    if with_ctx:
        out_shape.append(jax.ShapeDtypeStruct((b, m, w), BF16))
        out_specs.append(pl.BlockSpec((1, m, w), lambda i: (i, 0, 0)))
        scratch.append(pltpu.VMEM((m, w), F32))
    outs = pl.pallas_call(
        functools.partial(_lru_kernel, with_ctx=with_ctx),
        out_shape=out_shape,
        grid=(b,),
        in_specs=[full(xc), full(gc), full(x), full(g), _const_spec(cw.shape), _const_spec(cb.shape),
                  _const_spec(wg.shape), _const_spec(bg.shape), _const_spec(lam.shape)],
        out_specs=out_specs,
        scratch_shapes=scratch,
        compiler_params=_params("parallel"),
        name="rglru_ctx" if with_ctx else "rglru",
    )(xc, gc, x, g, cw, cb, wg, bg, lam)
    return (outs[0], outs[1]) if with_ctx else (outs[0], None)


def _post_kernel(fa_ref, fb_ref, fc_ref, x_ref, mod_ref, wo_ref, gpost_ref, gpre_ref, w1_ref, w2_ref,
                 gpm_ref, o_ref):
    mod = mod_ref[0]
    tm = x_ref.shape[1]
    sub = min(POST_SUB_TILE, tm)
    d_ff = w1_ref.shape[1]
    ff_chunk = d_ff // 4
    pre = []
    for r0 in range(0, tm, sub):
        rows = slice(r0, r0 + sub)
        y = (jnp.dot(fa_ref[0, rows, :], wo_ref[0:A_Q, :], preferred_element_type=F32)
             + jnp.dot(fb_ref[0, rows, :], wo_ref[A_Q:A_Q + B_Q, :], preferred_element_type=F32)
             + jnp.dot(fc_ref[0, rows, :], wo_ref[A_Q + B_Q:, :], preferred_element_type=F32))
        x1 = x_ref[0, rows, :] + mod[2:3] * _rms(y, gpost_ref[...])
        h2 = (_rms(x1, gpre_ref[...]) * (1.0 + mod[4:5]) + mod[3:4]).astype(BF16)
        pre.append((rows, x1, h2))
    for rows, x1, h2 in pre:
        acc = None
        for k in range(d_ff // ff_chunk):
            hk = jnp.dot(h2, w1_ref[:, k * ff_chunk:(k + 1) * ff_chunk], preferred_element_type=F32)
            hk = jnp.square(jnp.maximum(hk, 0.0)).astype(BF16)
            part = jnp.dot(hk, w2_ref[k * ff_chunk:(k + 1) * ff_chunk, :], preferred_element_type=F32)
            acc = part if acc is None else acc + part
        o_ref[0, rows, :] = x1 + mod[5:6] * _rms(acc, gpm_ref[...])


def _post(fa, fb, fc, x, mod, wo, gpost, gpre, w1, w2, gpm, name):
    b, n, d = x.shape
    tm = min(ROW_TILE, n)
    row_spec = lambda wd: pl.BlockSpec((1, tm, wd), lambda i, j: (i, j, 0))
    return pl.pallas_call(
        _post_kernel,
        out_shape=jax.ShapeDtypeStruct((b, n, d), F32),
        grid=(b, n // tm),
        in_specs=[row_spec(fa.shape[2]), row_spec(fb.shape[2]), row_spec(fc.shape[2]), row_spec(d),
                  pl.BlockSpec((1, 6, d), lambda i, j: (i, 0, 0)),
                  _const_spec(wo.shape), _const_spec(gpost.shape), _const_spec(gpre.shape),
                  _const_spec(w1.shape), _const_spec(w2.shape), _const_spec(gpm.shape)],
        out_specs=row_spec(d),
        compiler_params=_params("parallel", "parallel"),
        name=name,
    )(fa, fb, fc, x, mod, wo, gpost, gpre, w1, w2, gpm)


def _rope_tables(n):
    rows = n // GRID_W
    row = jnp.repeat(jnp.arange(rows, dtype=F32), GRID_W)
    col = jnp.tile(jnp.arange(GRID_W, dtype=F32), rows)
    half = HEAD_DIM // 2
    inv_freq = ROPE_BASE ** (-jnp.arange(0, half, 2, dtype=F32) / half)
    ang_r = row[:, None] * inv_freq
    ang_c = col[:, None] * inv_freq
    cr, sr, cc, sc = jnp.cos(ang_r), jnp.sin(ang_r), jnp.cos(ang_c), jnp.sin(ang_c)
    cos = jnp.concatenate([cr, cr, cc, cc], axis=-1)
    sin = jnp.concatenate([-sr, sr, -sc, sc], axis=-1)
    reps = LANES // HEAD_DIM
    return jnp.tile(cos, (1, reps)), jnp.tile(sin, (1, reps))


def _head_perm(base):
    idx = jnp.arange(HEAD_DIM)
    return jnp.concatenate([base + h * HEAD_DIM + idx for h in (0, 2, 1, 3)])


def _block_diag(w):
    nb, hb, _ = w.shape
    per = nb // 2
    eye = jnp.eye(per, dtype=w.dtype)
    blocks = w.reshape(2, per, hb, 1, hb) * eye[None, :, None, :, None]
    return blocks.reshape(2, per * hb, per * hb)


def kernel(x, c, ctx, c_ctx, w_mod, b_mod, g_pre_mix, g_post_mix, g_pre_mlp, g_post_mlp, w_in, g_q_a,
           g_k_a, sink_b, conv_w, conv_b, lru_w_a, lru_b_a, lru_w_i, lru_b_i, lru_lambda, w_out,
           w_mlp_in, w_mlp_out):
    b, n, d = x.shape
    depth = w_mod.shape[0]
    cos, sin = _rope_tables(n)
    seg = jnp.arange(A_Q) // HEAD_DIM
    bd = jnp.where(seg[:, None] == seg[None, :], 1.0 / HEAD_DIM, 0.0).astype(BF16)

    in_perm = jnp.concatenate([_head_perm(0), jnp.arange(A_Q, A_Q + 2 * A_KV),
                               _head_perm(A_Q + 2 * A_KV),
                               jnp.arange(A_Q + 2 * A_KV + B_Q, w_in.shape[2])])
    out_perm = jnp.concatenate([_head_perm(0), _head_perm(A_Q), jnp.arange(A_Q + B_Q, w_out.shape[1])])

    mod_rows = -(-(b + 1) // SUBLANES) * SUBLANES
    cc = jnp.zeros((mod_rows, d), F32).at[:b].set(c).at[b].set(c_ctx)

    mod_all = _modulation(cc, w_mod, b_mod[:, None, :])
    m = ctx.shape[1]
    flat = lambda t: t.reshape(1, b * m, t.shape[-1])
    unflat = lambda t: t.reshape(b, m, t.shape[-1])

    for l in range(depth):
        last = l == depth - 1
        mod_lat = mod_all[l, :b].reshape(b, 6, d)
        mod_ctx = mod_all[l, b].reshape(1, 6, d)

        w_in_l = w_in[l][:, in_perm].astype(BF16)
        gq = jnp.tile(g_q_a[l], A_Q_HEADS)[None, :]
        gk = jnp.tile(g_k_a[l], A_KV_HEADS)[None, :]
        gpre = g_pre_mix[l][None, :]
        qa, ka, va, qb, kb, vb, xr, gr = _inproj(x, mod_lat, gpre, w_in_l, gq, gk, bd, cos, sin, rope=True)
        qa_c, ka_c, va_c, qb_c, kb_c, vb_c, xr_c, gr_c = map(unflat, _inproj(
            flat(ctx), mod_ctx, gpre, w_in_l, gq, gk, bd, cos, sin, rope=False))

        feat_a = _dense_attn(qa, [(ka_c, va_c), (ka, va)], None, "global_attn")
        feat_b = _win_attn(qb, kb, vb, kb_c, vb_c, sink_b[l])

        wg = (0.5 * jnp.stack([jnp.concatenate([_block_diag(lru_w_a[l, dd]), _block_diag(lru_w_i[l, dd])],
                                               axis=-1) for dd in range(2)])).astype(BF16)
        bg = 0.5 * jnp.stack([lru_b_a[l, 0], lru_b_i[l, 0], lru_b_a[l, 1], lru_b_i[l, 1]])
        feat_c, feat_cc = _lru(xr_c, gr_c, xr, gr, conv_w[l], conv_b[l][None, :], wg, bg, lru_lambda[l],
                               with_ctx=not last)

        wo = w_out[l][out_perm].astype(BF16)
        post_w = (wo, g_post_mix[l][None, :], g_pre_mlp[l][None, :], w_mlp_in[l].astype(BF16),
                  w_mlp_out[l].astype(BF16), g_post_mlp[l][None, :])
        x = _post(feat_a, feat_b, feat_c, x, mod_lat, *post_w, name="post_mlp")
        if not last:
            feat_ac = _dense_attn(qa_c, [(ka_c, va_c)], None, "ctx_attn_a")
            feat_bc = _dense_attn(qb_c, [(kb_c, vb_c)], sink_b[l], "ctx_attn_b")
            ctx = unflat(_post(flat(feat_ac), flat(feat_bc), flat(feat_cc), flat(ctx), mod_ctx, *post_w,
                               name="post_mlp_ctx"))
    return x
```

```python
import functools

import jax
import jax.numpy as jnp
from jax import lax
from jax.experimental import pallas as pl
from jax.experimental.pallas import tpu as pltpu

F32 = jnp.float32
BF16 = jnp.bfloat16

GRID_W = 64
HEAD_DIM = 64
A_Q_HEADS = 4
A_KV_HEADS = 2
B_Q_HEADS = 4
B_KV_HEADS = 2
WINDOW = 128
LRU_WIDTH = 512
LRU_BLOCKS = 8
CONV_W = 4
CONV_LEFT = CONV_W // 2
LRU_C = 8.0
ROPE_BASE = 10000.0
EPS = 1e-6
NEG_INF = -1e30

A_Q = A_Q_HEADS * HEAD_DIM
A_KV = A_KV_HEADS * HEAD_DIM
B_Q = B_Q_HEADS * HEAD_DIM
B_KV = B_KV_HEADS * HEAD_DIM
Q_SCALE = HEAD_DIM ** -0.5

LANES = 128
SUBLANES = 8
VMEM_LIMIT = 56 * 1024 * 1024

ROW_TILE = 1024
INPROJ_TILE = 1024
SUB_TILE = 256
POST_SUB_TILE = 512
ATTN_Q_TILE = 1024
ATTN_CHAIN = 512
WIN_GROUP = 8
WIN_CHAINS = 2
LRU_CHUNK = 1024
SCAN_BLOCK = SUBLANES * SUBLANES

_NT = (((1,), (1,)), ((), ()))


def _const_spec(shape):
    return pl.BlockSpec(shape, lambda *_: (0,) * len(shape), pipeline_mode=pl.Buffered(1))


def _params(*sem):
    return pltpu.CompilerParams(dimension_semantics=sem, vmem_limit_bytes=VMEM_LIMIT)


def _rms(x, g):
    return x * lax.rsqrt(jnp.mean(x * x, axis=-1, keepdims=True) + EPS) * g


def _mod_kernel(c_ref, w_ref, b_ref, o_ref):
    c = c_ref[...]
    act = (c * jax.nn.sigmoid(c)).astype(BF16)
    o_ref[...] = jnp.dot(act, w_ref[...].astype(BF16), preferred_element_type=F32) + b_ref[...]


def _modulation(cc, w, b):
    rows, d = cc.shape
    depth, _, n_out = w.shape
    tn = n_out // 4
    return pl.pallas_call(
        _mod_kernel,
        out_shape=jax.ShapeDtypeStruct((depth, rows, n_out), F32),
        grid=(depth, n_out // tn),
        in_specs=[
            pl.BlockSpec((rows, d), lambda l, j: (0, 0)),
            pl.BlockSpec((None, d, tn), lambda l, j: (l, 0, j)),
            pl.BlockSpec((None, 1, tn), lambda l, j: (l, 0, j)),
        ],
        out_specs=pl.BlockSpec((None, rows, tn), lambda l, j: (l, 0, j)),
        compiler_params=_params("arbitrary", "arbitrary"),
        name="modulation",
    )(cc, w, b)


def _head_rms(t, g, bd):
    ms = jnp.dot((t * t).astype(BF16), bd, preferred_element_type=F32)
    return t * lax.rsqrt(ms + EPS) * g


def _rope(t, cos, sin, low_half):
    outs = []
    for j in range(t.shape[1] // LANES):
        tc = t[:, j * LANES:(j + 1) * LANES]
        partner = jnp.where(low_half, pltpu.roll(tc, LANES - 16, 1), pltpu.roll(tc, 16, 1))
        outs.append(tc * cos + partner * sin)
    return outs[0] if len(outs) == 1 else jnp.concatenate(outs, axis=-1)


def _inproj_kernel(x_ref, mod_ref, gpre_ref, w_ref, gq_ref, gk_ref, bd_ref, cos_ref, sin_ref,
                   qa_ref, ka_ref, va_ref, qb_ref, kb_ref, vb_ref, xr_ref, gr_ref, *, rope):
    mod = mod_ref[0]
    bd = bd_ref[...]
    tm = x_ref.shape[1]
    sub = min(SUB_TILE, tm)
    pre = []
    for r0 in range(0, tm, sub):
        rows = slice(r0, r0 + sub)
        h = _rms(x_ref[0, rows, :], gpre_ref[...]) * (1.0 + mod[1:2]) + mod[0:1]
        pre.append((rows, h.astype(BF16)))
    for rows, hb in pre:
        z = jnp.dot(hb, w_ref[...], preferred_element_type=F32)
        qa = _head_rms(z[:, 0:A_Q], gq_ref[...], bd)
        ka = _head_rms(z[:, A_Q:A_Q + A_KV], gk_ref[...], bd[:A_KV, :A_KV])
        o = A_Q + 2 * A_KV
        qb = z[:, o:o + B_Q]
        kb = z[:, o + B_Q:o + B_Q + B_KV]
        if rope:
            cos = cos_ref[rows, :]
            sin = sin_ref[rows, :]
            lane = lax.broadcasted_iota(jnp.int32, cos.shape, 1)
            low_half = (lane % 32) < 16
            qa = _rope(qa, cos, sin, low_half)
            ka = _rope(ka, cos, sin, low_half)
            qb = _rope(qb, cos, sin, low_half)
            kb = _rope(kb, cos, sin, low_half)
        qa_ref[0, rows, :] = (qa * Q_SCALE).astype(BF16)
        ka_ref[0, rows, :] = ka.astype(BF16)
        va_ref[0, rows, :] = z[:, A_Q + A_KV:o].astype(BF16)
        qb_ref[0, rows, :] = (qb * Q_SCALE).astype(BF16)
        kb_ref[0, rows, :] = kb.astype(BF16)
        vb_ref[0, rows, :] = z[:, o + B_Q + B_KV:o + B_Q + 2 * B_KV].astype(BF16)
        o2 = o + B_Q + 2 * B_KV
        xr_ref[0, rows, :] = z[:, o2:o2 + LRU_WIDTH]
        gr_ref[0, rows, :] = z[:, o2 + LRU_WIDTH:o2 + 2 * LRU_WIDTH]


def _inproj(x, mod, gpre, w, gq, gk, bd, cos, sin, *, rope):
    b, n, d = x.shape
    tm = min(INPROJ_TILE, n)
    nt = n // tm
    widths = (A_Q, A_KV, A_KV, B_Q, B_KV, B_KV, LRU_WIDTH, LRU_WIDTH)
    dtypes = (BF16,) * 6 + (F32, F32)
    row_spec = lambda wd: pl.BlockSpec((1, tm, wd), lambda i, j: (i, j, 0))
    table_spec = pl.BlockSpec((tm, LANES), (lambda i, j: (j, 0)) if rope else (lambda i, j: (0, 0)))
    return pl.pallas_call(
        functools.partial(_inproj_kernel, rope=rope),
        out_shape=[jax.ShapeDtypeStruct((b, n, wd), dt) for wd, dt in zip(widths, dtypes)],
        grid=(b, nt),
        in_specs=[
            row_spec(d),
            pl.BlockSpec((1, 6, d), lambda i, j: (i, 0, 0)),
            _const_spec(gpre.shape),
            _const_spec(w.shape),
            _const_spec(gq.shape),
            _const_spec(gk.shape),
            _const_spec(bd.shape),
            table_spec,
            table_spec,
        ],
        out_specs=[row_spec(wd) for wd in widths],
        compiler_params=_params("parallel", "parallel"),
        name="inproj_rope" if rope else "inproj_ctx",
    )(x, mod, gpre, w, gq, gk, bd, cos, sin)


def _split_heads(qc, low):
    zero = jnp.zeros_like(qc)
    return jnp.where(low, qc, zero), jnp.where(low, zero, qc)


def _stack_heads(q, low):
    parts = []
    for j in range(q.shape[1] // LANES):
        parts.extend(_split_heads(q[:, j * LANES:(j + 1) * LANES], low))
    return jnp.concatenate(parts, axis=0)


def _unstack_heads(o, t, low):
    return jnp.concatenate([jnp.where(low, o[0:t], o[t:2 * t]),
                            jnp.where(low, o[2 * t:3 * t], o[3 * t:4 * t])], axis=-1)


def _with_ones(v):
    return jnp.concatenate([v, jnp.ones_like(v)], axis=-1)


def _chain_sizes(total):
    assert total % ATTN_CHAIN == 0
    half = ATTN_CHAIN // 2
    return (half,) + (ATTN_CHAIN,) * (total // ATTN_CHAIN - 1) + (half,)


def _dense_attn_kernel(*refs, n_kv, has_sink):
    q_ref = refs[0]
    kv_refs = refs[1:1 + 2 * n_kv]
    sink_ref = refs[1 + 2 * n_kv] if has_sink else None
    o_ref = refs[-1]
    q = q_ref[0]
    tq = q.shape[0]
    low = lax.broadcasted_iota(jnp.int32, (tq, LANES), 1) < HEAD_DIM
    qs_all = _stack_heads(q, low)
    keys = [kv_refs[2 * i][0] for i in range(n_kv)]
    vals = [_with_ones(kv_refs[2 * i + 1][0]) for i in range(n_kv)]
    if has_sink:
        sink_all = jnp.concatenate([jnp.full((tq, 1), sink_ref[h], F32) for h in (0, 2, 1, 3)], axis=0)
    outs, r0 = [], 0
    for rows in _chain_sizes(4 * tq):
        qs = qs_all[r0:r0 + rows]
        scores = [lax.dot_general(qs, k, _NT, preferred_element_type=F32) for k in keys]
        s = scores[0] if n_kv == 1 else jnp.concatenate(scores, axis=-1)
        m = jnp.max(s, axis=-1, keepdims=True)
        if has_sink:
            sink = sink_all[r0:r0 + rows]
            m = jnp.maximum(m, sink)
        p = jnp.exp(s - m).astype(BF16)
        acc, c0 = None, 0
        for k, v in zip(keys, vals):
            pv = jnp.dot(p[:, c0:c0 + k.shape[0]], v, preferred_element_type=F32)
            acc = pv if acc is None else acc + pv
            c0 += k.shape[0]
        denom = acc[:, LANES:]
        if has_sink:
            denom = denom + jnp.exp(sink - m)
        outs.append(acc[:, :LANES] / denom)
        r0 += rows
    o_ref[0] = _unstack_heads(jnp.concatenate(outs, axis=0), tq, low).astype(o_ref.dtype)


def _dense_attn(q, kvs, sink, name):
    b, n, w = q.shape
    tq = min(ATTN_Q_TILE, n)
    in_specs = [pl.BlockSpec((1, tq, w), lambda i, j: (i, j, 0))]
    args = [q]
    for k, v in kvs:
        for t in (k, v):
            in_specs.append(pl.BlockSpec((1,) + t.shape[1:], lambda i, j: (i, 0, 0)))
            args.append(t)
    if sink is not None:
        in_specs.append(pl.BlockSpec(memory_space=pltpu.SMEM))
        args.append(sink)
    return pl.pallas_call(
        functools.partial(_dense_attn_kernel, n_kv=len(kvs), has_sink=sink is not None),
        out_shape=jax.ShapeDtypeStruct((b, n, w), BF16),
        grid=(b, n // tq),
        in_specs=in_specs,
        out_specs=pl.BlockSpec((1, tq, w), lambda i, j: (i, j, 0)),
        compiler_params=_params("parallel", "parallel"),
        name=name,
    )(*args)


def _win_attn_kernel(q_ref, k_ref, v_ref, kc_ref, vc_ref, sink_ref, o_ref):
    n = q_ref.shape[1]
    qb = WINDOW
    band = 3 * qb
    rows = WIN_GROUP * qb
    n_iter = n // rows
    low = lax.broadcasted_iota(jnp.int32, (qb, LANES), 1) < HEAD_DIM
    row_minus_col = (lax.broadcasted_iota(jnp.int32, (qb, band), 0)
                     - lax.broadcasted_iota(jnp.int32, (qb, band), 1))

    def band_bias(first_key_offset):
        bias = jnp.where(jnp.abs(row_minus_col + first_key_offset) <= WINDOW, 0.0, NEG_INF)
        return jnp.concatenate([bias.astype(F32)] * 4, axis=0)

    bias_first, bias_mid, bias_last = band_bias(0), band_bias(qb), band_bias(2 * qb)
    kc = kc_ref[0]
    vc = _with_ones(vc_ref[0])
    ones_band = jnp.ones((band, LANES), BF16)
    sink_blk = jnp.concatenate([jnp.full((qb, 1), sink_ref[h], F32) for h in (0, 2, 1, 3)], axis=0)
    per_chain = WIN_GROUP // WIN_CHAINS
    sink = jnp.concatenate([sink_blk] * per_chain, axis=0)

    def chain(it, t0, q, blocks):
        qs_blk = [_stack_heads(q[g * qb:(g + 1) * qb], low) for g in blocks]
        starts = [pl.multiple_of(jnp.clip(t0 + (g - 1) * qb, 0, n - band), qb) for g in blocks]
        s_ctx = lax.dot_general(jnp.concatenate(qs_blk, axis=0), kc, _NT, preferred_element_type=F32)
        s_band = []
        for i, g in enumerate(blocks):
            s = lax.dot_general(qs_blk[i], k_ref[0, pl.ds(starts[i], band), :], _NT,
                                preferred_element_type=F32)
            bias = bias_mid
            if g == 0:
                bias = jnp.where(it == 0, bias_first, bias)
            if g == WIN_GROUP - 1:
                bias = jnp.where(it == n_iter - 1, bias_last, bias)
            s_band.append(s + bias)
        s = jnp.concatenate([jnp.concatenate(s_band, axis=0), s_ctx], axis=-1)
        m = jnp.maximum(jnp.max(s, axis=-1, keepdims=True), sink)
        p = jnp.exp(s - m).astype(BF16)
        acc_band = [jnp.dot(p[i * 4 * qb:(i + 1) * 4 * qb, :band],
                            jnp.concatenate([v_ref[0, pl.ds(starts[i], band), :], ones_band], axis=-1),
                            preferred_element_type=F32) for i in range(len(blocks))]
        acc = jnp.concatenate(acc_band, axis=0) + jnp.dot(p[:, band:], vc, preferred_element_type=F32)
        out = acc[:, :LANES] / (acc[:, LANES:] + jnp.exp(sink - m))
        return [_unstack_heads(out[i * 4 * qb:(i + 1) * 4 * qb], qb, low) for i in range(len(blocks))]

    def body(it, carry):
        t0 = pl.multiple_of(it * rows, rows)
        q = q_ref[0, pl.ds(t0, rows), :]
        outs = []
        for c in range(WIN_CHAINS):
            outs.extend(chain(it, t0, q, range(c * per_chain, (c + 1) * per_chain)))
        o_ref[0, pl.ds(t0, rows), :] = jnp.concatenate(outs, axis=0).astype(o_ref.dtype)
        return carry

    lax.fori_loop(0, n_iter, body, 0)


def _win_attn(q, k, v, kc, vc, sink):
    b, n, w = q.shape
    full = lambda t: pl.BlockSpec((1,) + t.shape[1:], lambda i: (i, 0, 0))
    return pl.pallas_call(
        _win_attn_kernel,
        out_shape=jax.ShapeDtypeStruct((b, n, w), BF16),
        grid=(b,),
        in_specs=[full(q), full(k), full(v), full(kc), full(vc),
                  pl.BlockSpec(memory_space=pltpu.SMEM)],
        out_specs=full(q),
        compiler_params=_params("parallel"),
        name="window_attn",
    )(q, k, v, kc, vc, sink)


def _gelu_tanh(x):
    c = (2.0 / jnp.pi) ** 0.5
    half_x = 0.5 * x
    return half_x + half_x * jnp.tanh(x * (c + (c * 0.044715) * (x * x)))


def _scan_block(a_scr, u_scr, h_scr, base, col, c_in, row8, reverse):
    order = range(SUBLANES - 1, -1, -1) if reverse else range(SUBLANES)
    a_cum, h_loc = [], []
    for j in order:
        aj = a_scr[col, base + j * SUBLANES:base + (j + 1) * SUBLANES, :]
        uj = u_scr[col, base + j * SUBLANES:base + (j + 1) * SUBLANES, :]
        if a_cum:
            h_loc.append(aj * h_loc[-1] + uj)
            a_cum.append(aj * a_cum[-1])
        else:
            h_loc.append(uj)
            a_cum.append(aj)
    p, q = a_cum[-1], h_loc[-1]
    for s in (1, 2, 4):
        shift = SUBLANES - s if reverse else s
        valid = (row8 < SUBLANES - s) if reverse else (row8 >= s)
        q = jnp.where(valid, p * pltpu.roll(q, shift, 0) + q, q)
        p = jnp.where(valid, p * pltpu.roll(p, shift, 0), p)
    end = q + p * c_in
    if reverse:
        c_grp = jnp.where(row8 < SUBLANES - 1, pltpu.roll(end, SUBLANES - 1, 0), c_in)
        last = end[0:1]
    else:
        c_grp = jnp.where(row8 >= 1, pltpu.roll(end, 1, 0), c_in)
        last = end[SUBLANES - 1:SUBLANES]
    for idx, j in enumerate(order):
        h_scr[col, pl.ds(base + j, SUBLANES, stride=SUBLANES), :] = h_loc[idx] + a_cum[idx] * c_grp
    return jnp.broadcast_to(last, (SUBLANES, LANES))


def _lru_chunk(src_ref, conv_ref, t0, rows, d, carry, reverse, cw_ref, cb_ref, wg_ref, bg_ref,
               quarter_nsp, a_scr, u_scr, h_scr, row8):
    if reverse:
        xc = conv_ref[pl.ds(t0, rows), :]
    else:
        halo = SUBLANES
        length = src_ref.shape[1]
        lo = jnp.maximum(t0 - halo, 0)
        hi = jnp.minimum(t0 + rows, length - halo)
        prev = jnp.where(t0 > 0, src_ref[0, pl.ds(pl.multiple_of(lo, halo), halo), :], 0.0)
        nxt = jnp.where(t0 + rows < length, src_ref[0, pl.ds(pl.multiple_of(hi, halo), halo), :], 0.0)
        win = jnp.concatenate([prev, src_ref[0, pl.ds(t0, rows), :], nxt], axis=0)
        xc = cb_ref[...]
        for j in range(CONV_W):
            off = j - CONV_LEFT
            tap = win if off == 0 else pltpu.roll(win, (-off) % (rows + 2 * halo), 0)
            xc = xc + tap[halo:halo + rows] * cw_ref[j:j + 1, :]
        conv_ref[pl.ds(t0, rows), :] = xc
    xb = xc.astype(BF16)
    half = LRU_WIDTH // 2
    y0 = jnp.dot(xb[:, :half], wg_ref[d, 0], preferred_element_type=F32)
    y1 = jnp.dot(xb[:, half:], wg_ref[d, 1], preferred_element_type=F32)
    ya = jnp.concatenate([y0[:, :half], y1[:, :half]], axis=-1)
    yi = jnp.concatenate([y0[:, half:], y1[:, half:]], axis=-1)
    t_r = jnp.tanh(ya + bg_ref[2 * d:2 * d + 1, :])
    t_i = jnp.tanh(yi + bg_ref[2 * d + 1:2 * d + 2, :])
    t = jnp.tanh(t_r * quarter_nsp + quarter_nsp)
    inv = 1.0 / (1.0 - t)
    a = (1.0 + t) * inv
    neg_t = -t
    root = jnp.where(neg_t > 0.0, neg_t * lax.rsqrt(neg_t), 0.0)
    u = root * inv * ((t_i + 1.0) * xc)
    for col in range(LRU_WIDTH // LANES):
        for i in range(rows // SUBLANES):
            blk, g = divmod(i, SUBLANES)
            dst = pl.ds(blk * SCAN_BLOCK + g, SUBLANES, stride=SUBLANES)
            src = (slice(i * SUBLANES, (i + 1) * SUBLANES), slice(col * LANES, (col + 1) * LANES))
            a_scr[col, dst, :] = a[src]
            u_scr[col, dst, :] = u[src]
    n_blk = rows // SCAN_BLOCK
    blocks = range(n_blk - 1, -1, -1) if reverse else range(n_blk)
    carry = list(carry)
    for blk in blocks:
        for col in range(LRU_WIDTH // LANES):
            carry[col] = _scan_block(a_scr, u_scr, h_scr, blk * SCAN_BLOCK, col, carry[col],
                                     row8, reverse)
    return tuple(carry)


def _lru_kernel(xc_ref, gc_ref, x_ref, g_ref, cw_ref, cb_ref, wg_ref, bg_ref, lam_ref, *rest,
                with_ctx):
    if with_ctx:
        o_ref, oc_ref, xconv, xcconv, a_scr, u_scr, h_scr, hf, hfc = rest
    else:
        o_ref, xconv, xcconv, a_scr, u_scr, h_scr, hf = rest
    n = x_ref.shape[1]
    m = xc_ref.shape[1]

    neg_lam = -lam_ref[...]
    softplus = jnp.maximum(neg_lam, 0.0) + jnp.log1p(jnp.exp(-jnp.abs(neg_lam)))
    quarter_nsp = (-0.25 * LRU_C) * softplus
    row8 = lax.broadcasted_iota(jnp.int32, (SUBLANES, LANES), 0)
    zero_carry = tuple(jnp.zeros((SUBLANES, LANES), F32) for _ in range(LRU_WIDTH // LANES))
    n_chunks = n // LRU_CHUNK

    def chunk(ctx_part, t0, rows, d, carry):
        src_ref, conv_ref = (xc_ref, xcconv) if ctx_part else (x_ref, xconv)
        return _lru_chunk(src_ref, conv_ref, t0, rows, d, carry, d == 1, cw_ref, cb_ref, wg_ref, bg_ref,
                          quarter_nsp[d:d + 1, :], a_scr, u_scr, h_scr, row8)

    def read_h(rows):
        return jnp.concatenate([h_scr[col, 0:rows, :] for col in range(LRU_WIDTH // LANES)], axis=-1)

    carry = chunk(True, 0, m, 0, zero_carry)
    if with_ctx:
        hfc[...] = read_h(m)

    def fwd_body(ci, carry):
        t0 = pl.multiple_of(ci * LRU_CHUNK, LRU_CHUNK)
        carry = chunk(False, t0, LRU_CHUNK, 0, carry)
        hf[pl.ds(t0, LRU_CHUNK), :] = read_h(LRU_CHUNK)
        return carry

    lax.fori_loop(0, n_chunks, fwd_body, carry)

    carry = chunk(True, 0, m, 1, zero_carry)
    if with_ctx:
        oc_ref[0] = ((hfc[...] + read_h(m)) * _gelu_tanh(gc_ref[0])).astype(oc_ref.dtype)

    def bwd_body(ci, carry):
        t0 = pl.multiple_of((n_chunks - 1 - ci) * LRU_CHUNK, LRU_CHUNK)
        carry = chunk(False, t0, LRU_CHUNK, 1, carry)
        hf[pl.ds(t0, LRU_CHUNK), :] = hf[pl.ds(t0, LRU_CHUNK), :] + read_h(LRU_CHUNK)
        return carry

    lax.fori_loop(0, n_chunks, bwd_body, carry)

    def gate_body(ci, _):
        t0 = pl.multiple_of(ci * LRU_CHUNK, LRU_CHUNK)
        gate = _gelu_tanh(g_ref[0, pl.ds(t0, LRU_CHUNK), :])
        o_ref[0, pl.ds(t0, LRU_CHUNK), :] = (hf[pl.ds(t0, LRU_CHUNK), :] * gate).astype(o_ref.dtype)
        return 0

    lax.fori_loop(0, n_chunks, gate_body, 0)


def _lru(xc, gc, x, g, cw, cb, wg, bg, lam, *, with_ctx):
    b, n, w = x.shape
    m = xc.shape[1]
    full = lambda t: pl.BlockSpec((1,) + t.shape[1:], lambda i: (i, 0, 0))
    out_shape = [jax.ShapeDtypeStruct((b, n, w), BF16)]
    out_specs = [pl.BlockSpec((1, n, w), lambda i: (i, 0, 0))]
    scratch = [pltpu.VMEM((n, w), F32), pltpu.VMEM((m, w), F32),
               *[pltpu.VMEM((w // LANES, LRU_CHUNK, LANES), F32) for _ in range(3)],
               pltpu.VMEM((n, w), F32)]
    if with_ctx:
        out_shape.append(jax.ShapeDtypeStruct((b, m, w), BF16))
        out_specs.append(pl.BlockSpec((1, m, w), lambda i: (i, 0, 0)))
        scratch.append(pltpu.VMEM((m, w), F32))
    outs = pl.pallas_call(
        functools.partial(_lru_kernel, with_ctx=with_ctx),
        out_shape=out_shape,
        grid=(b,),
        in_specs=[full(xc), full(gc), full(x), full(g), _const_spec(cw.shape), _const_spec(cb.shape),
                  _const_spec(wg.shape), _const_spec(bg.shape), _const_spec(lam.shape)],
        out_specs=out_specs,
        scratch_shapes=scratch,
        compiler_params=_params("parallel"),
        name="rglru_ctx" if with_ctx else "rglru",
    )(xc, gc, x, g, cw, cb, wg, bg, lam)
    return (outs[0], outs[1]) if with_ctx else (outs[0], None)


def _post_kernel(fa_ref, fb_ref, fc_ref, x_ref, mod_ref, wo_ref, gpost_ref, gpre_ref, w1_ref, w2_ref,
                 gpm_ref, o_ref):
    mod = mod_ref[0]
    tm = x_ref.shape[1]
    sub = min(POST_SUB_TILE, tm)
    d_ff = w1_ref.shape[1]
    ff_chunk = d_ff // 4
    pre = []
    for r0 in range(0, tm, sub):
        rows = slice(r0, r0 + sub)
        y = (jnp.dot(fa_ref[0, rows, :], wo_ref[0:A_Q, :], preferred_element_type=F32)
             + jnp.dot(fb_ref[0, rows, :], wo_ref[A_Q:A_Q + B_Q, :], preferred_element_type=F32)
             + jnp.dot(fc_ref[0, rows, :], wo_ref[A_Q + B_Q:, :], preferred_element_type=F32))
        x1 = x_ref[0, rows, :] + mod[2:3] * _rms(y, gpost_ref[...])
        h2 = (_rms(x1, gpre_ref[...]) * (1.0 + mod[4:5]) + mod[3:4]).astype(BF16)
        pre.append((rows, x1, h2))
    for rows, x1, h2 in pre:
        acc = None
        for k in range(d_ff // ff_chunk):
            hk = jnp.dot(h2, w1_ref[:, k * ff_chunk:(k + 1) * ff_chunk], preferred_element_type=F32)
            hk = jnp.square(jnp.maximum(hk, 0.0)).astype(BF16)
            part = jnp.dot(hk, w2_ref[k * ff_chunk:(k + 1) * ff_chunk, :], preferred_element_type=F32)
            acc = part if acc is None else acc + part
        o_ref[0, rows, :] = x1 + mod[5:6] * _rms(acc, gpm_ref[...])


def _post(fa, fb, fc, x, mod, wo, gpost, gpre, w1, w2, gpm, name):
    b, n, d = x.shape
    tm = min(ROW_TILE, n)
    row_spec = lambda wd: pl.BlockSpec((1, tm, wd), lambda i, j: (i, j, 0))
    return pl.pallas_call(
        _post_kernel,
        out_shape=jax.ShapeDtypeStruct((b, n, d), F32),
        grid=(b, n // tm),
        in_specs=[row_spec(fa.shape[2]), row_spec(fb.shape[2]), row_spec(fc.shape[2]), row_spec(d),
                  pl.BlockSpec((1, 6, d), lambda i, j: (i, 0, 0)),
                  _const_spec(wo.shape), _const_spec(gpost.shape), _const_spec(gpre.shape),
                  _const_spec(w1.shape), _const_spec(w2.shape), _const_spec(gpm.shape)],
        out_specs=row_spec(d),
        compiler_params=_params("parallel", "parallel"),
        name=name,
    )(fa, fb, fc, x, mod, wo, gpost, gpre, w1, w2, gpm)


def _rope_tables(n):
    rows = n // GRID_W
    row = jnp.repeat(jnp.arange(rows, dtype=F32), GRID_W)
    col = jnp.tile(jnp.arange(GRID_W, dtype=F32), rows)
    half = HEAD_DIM // 2
    inv_freq = ROPE_BASE ** (-jnp.arange(0, half, 2, dtype=F32) / half)
    ang_r = row[:, None] * inv_freq
    ang_c = col[:, None] * inv_freq
    cr, sr, cc, sc = jnp.cos(ang_r), jnp.sin(ang_r), jnp.cos(ang_c), jnp.sin(ang_c)
    cos = jnp.concatenate([cr, cr, cc, cc], axis=-1)
    sin = jnp.concatenate([-sr, sr, -sc, sc], axis=-1)
    reps = LANES // HEAD_DIM
    return jnp.tile(cos, (1, reps)), jnp.tile(sin, (1, reps))


def _permute_heads(w, axis, bases):
    pieces, pos = [], 0
    for base in bases:
        pieces.append(lax.slice_in_dim(w, pos, base, axis=axis))
        for h in (0, 2, 1, 3):
            pieces.append(lax.slice_in_dim(w, base + h * HEAD_DIM, base + (h + 1) * HEAD_DIM, axis=axis))
        pos = base + 4 * HEAD_DIM
    pieces.append(lax.slice_in_dim(w, pos, w.shape[axis], axis=axis))
    return jnp.concatenate([p for p in pieces if p.shape[axis]], axis=axis)


def _block_diag(w):
    nb, hb, _ = w.shape
    per = nb // 2
    eye = jnp.eye(per, dtype=w.dtype)
    blocks = w.reshape(2, per, hb, 1, hb) * eye[None, :, None, :, None]
    return blocks.reshape(2, per * hb, per * hb)


def kernel(x, c, ctx, c_ctx, w_mod, b_mod, g_pre_mix, g_post_mix, g_pre_mlp, g_post_mlp, w_in, g_q_a,
           g_k_a, sink_b, conv_w, conv_b, lru_w_a, lru_b_a, lru_w_i, lru_b_i, lru_lambda, w_out,
           w_mlp_in, w_mlp_out):
    b, n, d = x.shape
    depth = w_mod.shape[0]
    cos, sin = _rope_tables(n)
    seg = jnp.arange(A_Q) // HEAD_DIM
    bd = jnp.where(seg[:, None] == seg[None, :], 1.0 / HEAD_DIM, 0.0).astype(BF16)

    mod_rows = -(-(b + 1) // SUBLANES) * SUBLANES
    cc = jnp.zeros((mod_rows, d), F32).at[:b].set(c).at[b].set(c_ctx)

    mod_all = _modulation(cc, w_mod, b_mod[:, None, :])
    m = ctx.shape[1]
    flat = lambda t: t.reshape(1, b * m, t.shape[-1])
    unflat = lambda t: t.reshape(b, m, t.shape[-1])

    for l in range(depth):
        last = l == depth - 1
        mod_lat = mod_all[l, :b].reshape(b, 6, d)
        mod_ctx = mod_all[l, b].reshape(1, 6, d)

        w_in_l = _permute_heads(w_in[l], 1, (0, A_Q + 2 * A_KV)).astype(BF16)
        gq = jnp.tile(g_q_a[l], A_Q_HEADS)[None, :]
        gk = jnp.tile(g_k_a[l], A_KV_HEADS)[None, :]
        gpre = g_pre_mix[l][None, :]
        qa, ka, va, qb, kb, vb, xr, gr = _inproj(x, mod_lat, gpre, w_in_l, gq, gk, bd, cos, sin, rope=True)
        qa_c, ka_c, va_c, qb_c, kb_c, vb_c, xr_c, gr_c = map(unflat, _inproj(
            flat(ctx), mod_ctx, gpre, w_in_l, gq, gk, bd, cos, sin, rope=False))

        feat_a = _dense_attn(qa, [(ka_c, va_c), (ka, va)], None, "global_attn")
        feat_b = _win_attn(qb, kb, vb, kb_c, vb_c, sink_b[l])

        wg = (0.5 * jnp.stack([jnp.concatenate([_block_diag(lru_w_a[l, dd]), _block_diag(lru_w_i[l, dd])],
                                               axis=-1) for dd in range(2)])).astype(BF16)
        bg = 0.5 * jnp.stack([lru_b_a[l, 0], lru_b_i[l, 0], lru_b_a[l, 1], lru_b_i[l, 1]])
        feat_c, feat_cc = _lru(xr_c, gr_c, xr, gr, conv_w[l], conv_b[l][None, :], wg, bg, lru_lambda[l],
                               with_ctx=not last)

        wo = _permute_heads(w_out[l], 0, (0, A_Q)).astype(BF16)
        post_w = (wo, g_post_mix[l][None, :], g_pre_mlp[l][None, :], w_mlp_in[l].astype(BF16),
                  w_mlp_out[l].astype(BF16), g_post_mlp[l][None, :])
        x = _post(feat_a, feat_b, feat_c, x, mod_lat, *post_w, name="post_mlp")
        if not last:
            feat_ac = _dense_attn(qa_c, [(ka_c, va_c)], None, "ctx_attn_a")
            feat_bc = _dense_attn(qb_c, [(kb_c, vb_c)], sink_b[l], "ctx_attn_b")
            ctx = unflat(_post(flat(feat_ac), flat(feat_bc), flat(feat_cc), flat(ctx), mod_ctx, *post_w,
                               name="post_mlp_ctx"))
    return x
```

```python
import functools

import jax
import jax.numpy as jnp
from jax import lax
from jax.experimental import pallas as pl
from jax.experimental.pallas import tpu as pltpu

F32 = jnp.float32
BF16 = jnp.bfloat16

GRID_W = 64
HEAD_DIM = 64
A_Q_HEADS = 4
A_KV_HEADS = 2
B_Q_HEADS = 4
B_KV_HEADS = 2
WINDOW = 128
LRU_WIDTH = 512
LRU_BLOCKS = 8
CONV_W = 4
CONV_LEFT = CONV_W // 2
LRU_C = 8.0
ROPE_BASE = 10000.0
EPS = 1e-6
NEG_INF = -1e30

A_Q = A_Q_HEADS * HEAD_DIM
A_KV = A_KV_HEADS * HEAD_DIM
B_Q = B_Q_HEADS * HEAD_DIM
B_KV = B_KV_HEADS * HEAD_DIM
Q_SCALE = HEAD_DIM ** -0.5

LANES = 128
SUBLANES = 8
VMEM_LIMIT = 56 * 1024 * 1024

ROW_TILE = 1024
INPROJ_TILE = 1024
SUB_TILE = 256
POST_SUB_TILE = 512
ATTN_Q_TILE = 1024
ATTN_CHAIN = 512
WIN_GROUP = 16
WIN_CHAINS = 4
LRU_CHUNK = 1024
SCAN_BLOCK = SUBLANES * SUBLANES

_NT = (((1,), (1,)), ((), ()))


def _const_spec(shape):
    return pl.BlockSpec(shape, lambda *_: (0,) * len(shape), pipeline_mode=pl.Buffered(1))


def _params(*sem):
    return pltpu.CompilerParams(dimension_semantics=sem, vmem_limit_bytes=VMEM_LIMIT)


def _rms(x, g):
    return x * lax.rsqrt(jnp.mean(x * x, axis=-1, keepdims=True) + EPS) * g


def _mod_kernel(c_ref, w_ref, b_ref, o_ref):
    c = c_ref[...]
    act = (c * jax.nn.sigmoid(c)).astype(BF16)
    o_ref[...] = jnp.dot(act, w_ref[...].astype(BF16), preferred_element_type=F32) + b_ref[...]


def _modulation(cc, w, b):
    rows, d = cc.shape
    depth, _, n_out = w.shape
    tn = n_out // 4
    return pl.pallas_call(
        _mod_kernel,
        out_shape=jax.ShapeDtypeStruct((depth, rows, n_out), F32),
        grid=(depth, n_out // tn),
        in_specs=[
            pl.BlockSpec((rows, d), lambda l, j: (0, 0)),
            pl.BlockSpec((None, d, tn), lambda l, j: (l, 0, j)),
            pl.BlockSpec((None, 1, tn), lambda l, j: (l, 0, j)),
        ],
        out_specs=pl.BlockSpec((None, rows, tn), lambda l, j: (l, 0, j)),
        compiler_params=_params("arbitrary", "arbitrary"),
        name="modulation",
    )(cc, w, b)


def _head_rms(t, g, bd):
    ms = jnp.dot((t * t).astype(BF16), bd, preferred_element_type=F32)
    return t * lax.rsqrt(ms + EPS) * g


def _rope(t, cos, sin, low_half):
    outs = []
    for j in range(t.shape[1] // LANES):
        tc = t[:, j * LANES:(j + 1) * LANES]
        partner = jnp.where(low_half, pltpu.roll(tc, LANES - 16, 1), pltpu.roll(tc, 16, 1))
        outs.append(tc * cos + partner * sin)
    return outs[0] if len(outs) == 1 else jnp.concatenate(outs, axis=-1)


def _inproj_kernel(x_ref, mod_ref, gpre_ref, w_ref, gq_ref, gk_ref, bd_ref, cos_ref, sin_ref,
                   qa_ref, ka_ref, va_ref, qb_ref, kb_ref, vb_ref, xr_ref, gr_ref, *, rope):
    mod = mod_ref[0]
    bd = bd_ref[...]
    tm = x_ref.shape[1]
    sub = min(SUB_TILE, tm)
    pre = []
    for r0 in range(0, tm, sub):
        rows = slice(r0, r0 + sub)
        h = _rms(x_ref[0, rows, :], gpre_ref[...]) * (1.0 + mod[1:2]) + mod[0:1]
        pre.append((rows, h.astype(BF16)))
    for rows, hb in pre:
        z = jnp.dot(hb, w_ref[...], preferred_element_type=F32)
        qa = _head_rms(z[:, 0:A_Q], gq_ref[...], bd)
        ka = _head_rms(z[:, A_Q:A_Q + A_KV], gk_ref[...], bd[:A_KV, :A_KV])
        o = A_Q + 2 * A_KV
        qb = z[:, o:o + B_Q]
        kb = z[:, o + B_Q:o + B_Q + B_KV]
        if rope:
            cos = cos_ref[rows, :]
            sin = sin_ref[rows, :]
            lane = lax.broadcasted_iota(jnp.int32, cos.shape, 1)
            low_half = (lane % 32) < 16
            qa = _rope(qa, cos, sin, low_half)
            ka = _rope(ka, cos, sin, low_half)
            qb = _rope(qb, cos, sin, low_half)
            kb = _rope(kb, cos, sin, low_half)
        qa_ref[0, rows, :] = (qa * Q_SCALE).astype(BF16)
        ka_ref[0, rows, :] = ka.astype(BF16)
        va_ref[0, rows, :] = z[:, A_Q + A_KV:o].astype(BF16)
        qb_ref[0, rows, :] = (qb * Q_SCALE).astype(BF16)
        kb_ref[0, rows, :] = kb.astype(BF16)
        vb_ref[0, rows, :] = z[:, o + B_Q + B_KV:o + B_Q + 2 * B_KV].astype(BF16)
        o2 = o + B_Q + 2 * B_KV
        xr_ref[0, rows, :] = z[:, o2:o2 + LRU_WIDTH]
        gr_ref[0, rows, :] = z[:, o2 + LRU_WIDTH:o2 + 2 * LRU_WIDTH]


def _inproj(x, mod, gpre, w, gq, gk, bd, cos, sin, *, rope):
    b, n, d = x.shape
    tm = min(INPROJ_TILE, n)
    nt = n // tm
    widths = (A_Q, A_KV, A_KV, B_Q, B_KV, B_KV, LRU_WIDTH, LRU_WIDTH)
    dtypes = (BF16,) * 6 + (F32, F32)
    row_spec = lambda wd: pl.BlockSpec((1, tm, wd), lambda i, j: (i, j, 0))
    table_spec = pl.BlockSpec((tm, LANES), (lambda i, j: (j, 0)) if rope else (lambda i, j: (0, 0)))
    return pl.pallas_call(
        functools.partial(_inproj_kernel, rope=rope),
        out_shape=[jax.ShapeDtypeStruct((b, n, wd), dt) for wd, dt in zip(widths, dtypes)],
        grid=(b, nt),
        in_specs=[
            row_spec(d),
            pl.BlockSpec((1, 6, d), lambda i, j: (i, 0, 0)),
            _const_spec(gpre.shape),
            _const_spec(w.shape),
            _const_spec(gq.shape),
            _const_spec(gk.shape),
            _const_spec(bd.shape),
            table_spec,
            table_spec,
        ],
        out_specs=[row_spec(wd) for wd in widths],
        compiler_params=_params("parallel", "parallel"),
        name="inproj_rope" if rope else "inproj_ctx",
    )(x, mod, gpre, w, gq, gk, bd, cos, sin)


def _split_heads(qc, low):
    zero = jnp.zeros_like(qc)
    return jnp.where(low, qc, zero), jnp.where(low, zero, qc)


def _stack_heads(q, low):
    parts = []
    for j in range(q.shape[1] // LANES):
        parts.extend(_split_heads(q[:, j * LANES:(j + 1) * LANES], low))
    return jnp.concatenate(parts, axis=0)


def _unstack_heads(o, t, low):
    return jnp.concatenate([jnp.where(low, o[0:t], o[t:2 * t]),
                            jnp.where(low, o[2 * t:3 * t], o[3 * t:4 * t])], axis=-1)


def _with_ones(v):
    return jnp.concatenate([v, jnp.ones_like(v)], axis=-1)


def _chain_sizes(total):
    assert total % ATTN_CHAIN == 0
    half = ATTN_CHAIN // 2
    return (half,) + (ATTN_CHAIN,) * (total // ATTN_CHAIN - 1) + (half,)


def _dense_attn_kernel(*refs, n_kv, has_sink):
    q_ref = refs[0]
    kv_refs = refs[1:1 + 2 * n_kv]
    sink_ref = refs[1 + 2 * n_kv] if has_sink else None
    o_ref = refs[-1]
    q = q_ref[0]
    tq = q.shape[0]
    low = lax.broadcasted_iota(jnp.int32, (tq, LANES), 1) < HEAD_DIM
    qs_all = _stack_heads(q, low)
    keys = [kv_refs[2 * i][0] for i in range(n_kv)]
    vals = [_with_ones(kv_refs[2 * i + 1][0]) for i in range(n_kv)]
    if has_sink:
        sink_all = jnp.concatenate([jnp.full((tq, LANES), sink_ref[h], F32) for h in (0, 2, 1, 3)], axis=0)
    outs, r0 = [], 0
    for rows in _chain_sizes(4 * tq):
        qs = qs_all[r0:r0 + rows]
        scores = [lax.dot_general(qs, k, _NT, preferred_element_type=F32) for k in keys]
        s = scores[0] if n_kv == 1 else jnp.concatenate(scores, axis=-1)
        m = jnp.max(s, axis=-1, keepdims=True)
        if has_sink:
            sink = sink_all[r0:r0 + rows]
            m_wide = jnp.maximum(jnp.broadcast_to(m, sink.shape), sink)
            p = jnp.exp(s - jnp.concatenate([m_wide] * (s.shape[1] // LANES), axis=-1)).astype(BF16)
        else:
            p = jnp.exp(s - m).astype(BF16)
        acc, c0 = None, 0
        for k, v in zip(keys, vals):
            pv = jnp.dot(p[:, c0:c0 + k.shape[0]], v, preferred_element_type=F32)
            acc = pv if acc is None else acc + pv
            c0 += k.shape[0]
        denom = acc[:, LANES:]
        if has_sink:
            denom = denom + jnp.exp(sink - m_wide)
        outs.append(acc[:, :LANES] / denom)
        r0 += rows
    o_ref[0] = _unstack_heads(jnp.concatenate(outs, axis=0), tq, low).astype(o_ref.dtype)


def _dense_attn(q, kvs, sink, name):
    b, n, w = q.shape
    tq = min(ATTN_Q_TILE, n)
    in_specs = [pl.BlockSpec((1, tq, w), lambda i, j: (i, j, 0))]
    args = [q]
    for k, v in kvs:
        for t in (k, v):
            in_specs.append(pl.BlockSpec((1,) + t.shape[1:], lambda i, j: (i, 0, 0)))
            args.append(t)
    if sink is not None:
        in_specs.append(pl.BlockSpec(memory_space=pltpu.SMEM))
        args.append(sink)
    return pl.pallas_call(
        functools.partial(_dense_attn_kernel, n_kv=len(kvs), has_sink=sink is not None),
        out_shape=jax.ShapeDtypeStruct((b, n, w), BF16),
        grid=(b, n // tq),
        in_specs=in_specs,
        out_specs=pl.BlockSpec((1, tq, w), lambda i, j: (i, j, 0)),
        compiler_params=_params("parallel", "parallel"),
        name=name,
    )(*args)


def _win_attn_kernel(q_ref, k_ref, v_ref, kc_ref, vc_ref, sink_ref, o_ref):
    n = q_ref.shape[1]
    qb = WINDOW
    band = 3 * qb
    rows = WIN_GROUP * qb
    n_iter = n // rows
    low = lax.broadcasted_iota(jnp.int32, (qb, LANES), 1) < HEAD_DIM
    row_minus_col = (lax.broadcasted_iota(jnp.int32, (qb, band), 0)
                     - lax.broadcasted_iota(jnp.int32, (qb, band), 1))

    def band_bias(first_key_offset):
        bias = jnp.where(jnp.abs(row_minus_col + first_key_offset) <= WINDOW, 0.0, NEG_INF)
        return jnp.concatenate([bias.astype(F32)] * 4, axis=0)

    bias_first, bias_mid, bias_last = band_bias(0), band_bias(qb), band_bias(2 * qb)
    kc = kc_ref[0]
    vc = _with_ones(vc_ref[0])
    ones_band = jnp.ones((band, LANES), BF16)
    sink_blk = jnp.concatenate([jnp.full((qb, LANES), sink_ref[h], F32) for h in (0, 2, 1, 3)], axis=0)
    per_chain = WIN_GROUP // WIN_CHAINS
    sink = jnp.concatenate([sink_blk] * per_chain, axis=0)

    def chain(it, t0, q, blocks):
        qs_blk = [_stack_heads(q[g * qb:(g + 1) * qb], low) for g in blocks]
        starts = [pl.multiple_of(jnp.clip(t0 + (g - 1) * qb, 0, n - band), qb) for g in blocks]
        s_ctx = lax.dot_general(jnp.concatenate(qs_blk, axis=0), kc, _NT, preferred_element_type=F32)
        s_band = []
        for i, g in enumerate(blocks):
            s = lax.dot_general(qs_blk[i], k_ref[0, pl.ds(starts[i], band), :], _NT,
                                preferred_element_type=F32)
            bias = bias_mid
            if g == 0:
                bias = jnp.where(it == 0, bias_first, bias)
            if g == WIN_GROUP - 1:
                bias = jnp.where(it == n_iter - 1, bias_last, bias)
            s_band.append(s + bias)
        s = jnp.concatenate([jnp.concatenate(s_band, axis=0), s_ctx], axis=-1)
        m = jnp.maximum(jnp.broadcast_to(jnp.max(s, axis=-1, keepdims=True), sink.shape), sink)
        p = jnp.exp(s - jnp.concatenate([m] * (s.shape[1] // LANES), axis=-1)).astype(BF16)
        acc_band = [jnp.dot(p[i * 4 * qb:(i + 1) * 4 * qb, :band],
                            jnp.concatenate([v_ref[0, pl.ds(starts[i], band), :], ones_band], axis=-1),
                            preferred_element_type=F32) for i in range(len(blocks))]
        acc = jnp.concatenate(acc_band, axis=0) + jnp.dot(p[:, band:], vc, preferred_element_type=F32)
        out = acc[:, :LANES] / (acc[:, LANES:] + jnp.exp(sink - m))
        return [_unstack_heads(out[i * 4 * qb:(i + 1) * 4 * qb], qb, low) for i in range(len(blocks))]

    def body(it, carry):
        t0 = pl.multiple_of(it * rows, rows)
        q = q_ref[0, pl.ds(t0, rows), :]
        outs = []
        for c in range(WIN_CHAINS):
            outs.extend(chain(it, t0, q, range(c * per_chain, (c + 1) * per_chain)))
        o_ref[0, pl.ds(t0, rows), :] = jnp.concatenate(outs, axis=0).astype(o_ref.dtype)
        return carry

    lax.fori_loop(0, n_iter, body, 0)


def _win_attn(q, k, v, kc, vc, sink):
    b, n, w = q.shape
    full = lambda t: pl.BlockSpec((1,) + t.shape[1:], lambda i: (i, 0, 0))
    return pl.pallas_call(
        _win_attn_kernel,
        out_shape=jax.ShapeDtypeStruct((b, n, w), BF16),
        grid=(b,),
        in_specs=[full(q), full(k), full(v), full(kc), full(vc),
                  pl.BlockSpec(memory_space=pltpu.SMEM)],
        out_specs=full(q),
        compiler_params=_params("parallel"),
        name="window_attn",
    )(q, k, v, kc, vc, sink)


def _gelu_tanh(x):
    c = (2.0 / jnp.pi) ** 0.5
    half_x = 0.5 * x
    return half_x + half_x * jnp.tanh(x * (c + (c * 0.044715) * (x * x)))


def _scan_block(a_scr, u_scr, h_scr, base, col, c_in, row8, reverse):
    order = range(SUBLANES - 1, -1, -1) if reverse else range(SUBLANES)
    a_cum, h_loc = [], []
    for j in order:
        aj = a_scr[col, base + j * SUBLANES:base + (j + 1) * SUBLANES, :]
        uj = u_scr[col, base + j * SUBLANES:base + (j + 1) * SUBLANES, :]
        if a_cum:
            h_loc.append(aj * h_loc[-1] + uj)
            a_cum.append(aj * a_cum[-1])
        else:
            h_loc.append(uj)
            a_cum.append(aj)
    p, q = a_cum[-1], h_loc[-1]
    for s in (1, 2, 4):
        shift = SUBLANES - s if reverse else s
        valid = (row8 < SUBLANES - s) if reverse else (row8 >= s)
        q = jnp.where(valid, p * pltpu.roll(q, shift, 0) + q, q)
        p = jnp.where(valid, p * pltpu.roll(p, shift, 0), p)
    end = q + p * c_in
    if reverse:
        c_grp = jnp.where(row8 < SUBLANES - 1, pltpu.roll(end, SUBLANES - 1, 0), c_in)
        last = end[0:1]
    else:
        c_grp = jnp.where(row8 >= 1, pltpu.roll(end, 1, 0), c_in)
        last = end[SUBLANES - 1:SUBLANES]
    for idx, j in enumerate(order):
        h_scr[col, pl.ds(base + j, SUBLANES, stride=SUBLANES), :] = h_loc[idx] + a_cum[idx] * c_grp
    return jnp.broadcast_to(last, (SUBLANES, LANES))


def _lru_chunk(src_ref, conv_ref, t0, rows, d, carry, reverse, cw_ref, cb_ref, wg_ref, bg_ref,
               quarter_nsp, a_scr, u_scr, h_scr, row8):
    if reverse:
        xc = conv_ref[pl.ds(t0, rows), :]
    else:
        halo = SUBLANES
        length = src_ref.shape[1]
        lo = jnp.maximum(t0 - halo, 0)
        hi = jnp.minimum(t0 + rows, length - halo)
        prev = jnp.where(t0 > 0, src_ref[0, pl.ds(pl.multiple_of(lo, halo), halo), :], 0.0)
        nxt = jnp.where(t0 + rows < length, src_ref[0, pl.ds(pl.multiple_of(hi, halo), halo), :], 0.0)
        win = jnp.concatenate([prev, src_ref[0, pl.ds(t0, rows), :], nxt], axis=0)
        xc = cb_ref[...]
        for j in range(CONV_W):
            off = j - CONV_LEFT
            tap = win if off == 0 else pltpu.roll(win, (-off) % (rows + 2 * halo), 0)
            xc = xc + tap[halo:halo + rows] * cw_ref[j:j + 1, :]
        conv_ref[pl.ds(t0, rows), :] = xc
    xb = xc.astype(BF16)
    half = LRU_WIDTH // 2
    y0 = jnp.dot(xb[:, :half], wg_ref[d, 0], preferred_element_type=F32)
    y1 = jnp.dot(xb[:, half:], wg_ref[d, 1], preferred_element_type=F32)
    ya = jnp.concatenate([y0[:, :half], y1[:, :half]], axis=-1)
    yi = jnp.concatenate([y0[:, half:], y1[:, half:]], axis=-1)
    t_r = jnp.tanh(ya + bg_ref[2 * d:2 * d + 1, :])
    t_i = jnp.tanh(yi + bg_ref[2 * d + 1:2 * d + 2, :])
    t = jnp.tanh(t_r * quarter_nsp + quarter_nsp)
    inv = 1.0 / (1.0 - t)
    a = (1.0 + t) * inv
    neg_t = -t
    root = jnp.where(neg_t > 0.0, neg_t * lax.rsqrt(neg_t), 0.0)
    u = root * inv * ((t_i + 1.0) * xc)
    for col in range(LRU_WIDTH // LANES):
        for i in range(rows // SUBLANES):
            blk, g = divmod(i, SUBLANES)
            dst = pl.ds(blk * SCAN_BLOCK + g, SUBLANES, stride=SUBLANES)
            src = (slice(i * SUBLANES, (i + 1) * SUBLANES), slice(col * LANES, (col + 1) * LANES))
            a_scr[col, dst, :] = a[src]
            u_scr[col, dst, :] = u[src]
    n_blk = rows // SCAN_BLOCK
    blocks = range(n_blk - 1, -1, -1) if reverse else range(n_blk)
    carry = list(carry)
    for blk in blocks:
        for col in range(LRU_WIDTH // LANES):
            carry[col] = _scan_block(a_scr, u_scr, h_scr, blk * SCAN_BLOCK, col, carry[col],
                                     row8, reverse)
    return tuple(carry)


def _lru_kernel(xc_ref, gc_ref, x_ref, g_ref, cw_ref, cb_ref, wg_ref, bg_ref, lam_ref, *rest,
                with_ctx):
    if with_ctx:
        o_ref, oc_ref, xconv, xcconv, a_scr, u_scr, h_scr, hf, hfc = rest
    else:
        o_ref, xconv, xcconv, a_scr, u_scr, h_scr, hf = rest
    n = x_ref.shape[1]
    m = xc_ref.shape[1]

    neg_lam = -lam_ref[...]
    softplus = jnp.maximum(neg_lam, 0.0) + jnp.log1p(jnp.exp(-jnp.abs(neg_lam)))
    quarter_nsp = (-0.25 * LRU_C) * softplus
    row8 = lax.broadcasted_iota(jnp.int32, (SUBLANES, LANES), 0)
    zero_carry = tuple(jnp.zeros((SUBLANES, LANES), F32) for _ in range(LRU_WIDTH // LANES))
    n_chunks = n // LRU_CHUNK

    def chunk(ctx_part, t0, rows, d, carry):
        src_ref, conv_ref = (xc_ref, xcconv) if ctx_part else (x_ref, xconv)
        return _lru_chunk(src_ref, conv_ref, t0, rows, d, carry, d == 1, cw_ref, cb_ref, wg_ref, bg_ref,
                          quarter_nsp[d:d + 1, :], a_scr, u_scr, h_scr, row8)

    def read_h(rows):
        return jnp.concatenate([h_scr[col, 0:rows, :] for col in range(LRU_WIDTH // LANES)], axis=-1)

    carry = chunk(True, 0, m, 0, zero_carry)
    if with_ctx:
        hfc[...] = read_h(m)

    def fwd_body(ci, carry):
        t0 = pl.multiple_of(ci * LRU_CHUNK, LRU_CHUNK)
        carry = chunk(False, t0, LRU_CHUNK, 0, carry)
        hf[pl.ds(t0, LRU_CHUNK), :] = read_h(LRU_CHUNK)
        return carry

    lax.fori_loop(0, n_chunks, fwd_body, carry)

    carry = chunk(True, 0, m, 1, zero_carry)
    if with_ctx:
        oc_ref[0] = ((hfc[...] + read_h(m)) * _gelu_tanh(gc_ref[0])).astype(oc_ref.dtype)

    def bwd_body(ci, carry):
        t0 = pl.multiple_of((n_chunks - 1 - ci) * LRU_CHUNK, LRU_CHUNK)
        carry = chunk(False, t0, LRU_CHUNK, 1, carry)
        hf[pl.ds(t0, LRU_CHUNK), :] = hf[pl.ds(t0, LRU_CHUNK), :] + read_h(LRU_CHUNK)
        return carry

    lax.fori_loop(0, n_chunks, bwd_body, carry)

    def gate_body(ci, _):
        t0 = pl.multiple_of(ci * LRU_CHUNK, LRU_CHUNK)
        gate = _gelu_tanh(g_ref[0, pl.ds(t0, LRU_CHUNK), :])
        o_ref[0, pl.ds(t0, LRU_CHUNK), :] = (hf[pl.ds(t0, LRU_CHUNK), :] * gate).astype(o_ref.dtype)
        return 0

    lax.fori_loop(0, n_chunks, gate_body, 0)


def _lru(xc, gc, x, g, cw, cb, wg, bg, lam, *, with_ctx):
    b, n, w = x.shape
    m = xc.shape[1]
    full = lambda t: pl.BlockSpec((1,) + t.shape[1:], lambda i: (i, 0, 0))
    out_shape = [jax.ShapeDtypeStruct((b, n, w), BF16)]
    out_specs = [pl.BlockSpec((1, n, w), lambda i: (i, 0, 0))]
    scratch = [pltpu.VMEM((n, w), F32), pltpu.VMEM((m, w), F32),
               *[pltpu.VMEM((w // LANES, LRU_CHUNK, LANES), F32) for _ in range(3)],
               pltpu.VMEM((n, w), F32)]
    if with_ctx:
        out_shape.append(jax.ShapeDtypeStruct((b, m, w), BF16))
        out_specs.append(pl.BlockSpec((1, m, w), lambda i: (i, 0, 0)))
        scratch.append(pltpu.VMEM((m, w), F32))
    outs = pl.pallas_call(
        functools.partial(_lru_kernel, with_ctx=with_ctx),
        out_shape=out_shape,
        grid=(b,),
        in_specs=[full(xc), full(gc), full(x), full(g), _const_spec(cw.shape), _const_spec(cb.shape),
                  _const_spec(wg.shape), _const_spec(bg.shape), _const_spec(lam.shape)],
        out_specs=out_specs,
        scratch_shapes=scratch,
        compiler_params=_params("parallel"),
        name="rglru_ctx" if with_ctx else "rglru",
    )(xc, gc, x, g, cw, cb, wg, bg, lam)
    return (outs[0], outs[1]) if with_ctx else (outs[0], None)


def _post_kernel(fa_ref, fb_ref, fc_ref, x_ref, mod_ref, wo_ref, gpost_ref, gpre_ref, w1_ref, w2_ref,
                 gpm_ref, o_ref):
    mod = mod_ref[0]
    tm = x_ref.shape[1]
    sub = min(POST_SUB_TILE, tm)
    d_ff = w1_ref.shape[1]
    ff_chunk = d_ff // 4
    pre = []
    for r0 in range(0, tm, sub):
        rows = slice(r0, r0 + sub)
        y = (jnp.dot(fa_ref[0, rows, :], wo_ref[0:A_Q, :], preferred_element_type=F32)
             + jnp.dot(fb_ref[0, rows, :], wo_ref[A_Q:A_Q + B_Q, :], preferred_element_type=F32)
             + jnp.dot(fc_ref[0, rows, :], wo_ref[A_Q + B_Q:, :], preferred_element_type=F32))
        x1 = x_ref[0, rows, :] + mod[2:3] * _rms(y, gpost_ref[...])
        h2 = (_rms(x1, gpre_ref[...]) * (1.0 + mod[4:5]) + mod[3:4]).astype(BF16)
        pre.append((rows, x1, h2))
    for rows, x1, h2 in pre:
        acc = None
        for k in range(d_ff // ff_chunk):
            hk = jnp.dot(h2, w1_ref[:, k * ff_chunk:(k + 1) * ff_chunk], preferred_element_type=F32)
            hk = jnp.square(jnp.maximum(hk, 0.0)).astype(BF16)
            part = jnp.dot(hk, w2_ref[k * ff_chunk:(k + 1) * ff_chunk, :], preferred_element_type=F32)
            acc = part if acc is None else acc + part
        o_ref[0, rows, :] = x1 + mod[5:6] * _rms(acc, gpm_ref[...])


def _post(fa, fb, fc, x, mod, wo, gpost, gpre, w1, w2, gpm, name):
    b, n, d = x.shape
    tm = min(ROW_TILE, n)
    row_spec = lambda wd: pl.BlockSpec((1, tm, wd), lambda i, j: (i, j, 0))
    return pl.pallas_call(
        _post_kernel,
        out_shape=jax.ShapeDtypeStruct((b, n, d), F32),
        grid=(b, n // tm),
        in_specs=[row_spec(fa.shape[2]), row_spec(fb.shape[2]), row_spec(fc.shape[2]), row_spec(d),
                  pl.BlockSpec((1, 6, d), lambda i, j: (i, 0, 0)),
                  _const_spec(wo.shape), _const_spec(gpost.shape), _const_spec(gpre.shape),
                  _const_spec(w1.shape), _const_spec(w2.shape), _const_spec(gpm.shape)],
        out_specs=row_spec(d),
        compiler_params=_params("parallel", "parallel"),
        name=name,
    )(fa, fb, fc, x, mod, wo, gpost, gpre, w1, w2, gpm)


def _rope_tables(n):
    rows = n // GRID_W
    row = jnp.repeat(jnp.arange(rows, dtype=F32), GRID_W)
    col = jnp.tile(jnp.arange(GRID_W, dtype=F32), rows)
    half = HEAD_DIM // 2
    inv_freq = ROPE_BASE ** (-jnp.arange(0, half, 2, dtype=F32) / half)
    ang_r = row[:, None] * inv_freq
    ang_c = col[:, None] * inv_freq
    cr, sr, cc, sc = jnp.cos(ang_r), jnp.sin(ang_r), jnp.cos(ang_c), jnp.sin(ang_c)
    cos = jnp.concatenate([cr, cr, cc, cc], axis=-1)
    sin = jnp.concatenate([-sr, sr, -sc, sc], axis=-1)
    reps = LANES // HEAD_DIM
    return jnp.tile(cos, (1, reps)), jnp.tile(sin, (1, reps))


def _permute_heads(w, axis, bases):
    pieces, pos = [], 0
    for base in bases:
        pieces.append(lax.slice_in_dim(w, pos, base, axis=axis))
        for h in (0, 2, 1, 3):
            pieces.append(lax.slice_in_dim(w, base + h * HEAD_DIM, base + (h + 1) * HEAD_DIM, axis=axis))
        pos = base + 4 * HEAD_DIM
    pieces.append(lax.slice_in_dim(w, pos, w.shape[axis], axis=axis))
    return jnp.concatenate([p for p in pieces if p.shape[axis]], axis=axis)


def _block_diag(w):
    nb, hb, _ = w.shape
    per = nb // 2
    eye = jnp.eye(per, dtype=w.dtype)
    blocks = w.reshape(2, per, hb, 1, hb) * eye[None, :, None, :, None]
    return blocks.reshape(2, per * hb, per * hb)


def kernel(x, c, ctx, c_ctx, w_mod, b_mod, g_pre_mix, g_post_mix, g_pre_mlp, g_post_mlp, w_in, g_q_a,
           g_k_a, sink_b, conv_w, conv_b, lru_w_a, lru_b_a, lru_w_i, lru_b_i, lru_lambda, w_out,
           w_mlp_in, w_mlp_out):
    b, n, d = x.shape
    depth = w_mod.shape[0]
    cos, sin = _rope_tables(n)
    seg = jnp.arange(A_Q) // HEAD_DIM
    bd = jnp.where(seg[:, None] == seg[None, :], 1.0 / HEAD_DIM, 0.0).astype(BF16)

    mod_rows = -(-(b + 1) // SUBLANES) * SUBLANES
    cc = jnp.zeros((mod_rows, d), F32).at[:b].set(c).at[b].set(c_ctx)

    mod_all = _modulation(cc, w_mod, b_mod[:, None, :])
    m = ctx.shape[1]
    flat = lambda t: t.reshape(1, b * m, t.shape[-1])
    unflat = lambda t: t.reshape(b, m, t.shape[-1])

    for l in range(depth):
        last = l == depth - 1
        mod_lat = mod_all[l, :b].reshape(b, 6, d)
        mod_ctx = mod_all[l, b].reshape(1, 6, d)

        w_in_l = _permute_heads(w_in[l], 1, (0, A_Q + 2 * A_KV)).astype(BF16)
        gq = jnp.tile(g_q_a[l], A_Q_HEADS)[None, :]
        gk = jnp.tile(g_k_a[l], A_KV_HEADS)[None, :]
        gpre = g_pre_mix[l][None, :]
        qa, ka, va, qb, kb, vb, xr, gr = _inproj(x, mod_lat, gpre, w_in_l, gq, gk, bd, cos, sin, rope=True)
        qa_c, ka_c, va_c, qb_c, kb_c, vb_c, xr_c, gr_c = map(unflat, _inproj(
            flat(ctx), mod_ctx, gpre, w_in_l, gq, gk, bd, cos, sin, rope=False))

        feat_a = _dense_attn(qa, [(ka_c, va_c), (ka, va)], None, "global_attn")
        feat_b = _win_attn(qb, kb, vb, kb_c, vb_c, sink_b[l])

        wg = (0.5 * jnp.stack([jnp.concatenate([_block_diag(lru_w_a[l, dd]), _block_diag(lru_w_i[l, dd])],
                                               axis=-1) for dd in range(2)])).astype(BF16)
        bg = 0.5 * jnp.stack([lru_b_a[l, 0], lru_b_i[l, 0], lru_b_a[l, 1], lru_b_i[l, 1]])
        feat_c, feat_cc = _lru(xr_c, gr_c, xr, gr, conv_w[l], conv_b[l][None, :], wg, bg, lru_lambda[l],
                               with_ctx=not last)

        wo = _permute_heads(w_out[l], 0, (0, A_Q)).astype(BF16)
        post_w = (wo, g_post_mix[l][None, :], g_pre_mlp[l][None, :], w_mlp_in[l].astype(BF16),
                  w_mlp_out[l].astype(BF16), g_post_mlp[l][None, :])
        x = _post(feat_a, feat_b, feat_c, x, mod_lat, *post_w, name="post_mlp")
        if not last:
            feat_ac = _dense_attn(qa_c, [(ka_c, va_c)], None, "ctx_attn_a")
            feat_bc = _dense_attn(qb_c, [(kb_c, vb_c)], sink_b[l], "ctx_attn_b")
            ctx = unflat(_post(flat(feat_ac), flat(feat_bc), flat(feat_cc), flat(ctx), mod_ctx, *post_w,
                               name="post_mlp_ctx"))
    return x
```

```python
import functools

import jax
import jax.numpy as jnp
from jax import lax
from jax.experimental import pallas as pl
from jax.experimental.pallas import tpu as pltpu

F32 = jnp.float32
BF16 = jnp.bfloat16

GRID_W = 64
HEAD_DIM = 64
A_Q_HEADS = 4
A_KV_HEADS = 2
B_Q_HEADS = 4
B_KV_HEADS = 2
WINDOW = 128
LRU_WIDTH = 512
LRU_BLOCKS = 8
CONV_W = 4
CONV_LEFT = CONV_W // 2
LRU_C = 8.0
ROPE_BASE = 10000.0
EPS = 1e-6
NEG_INF = -1e30

A_Q = A_Q_HEADS * HEAD_DIM
A_KV = A_KV_HEADS * HEAD_DIM
B_Q = B_Q_HEADS * HEAD_DIM
B_KV = B_KV_HEADS * HEAD_DIM
Q_SCALE = HEAD_DIM ** -0.5

LANES = 128
SUBLANES = 8
VMEM_LIMIT = 56 * 1024 * 1024

ROW_TILE = 1024
INPROJ_TILE = 1024
SUB_TILE = 256
POST_SUB_TILE = 512
ATTN_Q_TILE = 1024
CTX_ATTN_BATCH = 4
ATTN_CHAIN = 512
WIN_GROUP = 16
WIN_CHAINS = 4
LRU_CHUNK = 1024
SCAN_BLOCK = SUBLANES * SUBLANES

_NT = (((1,), (1,)), ((), ()))


def _const_spec(shape):
    return pl.BlockSpec(shape, lambda *_: (0,) * len(shape), pipeline_mode=pl.Buffered(1))


def _params(*sem):
    return pltpu.CompilerParams(dimension_semantics=sem, vmem_limit_bytes=VMEM_LIMIT)


def _rms(x, g):
    return x * lax.rsqrt(jnp.mean(x * x, axis=-1, keepdims=True) + EPS) * g


def _mod_kernel(c_ref, w_ref, b_ref, o_ref):
    c = c_ref[...]
    act = (c * jax.nn.sigmoid(c)).astype(BF16)
    o_ref[...] = jnp.dot(act, w_ref[...].astype(BF16), preferred_element_type=F32) + b_ref[...]


def _modulation(cc, w, b):
    rows, d = cc.shape
    depth, _, n_out = w.shape
    tn = n_out // 4
    return pl.pallas_call(
        _mod_kernel,
        out_shape=jax.ShapeDtypeStruct((depth, rows, n_out), F32),
        grid=(depth, n_out // tn),
        in_specs=[
            pl.BlockSpec((rows, d), lambda l, j: (0, 0)),
            pl.BlockSpec((None, d, tn), lambda l, j: (l, 0, j)),
            pl.BlockSpec((None, 1, tn), lambda l, j: (l, 0, j)),
        ],
        out_specs=pl.BlockSpec((None, rows, tn), lambda l, j: (l, 0, j)),
        compiler_params=_params("arbitrary", "arbitrary"),
        name="modulation",
    )(cc, w, b)


def _head_rms(t, g, bd):
    ms = jnp.dot((t * t).astype(BF16), bd, preferred_element_type=F32)
    return t * lax.rsqrt(ms + EPS) * g


def _rope(t, cos, sin, low_half):
    outs = []
    for j in range(t.shape[1] // LANES):
        tc = t[:, j * LANES:(j + 1) * LANES]
        partner = jnp.where(low_half, pltpu.roll(tc, LANES - 16, 1), pltpu.roll(tc, 16, 1))
        outs.append(tc * cos + partner * sin)
    return outs[0] if len(outs) == 1 else jnp.concatenate(outs, axis=-1)


def _inproj_kernel(x_ref, mod_ref, gpre_ref, w_ref, gq_ref, gk_ref, bd_ref, cos_ref, sin_ref,
                   qa_ref, ka_ref, va_ref, qb_ref, kb_ref, vb_ref, xr_ref, gr_ref, *, rope):
    mod = mod_ref[0]
    bd = bd_ref[...]
    tm = x_ref.shape[1]
    sub = min(SUB_TILE, tm)
    pre = []
    for r0 in range(0, tm, sub):
        rows = slice(r0, r0 + sub)
        h = _rms(x_ref[0, rows, :], gpre_ref[...]) * (1.0 + mod[1:2]) + mod[0:1]
        pre.append((rows, h.astype(BF16)))
    for rows, hb in pre:
        z = jnp.dot(hb, w_ref[...], preferred_element_type=F32)
        qa = _head_rms(z[:, 0:A_Q], gq_ref[...], bd)
        ka = _head_rms(z[:, A_Q:A_Q + A_KV], gk_ref[...], bd[:A_KV, :A_KV])
        o = A_Q + 2 * A_KV
        qb = z[:, o:o + B_Q]
        kb = z[:, o + B_Q:o + B_Q + B_KV]
        if rope:
            cos = cos_ref[rows, :]
            sin = sin_ref[rows, :]
            lane = lax.broadcasted_iota(jnp.int32, cos.shape, 1)
            low_half = (lane % 32) < 16
            qa = _rope(qa, cos, sin, low_half)
            ka = _rope(ka, cos, sin, low_half)
            qb = _rope(qb, cos, sin, low_half)
            kb = _rope(kb, cos, sin, low_half)
        qa_ref[0, rows, :] = (qa * Q_SCALE).astype(BF16)
        ka_ref[0, rows, :] = ka.astype(BF16)
        va_ref[0, rows, :] = z[:, A_Q + A_KV:o].astype(BF16)
        qb_ref[0, rows, :] = (qb * Q_SCALE).astype(BF16)
        kb_ref[0, rows, :] = kb.astype(BF16)
        vb_ref[0, rows, :] = z[:, o + B_Q + B_KV:o + B_Q + 2 * B_KV].astype(BF16)
        o2 = o + B_Q + 2 * B_KV
        xr_ref[0, rows, :] = z[:, o2:o2 + LRU_WIDTH]
        gr_ref[0, rows, :] = z[:, o2 + LRU_WIDTH:o2 + 2 * LRU_WIDTH]


def _inproj(x, mod, gpre, w, gq, gk, bd, cos, sin, *, rope):
    b, n, d = x.shape
    tm = min(INPROJ_TILE, n)
    nt = n // tm
    widths = (A_Q, A_KV, A_KV, B_Q, B_KV, B_KV, LRU_WIDTH, LRU_WIDTH)
    dtypes = (BF16,) * 6 + (F32, F32)
    row_spec = lambda wd: pl.BlockSpec((1, tm, wd), lambda i, j: (i, j, 0))
    table_spec = pl.BlockSpec((tm, LANES), (lambda i, j: (j, 0)) if rope else (lambda i, j: (0, 0)))
    return pl.pallas_call(
        functools.partial(_inproj_kernel, rope=rope),
        out_shape=[jax.ShapeDtypeStruct((b, n, wd), dt) for wd, dt in zip(widths, dtypes)],
        grid=(b, nt),
        in_specs=[
            row_spec(d),
            pl.BlockSpec((1, 6, d), lambda i, j: (i, 0, 0)),
            _const_spec(gpre.shape),
            _const_spec(w.shape),
            _const_spec(gq.shape),
            _const_spec(gk.shape),
            _const_spec(bd.shape),
            table_spec,
            table_spec,
        ],
        out_specs=[row_spec(wd) for wd in widths],
        compiler_params=_params("parallel", "parallel"),
        name="inproj_rope" if rope else "inproj_ctx",
    )(x, mod, gpre, w, gq, gk, bd, cos, sin)


def _split_heads(qc, low):
    zero = jnp.zeros_like(qc)
    return jnp.where(low, qc, zero), jnp.where(low, zero, qc)


def _stack_heads(q, low):
    parts = []
    for j in range(q.shape[1] // LANES):
        parts.extend(_split_heads(q[:, j * LANES:(j + 1) * LANES], low))
    return jnp.concatenate(parts, axis=0)


def _unstack_heads(o, t, low):
    return jnp.concatenate([jnp.where(low, o[0:t], o[t:2 * t]),
                            jnp.where(low, o[2 * t:3 * t], o[3 * t:4 * t])], axis=-1)


def _with_ones(v):
    return jnp.concatenate([v, jnp.ones_like(v)], axis=-1)


def _chain_sizes(total):
    assert total % ATTN_CHAIN == 0
    half = ATTN_CHAIN // 2
    return (half,) + (ATTN_CHAIN,) * (total // ATTN_CHAIN - 1) + (half,)


def _attend(q, keys, vals, sink_ref):
    tq = q.shape[0]
    has_sink = sink_ref is not None
    low = lax.broadcasted_iota(jnp.int32, (tq, LANES), 1) < HEAD_DIM
    qs_all = _stack_heads(q, low)
    if has_sink:
        sink_all = jnp.concatenate([jnp.full((tq, LANES), sink_ref[h], F32) for h in (0, 2, 1, 3)], axis=0)
    outs, r0 = [], 0
    for rows in _chain_sizes(4 * tq):
        qs = qs_all[r0:r0 + rows]
        scores = [lax.dot_general(qs, k, _NT, preferred_element_type=F32) for k in keys]
        s = scores[0] if len(keys) == 1 else jnp.concatenate(scores, axis=-1)
        m = jnp.max(s, axis=-1, keepdims=True)
        if has_sink:
            sink = sink_all[r0:r0 + rows]
            m_wide = jnp.maximum(jnp.broadcast_to(m, sink.shape), sink)
            p = jnp.exp(s - jnp.concatenate([m_wide] * (s.shape[1] // LANES), axis=-1)).astype(BF16)
        else:
            p = jnp.exp(s - m).astype(BF16)
        acc, c0 = None, 0
        for k, v in zip(keys, vals):
            pv = jnp.dot(p[:, c0:c0 + k.shape[0]], v, preferred_element_type=F32)
            acc = pv if acc is None else acc + pv
            c0 += k.shape[0]
        denom = acc[:, LANES:]
        if has_sink:
            denom = denom + jnp.exp(sink - m_wide)
        outs.append(acc[:, :LANES] / denom)
        r0 += rows
    return _unstack_heads(jnp.concatenate(outs, axis=0), tq, low)


def _dense_attn_kernel(*refs, n_kv):
    q_ref, kv_refs, o_ref = refs[0], refs[1:1 + 2 * n_kv], refs[-1]
    keys = [kv_refs[2 * i][0] for i in range(n_kv)]
    vals = [_with_ones(kv_refs[2 * i + 1][0]) for i in range(n_kv)]
    o_ref[0] = _attend(q_ref[0], keys, vals, None).astype(o_ref.dtype)


def _ctx_attn_kernel(qa_ref, ka_ref, va_ref, qb_ref, kb_ref, vb_ref, sink_ref, oa_ref, ob_ref):
    for i in range(qa_ref.shape[0]):
        oa_ref[i] = _attend(qa_ref[i], [ka_ref[i]], [_with_ones(va_ref[i])], None).astype(oa_ref.dtype)
        ob_ref[i] = _attend(qb_ref[i], [kb_ref[i]], [_with_ones(vb_ref[i])], sink_ref).astype(ob_ref.dtype)


def _dense_attn(q, kvs, name):
    b, n, w = q.shape
    tq = min(ATTN_Q_TILE, n)
    in_specs = [pl.BlockSpec((1, tq, w), lambda i, j: (i, j, 0))]
    args = [q]
    for k, v in kvs:
        for t in (k, v):
            in_specs.append(pl.BlockSpec((1,) + t.shape[1:], lambda i, j: (i, 0, 0)))
            args.append(t)
    return pl.pallas_call(
        functools.partial(_dense_attn_kernel, n_kv=len(kvs)),
        out_shape=jax.ShapeDtypeStruct((b, n, w), BF16),
        grid=(b, n // tq),
        in_specs=in_specs,
        out_specs=pl.BlockSpec((1, tq, w), lambda i, j: (i, j, 0)),
        compiler_params=_params("parallel", "parallel"),
        name=name,
    )(*args)


def _ctx_attn(qa, ka, va, qb, kb, vb, sink):
    b = qa.shape[0]
    bt = CTX_ATTN_BATCH if b % CTX_ATTN_BATCH == 0 else 1
    spec = lambda t: pl.BlockSpec((bt,) + t.shape[1:], lambda i: (i, 0, 0))
    args = (qa, ka, va, qb, kb, vb)
    return pl.pallas_call(
        _ctx_attn_kernel,
        out_shape=[jax.ShapeDtypeStruct(qa.shape, BF16), jax.ShapeDtypeStruct(qb.shape, BF16)],
        grid=(b // bt,),
        in_specs=[spec(t) for t in args] + [pl.BlockSpec(memory_space=pltpu.SMEM)],
        out_specs=[spec(qa), spec(qb)],
        compiler_params=_params("parallel"),
        name="ctx_attn",
    )(*args, sink)


def _win_attn_kernel(q_ref, k_ref, v_ref, kc_ref, vc_ref, sink_ref, o_ref):
    n = q_ref.shape[1]
    qb = WINDOW
    band = 3 * qb
    rows = WIN_GROUP * qb
    n_iter = n // rows
    low = lax.broadcasted_iota(jnp.int32, (qb, LANES), 1) < HEAD_DIM
    row_minus_col = (lax.broadcasted_iota(jnp.int32, (qb, band), 0)
                     - lax.broadcasted_iota(jnp.int32, (qb, band), 1))

    def band_bias(first_key_offset):
        bias = jnp.where(jnp.abs(row_minus_col + first_key_offset) <= WINDOW, 0.0, NEG_INF)
        return jnp.concatenate([bias.astype(F32)] * 4, axis=0)

    bias_first, bias_mid, bias_last = band_bias(0), band_bias(qb), band_bias(2 * qb)
    kc = kc_ref[0]
    vc = _with_ones(vc_ref[0])
    ones_band = jnp.ones((band, LANES), BF16)
    sink_blk = jnp.concatenate([jnp.full((qb, LANES), sink_ref[h], F32) for h in (0, 2, 1, 3)], axis=0)
    per_chain = WIN_GROUP // WIN_CHAINS
    sink = jnp.concatenate([sink_blk] * per_chain, axis=0)

    def chain(it, t0, q, blocks):
        qs_blk = [_stack_heads(q[g * qb:(g + 1) * qb], low) for g in blocks]
        starts = [pl.multiple_of(jnp.clip(t0 + (g - 1) * qb, 0, n - band), qb) for g in blocks]
        s_ctx = lax.dot_general(jnp.concatenate(qs_blk, axis=0), kc, _NT, preferred_element_type=F32)
        s_band = []
        for i, g in enumerate(blocks):
            s = lax.dot_general(qs_blk[i], k_ref[0, pl.ds(starts[i], band), :], _NT,
                                preferred_element_type=F32)
            bias = bias_mid
            if g == 0:
                bias = jnp.where(it == 0, bias_first, bias)
            if g == WIN_GROUP - 1:
                bias = jnp.where(it == n_iter - 1, bias_last, bias)
            s_band.append(s + bias)
        s = jnp.concatenate([jnp.concatenate(s_band, axis=0), s_ctx], axis=-1)
        m = jnp.maximum(jnp.broadcast_to(jnp.max(s, axis=-1, keepdims=True), sink.shape), sink)
        p = jnp.exp(s - jnp.concatenate([m] * (s.shape[1] // LANES), axis=-1)).astype(BF16)
        acc_band = [jnp.dot(p[i * 4 * qb:(i + 1) * 4 * qb, :band],
                            jnp.concatenate([v_ref[0, pl.ds(starts[i], band), :], ones_band], axis=-1),
                            preferred_element_type=F32) for i in range(len(blocks))]
        acc = jnp.concatenate(acc_band, axis=0) + jnp.dot(p[:, band:], vc, preferred_element_type=F32)
        out = acc[:, :LANES] / (acc[:, LANES:] + jnp.exp(sink - m))
        return [_unstack_heads(out[i * 4 * qb:(i + 1) * 4 * qb], qb, low) for i in range(len(blocks))]

    def body(it, carry):
        t0 = pl.multiple_of(it * rows, rows)
        q = q_ref[0, pl.ds(t0, rows), :]
        outs = []
        for c in range(WIN_CHAINS):
            outs.extend(chain(it, t0, q, range(c * per_chain, (c + 1) * per_chain)))
        o_ref[0, pl.ds(t0, rows), :] = jnp.concatenate(outs, axis=0).astype(o_ref.dtype)
        return carry

    lax.fori_loop(0, n_iter, body, 0)


def _win_attn(q, k, v, kc, vc, sink):
    b, n, w = q.shape
    full = lambda t: pl.BlockSpec((1,) + t.shape[1:], lambda i: (i, 0, 0))
    return pl.pallas_call(
        _win_attn_kernel,
        out_shape=jax.ShapeDtypeStruct((b, n, w), BF16),
        grid=(b,),
        in_specs=[full(q), full(k), full(v), full(kc), full(vc),
                  pl.BlockSpec(memory_space=pltpu.SMEM)],
        out_specs=full(q),
        compiler_params=_params("parallel"),
        name="window_attn",
    )(q, k, v, kc, vc, sink)


def _gelu_tanh(x):
    c = (2.0 / jnp.pi) ** 0.5
    half_x = 0.5 * x
    return half_x + half_x * jnp.tanh(x * (c + (c * 0.044715) * (x * x)))


def _scan_block(a_scr, u_scr, h_scr, base, col, c_in, row8, reverse):
    order = range(SUBLANES - 1, -1, -1) if reverse else range(SUBLANES)
    a_cum, h_loc = [], []
    for j in order:
        aj = a_scr[col, base + j * SUBLANES:base + (j + 1) * SUBLANES, :]
        uj = u_scr[col, base + j * SUBLANES:base + (j + 1) * SUBLANES, :]
        if a_cum:
            h_loc.append(aj * h_loc[-1] + uj)
            a_cum.append(aj * a_cum[-1])
        else:
            h_loc.append(uj)
            a_cum.append(aj)
    p, q = a_cum[-1], h_loc[-1]
    for s in (1, 2, 4):
        shift = SUBLANES - s if reverse else s
        valid = (row8 < SUBLANES - s) if reverse else (row8 >= s)
        q = jnp.where(valid, p * pltpu.roll(q, shift, 0) + q, q)
        p = jnp.where(valid, p * pltpu.roll(p, shift, 0), p)
    end = q + p * c_in
    if reverse:
        c_grp = jnp.where(row8 < SUBLANES - 1, pltpu.roll(end, SUBLANES - 1, 0), c_in)
        last = end[0:1]
    else:
        c_grp = jnp.where(row8 >= 1, pltpu.roll(end, 1, 0), c_in)
        last = end[SUBLANES - 1:SUBLANES]
    for idx, j in enumerate(order):
        h_scr[col, pl.ds(base + j, SUBLANES, stride=SUBLANES), :] = h_loc[idx] + a_cum[idx] * c_grp
    return jnp.broadcast_to(last, (SUBLANES, LANES))


def _lru_chunk(src_ref, conv_ref, t0, rows, d, carry, reverse, cw_ref, cb_ref, wg_ref, bg_ref,
               quarter_nsp, a_scr, u_scr, h_scr, row8):
    if reverse:
        xc = conv_ref[pl.ds(t0, rows), :]
    else:
        halo = SUBLANES
        length = src_ref.shape[1]
        lo = jnp.maximum(t0 - halo, 0)
        hi = jnp.minimum(t0 + rows, length - halo)
        prev = jnp.where(t0 > 0, src_ref[0, pl.ds(pl.multiple_of(lo, halo), halo), :], 0.0)
        nxt = jnp.where(t0 + rows < length, src_ref[0, pl.ds(pl.multiple_of(hi, halo), halo), :], 0.0)
        win = jnp.concatenate([prev, src_ref[0, pl.ds(t0, rows), :], nxt], axis=0)
        xc = cb_ref[...]
        for j in range(CONV_W):
            off = j - CONV_LEFT
            tap = win if off == 0 else pltpu.roll(win, (-off) % (rows + 2 * halo), 0)
            xc = xc + tap[halo:halo + rows] * cw_ref[j:j + 1, :]
        conv_ref[pl.ds(t0, rows), :] = xc
    xb = xc.astype(BF16)
    half = LRU_WIDTH // 2
    y0 = jnp.dot(xb[:, :half], wg_ref[d, 0], preferred_element_type=F32)
    y1 = jnp.dot(xb[:, half:], wg_ref[d, 1], preferred_element_type=F32)
    ya = jnp.concatenate([y0[:, :half], y1[:, :half]], axis=-1)
    yi = jnp.concatenate([y0[:, half:], y1[:, half:]], axis=-1)
    t_r = jnp.tanh(ya + bg_ref[2 * d:2 * d + 1, :])
    t_i = jnp.tanh(yi + bg_ref[2 * d + 1:2 * d + 2, :])
    t = jnp.tanh(t_r * quarter_nsp + quarter_nsp)
    inv = 1.0 / (1.0 - t)
    a = (1.0 + t) * inv
    neg_t = -t
    root = jnp.where(neg_t > 0.0, neg_t * lax.rsqrt(neg_t), 0.0)
    u = root * inv * ((t_i + 1.0) * xc)
    for col in range(LRU_WIDTH // LANES):
        for i in range(rows // SUBLANES):
            blk, g = divmod(i, SUBLANES)
            dst = pl.ds(blk * SCAN_BLOCK + g, SUBLANES, stride=SUBLANES)
            src = (slice(i * SUBLANES, (i + 1) * SUBLANES), slice(col * LANES, (col + 1) * LANES))
            a_scr[col, dst, :] = a[src]
            u_scr[col, dst, :] = u[src]
    n_blk = rows // SCAN_BLOCK
    blocks = range(n_blk - 1, -1, -1) if reverse else range(n_blk)
    carry = list(carry)
    for blk in blocks:
        for col in range(LRU_WIDTH // LANES):
            carry[col] = _scan_block(a_scr, u_scr, h_scr, blk * SCAN_BLOCK, col, carry[col],
                                     row8, reverse)
    return tuple(carry)


def _lru_kernel(xc_ref, gc_ref, x_ref, g_ref, cw_ref, cb_ref, wg_ref, bg_ref, lam_ref, *rest,
                with_ctx):
    if with_ctx:
        o_ref, oc_ref, xconv, xcconv, a_scr, u_scr, h_scr, hf, hfc = rest
    else:
        o_ref, xconv, xcconv, a_scr, u_scr, h_scr, hf = rest
    n = x_ref.shape[1]
    m = xc_ref.shape[1]

    neg_lam = -lam_ref[...]
    softplus = jnp.maximum(neg_lam, 0.0) + jnp.log1p(jnp.exp(-jnp.abs(neg_lam)))
    quarter_nsp = (-0.25 * LRU_C) * softplus
    row8 = lax.broadcasted_iota(jnp.int32, (SUBLANES, LANES), 0)
    zero_carry = tuple(jnp.zeros((SUBLANES, LANES), F32) for _ in range(LRU_WIDTH // LANES))
    n_chunks = n // LRU_CHUNK

    def chunk(ctx_part, t0, rows, d, carry):
        src_ref, conv_ref = (xc_ref, xcconv) if ctx_part else (x_ref, xconv)
        return _lru_chunk(src_ref, conv_ref, t0, rows, d, carry, d == 1, cw_ref, cb_ref, wg_ref, bg_ref,
                          quarter_nsp[d:d + 1, :], a_scr, u_scr, h_scr, row8)

    def read_h(rows):
        return jnp.concatenate([h_scr[col, 0:rows, :] for col in range(LRU_WIDTH // LANES)], axis=-1)

    carry = chunk(True, 0, m, 0, zero_carry)
    if with_ctx:
        hfc[...] = read_h(m)

    def fwd_body(ci, carry):
        t0 = pl.multiple_of(ci * LRU_CHUNK, LRU_CHUNK)
        carry = chunk(False, t0, LRU_CHUNK, 0, carry)
        hf[pl.ds(t0, LRU_CHUNK), :] = read_h(LRU_CHUNK)
        return carry

    lax.fori_loop(0, n_chunks, fwd_body, carry)

    carry = chunk(True, 0, m, 1, zero_carry)
    if with_ctx:
        oc_ref[0] = ((hfc[...] + read_h(m)) * _gelu_tanh(gc_ref[0])).astype(oc_ref.dtype)

    def bwd_body(ci, carry):
        t0 = pl.multiple_of((n_chunks - 1 - ci) * LRU_CHUNK, LRU_CHUNK)
        carry = chunk(False, t0, LRU_CHUNK, 1, carry)
        hf[pl.ds(t0, LRU_CHUNK), :] = hf[pl.ds(t0, LRU_CHUNK), :] + read_h(LRU_CHUNK)
        return carry

    lax.fori_loop(0, n_chunks, bwd_body, carry)

    def gate_body(ci, _):
        t0 = pl.multiple_of(ci * LRU_CHUNK, LRU_CHUNK)
        gate = _gelu_tanh(g_ref[0, pl.ds(t0, LRU_CHUNK), :])
        o_ref[0, pl.ds(t0, LRU_CHUNK), :] = (hf[pl.ds(t0, LRU_CHUNK), :] * gate).astype(o_ref.dtype)
        return 0

    lax.fori_loop(0, n_chunks, gate_body, 0)


def _lru(xc, gc, x, g, cw, cb, wg, bg, lam, *, with_ctx):
    b, n, w = x.shape
    m = xc.shape[1]
    full = lambda t: pl.BlockSpec((1,) + t.shape[1:], lambda i: (i, 0, 0))
    out_shape = [jax.ShapeDtypeStruct((b, n, w), BF16)]
    out_specs = [pl.BlockSpec((1, n, w), lambda i: (i, 0, 0))]
    scratch = [pltpu.VMEM((n, w), F32), pltpu.VMEM((m, w), F32),
               *[pltpu.VMEM((w // LANES, LRU_CHUNK, LANES), F32) for _ in range(3)],
               pltpu.VMEM((n, w), F32)]
    if with_ctx:
        out_shape.append(jax.ShapeDtypeStruct((b, m, w), BF16))
        out_specs.append(pl.BlockSpec((1, m, w), lambda i: (i, 0, 0)))
        scratch.append(pltpu.VMEM((m, w), F32))
    outs = pl.pallas_call(
        functools.partial(_lru_kernel, with_ctx=with_ctx),
        out_shape=out_shape,
        grid=(b,),
        in_specs=[full(xc), full(gc), full(x), full(g), _const_spec(cw.shape), _const_spec(cb.shape),
                  _const_spec(wg.shape), _const_spec(bg.shape), _const_spec(lam.shape)],
        out_specs=out_specs,
        scratch_shapes=scratch,
        compiler_params=_params("parallel"),
        name="rglru_ctx" if with_ctx else "rglru",
    )(xc, gc, x, g, cw, cb, wg, bg, lam)
    return (outs[0], outs[1]) if with_ctx else (outs[0], None)


def _post_kernel(fa_ref, fb_ref, fc_ref, x_ref, mod_ref, wo_ref, gpost_ref, gpre_ref, w1_ref, w2_ref,
                 gpm_ref, o_ref):
    mod = mod_ref[0]
    tm = x_ref.shape[1]
    sub = min(POST_SUB_TILE, tm)
    d_ff = w1_ref.shape[1]
    ff_chunk = d_ff // 4
    pre = []
    for r0 in range(0, tm, sub):
        rows = slice(r0, r0 + sub)
        y = (jnp.dot(fa_ref[0, rows, :], wo_ref[0:A_Q, :], preferred_element_type=F32)
             + jnp.dot(fb_ref[0, rows, :], wo_ref[A_Q:A_Q + B_Q, :], preferred_element_type=F32)
             + jnp.dot(fc_ref[0, rows, :], wo_ref[A_Q + B_Q:, :], preferred_element_type=F32))
        x1 = x_ref[0, rows, :] + mod[2:3] * _rms(y, gpost_ref[...])
        h2 = (_rms(x1, gpre_ref[...]) * (1.0 + mod[4:5]) + mod[3:4]).astype(BF16)
        pre.append((rows, x1, h2))
    for rows, x1, h2 in pre:
        acc = None
        for k in range(d_ff // ff_chunk):
            hk = jnp.dot(h2, w1_ref[:, k * ff_chunk:(k + 1) * ff_chunk], preferred_element_type=F32)
            hk = jnp.square(jnp.maximum(hk, 0.0)).astype(BF16)
            part = jnp.dot(hk, w2_ref[k * ff_chunk:(k + 1) * ff_chunk, :], preferred_element_type=F32)
            acc = part if acc is None else acc + part
        o_ref[0, rows, :] = x1 + mod[5:6] * _rms(acc, gpm_ref[...])


def _post(fa, fb, fc, x, mod, wo, gpost, gpre, w1, w2, gpm, name):
    b, n, d = x.shape
    tm = min(ROW_TILE, n)
    row_spec = lambda wd: pl.BlockSpec((1, tm, wd), lambda i, j: (i, j, 0))
    return pl.pallas_call(
        _post_kernel,
        out_shape=jax.ShapeDtypeStruct((b, n, d), F32),
        grid=(b, n // tm),
        in_specs=[row_spec(fa.shape[2]), row_spec(fb.shape[2]), row_spec(fc.shape[2]), row_spec(d),
                  pl.BlockSpec((1, 6, d), lambda i, j: (i, 0, 0)),
                  _const_spec(wo.shape), _const_spec(gpost.shape), _const_spec(gpre.shape),
                  _const_spec(w1.shape), _const_spec(w2.shape), _const_spec(gpm.shape)],
        out_specs=row_spec(d),
        compiler_params=_params("parallel", "parallel"),
        name=name,
    )(fa, fb, fc, x, mod, wo, gpost, gpre, w1, w2, gpm)


def _rope_tables(n):
    rows = n // GRID_W
    row = jnp.repeat(jnp.arange(rows, dtype=F32), GRID_W)
    col = jnp.tile(jnp.arange(GRID_W, dtype=F32), rows)
    half = HEAD_DIM // 2
    inv_freq = ROPE_BASE ** (-jnp.arange(0, half, 2, dtype=F32) / half)
    ang_r = row[:, None] * inv_freq
    ang_c = col[:, None] * inv_freq
    cr, sr, cc, sc = jnp.cos(ang_r), jnp.sin(ang_r), jnp.cos(ang_c), jnp.sin(ang_c)
    cos = jnp.concatenate([cr, cr, cc, cc], axis=-1)
    sin = jnp.concatenate([-sr, sr, -sc, sc], axis=-1)
    reps = LANES // HEAD_DIM
    return jnp.tile(cos, (1, reps)), jnp.tile(sin, (1, reps))


def _permute_heads(w, axis, bases):
    pieces, pos = [], 0
    for base in bases:
        pieces.append(lax.slice_in_dim(w, pos, base, axis=axis))
        for h in (0, 2, 1, 3):
            pieces.append(lax.slice_in_dim(w, base + h * HEAD_DIM, base + (h + 1) * HEAD_DIM, axis=axis))
        pos = base + 4 * HEAD_DIM
    pieces.append(lax.slice_in_dim(w, pos, w.shape[axis], axis=axis))
    return jnp.concatenate([p for p in pieces if p.shape[axis]], axis=axis)


def _block_diag(w):
    nb, hb, _ = w.shape
    per = nb // 2
    eye = jnp.eye(per, dtype=w.dtype)
    blocks = w.reshape(2, per, hb, 1, hb) * eye[None, :, None, :, None]
    return blocks.reshape(2, per * hb, per * hb)


def kernel(x, c, ctx, c_ctx, w_mod, b_mod, g_pre_mix, g_post_mix, g_pre_mlp, g_post_mlp, w_in, g_q_a,
           g_k_a, sink_b, conv_w, conv_b, lru_w_a, lru_b_a, lru_w_i, lru_b_i, lru_lambda, w_out,
           w_mlp_in, w_mlp_out):
    b, n, d = x.shape
    depth = w_mod.shape[0]
    cos, sin = _rope_tables(n)
    seg = jnp.arange(A_Q) // HEAD_DIM
    bd = jnp.where(seg[:, None] == seg[None, :], 1.0 / HEAD_DIM, 0.0).astype(BF16)

    mod_rows = -(-(b + 1) // SUBLANES) * SUBLANES
    cc = jnp.zeros((mod_rows, d), F32).at[:b].set(c).at[b].set(c_ctx)

    mod_all = _modulation(cc, w_mod, b_mod[:, None, :])
    m = ctx.shape[1]
    flat = lambda t: t.reshape(1, b * m, t.shape[-1])
    unflat = lambda t: t.reshape(b, m, t.shape[-1])

    for l in range(depth):
        last = l == depth - 1
        mod_lat = mod_all[l, :b].reshape(b, 6, d)
        mod_ctx = mod_all[l, b].reshape(1, 6, d)

        w_in_l = _permute_heads(w_in[l], 1, (0, A_Q + 2 * A_KV)).astype(BF16)
        gq = jnp.tile(g_q_a[l], A_Q_HEADS)[None, :]
        gk = jnp.tile(g_k_a[l], A_KV_HEADS)[None, :]
        gpre = g_pre_mix[l][None, :]
        qa, ka, va, qb, kb, vb, xr, gr = _inproj(x, mod_lat, gpre, w_in_l, gq, gk, bd, cos, sin, rope=True)
        qa_c, ka_c, va_c, qb_c, kb_c, vb_c, xr_c, gr_c = map(unflat, _inproj(
            flat(ctx), mod_ctx, gpre, w_in_l, gq, gk, bd, cos, sin, rope=False))

        feat_a = _dense_attn(qa, [(ka_c, va_c), (ka, va)], "global_attn")
        feat_b = _win_attn(qb, kb, vb, kb_c, vb_c, sink_b[l])

        wg = (0.5 * jnp.stack([jnp.concatenate([_block_diag(lru_w_a[l, dd]), _block_diag(lru_w_i[l, dd])],
                                               axis=-1) for dd in range(2)])).astype(BF16)
        bg = 0.5 * jnp.stack([lru_b_a[l, 0], lru_b_i[l, 0], lru_b_a[l, 1], lru_b_i[l, 1]])
        feat_c, feat_cc = _lru(xr_c, gr_c, xr, gr, conv_w[l], conv_b[l][None, :], wg, bg, lru_lambda[l],
                               with_ctx=not last)

        wo = _permute_heads(w_out[l], 0, (0, A_Q)).astype(BF16)
        post_w = (wo, g_post_mix[l][None, :], g_pre_mlp[l][None, :], w_mlp_in[l].astype(BF16),
                  w_mlp_out[l].astype(BF16), g_post_mlp[l][None, :])
        x = _post(feat_a, feat_b, feat_c, x, mod_lat, *post_w, name="post_mlp")
        if not last:
            feat_ac, feat_bc = _ctx_attn(qa_c, ka_c, va_c, qb_c, kb_c, vb_c, sink_b[l])
            ctx = unflat(_post(flat(feat_ac), flat(feat_bc), flat(feat_cc), flat(ctx), mod_ctx, *post_w,
                               name="post_mlp_ctx"))
    return x
```

```python
import functools

import jax
import jax.numpy as jnp
from jax import lax
from jax.experimental import pallas as pl
from jax.experimental.pallas import tpu as pltpu

F32 = jnp.float32
BF16 = jnp.bfloat16

GRID_W = 64
HEAD_DIM = 64
A_Q_HEADS = 4
A_KV_HEADS = 2
B_Q_HEADS = 4
B_KV_HEADS = 2
WINDOW = 128
LRU_WIDTH = 512
LRU_BLOCKS = 8
CONV_W = 4
CONV_LEFT = CONV_W // 2
LRU_C = 8.0
ROPE_BASE = 10000.0
EPS = 1e-6
NEG_INF = -1e30

A_Q = A_Q_HEADS * HEAD_DIM
A_KV = A_KV_HEADS * HEAD_DIM
B_Q = B_Q_HEADS * HEAD_DIM
B_KV = B_KV_HEADS * HEAD_DIM
Q_SCALE = HEAD_DIM ** -0.5

LANES = 128
SUBLANES = 8
VMEM_LIMIT = 56 * 1024 * 1024

ROW_TILE = 1024
INPROJ_TILE = 1024
SUB_TILE = 256
POST_SUB_TILE = 512
ATTN_Q_TILE = 1024
CTX_ATTN_BATCH = 4
ATTN_CHAIN = 512
WIN_GROUP = 16
WIN_CHAINS = 4
LRU_CHUNK = 1024
SCAN_BLOCK = SUBLANES * SUBLANES

_NT = (((1,), (1,)), ((), ()))


def _const_spec(shape):
    return pl.BlockSpec(shape, lambda *_: (0,) * len(shape), pipeline_mode=pl.Buffered(1))


def _params(*sem):
    return pltpu.CompilerParams(dimension_semantics=sem, vmem_limit_bytes=VMEM_LIMIT)


def _rms(x, g):
    return x * lax.rsqrt(jnp.mean(x * x, axis=-1, keepdims=True) + EPS) * g


def _mod_kernel(c_ref, w_ref, b_ref, o_ref):
    c = c_ref[...]
    act = (c * jax.nn.sigmoid(c)).astype(BF16)
    o_ref[...] = jnp.dot(act, w_ref[...].astype(BF16), preferred_element_type=F32) + b_ref[...]


def _modulation(cc, w, b):
    rows, d = cc.shape
    depth, _, n_out = w.shape
    tn = n_out // 4
    return pl.pallas_call(
        _mod_kernel,
        out_shape=jax.ShapeDtypeStruct((depth, rows, n_out), F32),
        grid=(depth, n_out // tn),
        in_specs=[
            pl.BlockSpec((rows, d), lambda l, j: (0, 0)),
            pl.BlockSpec((None, d, tn), lambda l, j: (l, 0, j)),
            pl.BlockSpec((None, 1, tn), lambda l, j: (l, 0, j)),
        ],
        out_specs=pl.BlockSpec((None, rows, tn), lambda l, j: (l, 0, j)),
        compiler_params=_params("arbitrary", "arbitrary"),
        name="modulation",
    )(cc, w, b)


def _head_rms(t, g, bd):
    ms = jnp.dot((t * t).astype(BF16), bd, preferred_element_type=F32)
    return t * lax.rsqrt(ms + EPS) * g


def _rope(t, cos, sin, low_half):
    outs = []
    for j in range(t.shape[1] // LANES):
        tc = t[:, j * LANES:(j + 1) * LANES]
        partner = jnp.where(low_half, pltpu.roll(tc, LANES - 16, 1), pltpu.roll(tc, 16, 1))
        outs.append(tc * cos + partner * sin)
    return outs[0] if len(outs) == 1 else jnp.concatenate(outs, axis=-1)


def _inproj_kernel(x_ref, mod_ref, gpre_ref, w_ref, gq_ref, gk_ref, bd_ref, cos_ref, sin_ref,
                   qa_ref, ka_ref, va_ref, qb_ref, kb_ref, vb_ref, xr_ref, gr_ref, *, rope):
    mod = mod_ref[0]
    bd = bd_ref[...]
    tm = x_ref.shape[1]
    sub = min(SUB_TILE, tm)
    pre = []
    for r0 in range(0, tm, sub):
        rows = slice(r0, r0 + sub)
        h = _rms(x_ref[0, rows, :], gpre_ref[...]) * (1.0 + mod[1:2]) + mod[0:1]
        pre.append((rows, h.astype(BF16)))
    for rows, hb in pre:
        z = jnp.dot(hb, w_ref[...], preferred_element_type=F32)
        qa = _head_rms(z[:, 0:A_Q], gq_ref[...], bd)
        ka = _head_rms(z[:, A_Q:A_Q + A_KV], gk_ref[...], bd[:A_KV, :A_KV])
        o = A_Q + 2 * A_KV
        qb = z[:, o:o + B_Q]
        kb = z[:, o + B_Q:o + B_Q + B_KV]
        if rope:
            cos = cos_ref[rows, :]
            sin = sin_ref[rows, :]
            lane = lax.broadcasted_iota(jnp.int32, cos.shape, 1)
            low_half = (lane % 32) < 16
            qa = _rope(qa, cos, sin, low_half)
            ka = _rope(ka, cos, sin, low_half)
            qb = _rope(qb, cos, sin, low_half)
            kb = _rope(kb, cos, sin, low_half)
        qa_ref[0, rows, :] = (qa * Q_SCALE).astype(BF16)
        ka_ref[0, rows, :] = ka.astype(BF16)
        va_ref[0, rows, :] = z[:, A_Q + A_KV:o].astype(BF16)
        qb_ref[0, rows, :] = (qb * Q_SCALE).astype(BF16)
        kb_ref[0, rows, :] = kb.astype(BF16)
        vb_ref[0, rows, :] = z[:, o + B_Q + B_KV:o + B_Q + 2 * B_KV].astype(BF16)
        o2 = o + B_Q + 2 * B_KV
        xr_ref[0, rows, :] = z[:, o2:o2 + LRU_WIDTH]
        gr_ref[0, rows, :] = z[:, o2 + LRU_WIDTH:o2 + 2 * LRU_WIDTH]


def _inproj(x, mod, gpre, w, gq, gk, bd, cos, sin, *, rope):
    b, n, d = x.shape
    tm = min(INPROJ_TILE, n)
    nt = n // tm
    widths = (A_Q, A_KV, A_KV, B_Q, B_KV, B_KV, LRU_WIDTH, LRU_WIDTH)
    dtypes = (BF16,) * 6 + (F32, F32)
    row_spec = lambda wd: pl.BlockSpec((1, tm, wd), lambda i, j: (i, j, 0))
    table_spec = pl.BlockSpec((tm, LANES), (lambda i, j: (j, 0)) if rope else (lambda i, j: (0, 0)))
    return pl.pallas_call(
        functools.partial(_inproj_kernel, rope=rope),
        out_shape=[jax.ShapeDtypeStruct((b, n, wd), dt) for wd, dt in zip(widths, dtypes)],
        grid=(b, nt),
        in_specs=[
            row_spec(d),
            pl.BlockSpec((1, 6, d), lambda i, j: (i, 0, 0)),
            _const_spec(gpre.shape),
            _const_spec(w.shape),
            _const_spec(gq.shape),
            _const_spec(gk.shape),
            _const_spec(bd.shape),
            table_spec,
            table_spec,
        ],
        out_specs=[row_spec(wd) for wd in widths],
        compiler_params=_params("parallel", "parallel"),
        name="inproj_rope" if rope else "inproj_ctx",
    )(x, mod, gpre, w, gq, gk, bd, cos, sin)


def _split_heads(qc, low):
    zero = jnp.zeros_like(qc)
    return jnp.where(low, qc, zero), jnp.where(low, zero, qc)


def _stack_heads(q, low):
    parts = []
    for j in range(q.shape[1] // LANES):
        parts.extend(_split_heads(q[:, j * LANES:(j + 1) * LANES], low))
    return jnp.concatenate(parts, axis=0)


def _unstack_heads(o, t, low):
    return jnp.concatenate([jnp.where(low, o[0:t], o[t:2 * t]),
                            jnp.where(low, o[2 * t:3 * t], o[3 * t:4 * t])], axis=-1)


def _with_ones(v):
    return jnp.concatenate([v, jnp.ones_like(v)], axis=-1)


def _chain_sizes(total):
    assert total % ATTN_CHAIN == 0
    half = ATTN_CHAIN // 2
    return (half,) + (ATTN_CHAIN,) * (total // ATTN_CHAIN - 1) + (half,)


def _attend(q, keys, vals, sink_ref):
    tq = q.shape[0]
    has_sink = sink_ref is not None
    low = lax.broadcasted_iota(jnp.int32, (tq, LANES), 1) < HEAD_DIM
    qs_all = _stack_heads(q, low)
    if has_sink:
        sink_all = jnp.concatenate([jnp.full((tq, LANES), sink_ref[h], F32) for h in (0, 2, 1, 3)], axis=0)
    outs, r0 = [], 0
    for rows in _chain_sizes(4 * tq):
        qs = qs_all[r0:r0 + rows]
        scores = [lax.dot_general(qs, k, _NT, preferred_element_type=F32) for k in keys]
        s = scores[0] if len(keys) == 1 else jnp.concatenate(scores, axis=-1)
        m = jnp.max(s, axis=-1, keepdims=True)
        if has_sink:
            sink = sink_all[r0:r0 + rows]
            m_wide = jnp.maximum(jnp.broadcast_to(m, sink.shape), sink)
            p = jnp.exp(s - jnp.concatenate([m_wide] * (s.shape[1] // LANES), axis=-1)).astype(BF16)
        else:
            p = jnp.exp(s - m).astype(BF16)
        acc, c0 = None, 0
        for k, v in zip(keys, vals):
            pv = jnp.dot(p[:, c0:c0 + k.shape[0]], v, preferred_element_type=F32)
            acc = pv if acc is None else acc + pv
            c0 += k.shape[0]
        denom = acc[:, LANES:]
        if has_sink:
            denom = denom + jnp.exp(sink - m_wide)
        outs.append(acc[:, :LANES] / denom)
        r0 += rows
    return _unstack_heads(jnp.concatenate(outs, axis=0), tq, low)


def _dense_attn_kernel(*refs, n_kv):
    q_ref, kv_refs, o_ref = refs[0], refs[1:1 + 2 * n_kv], refs[-1]
    keys = [kv_refs[2 * i][0] for i in range(n_kv)]
    vals = [_with_ones(kv_refs[2 * i + 1][0]) for i in range(n_kv)]
    o_ref[0] = _attend(q_ref[0], keys, vals, None).astype(o_ref.dtype)


def _ctx_attn_kernel(qa_ref, ka_ref, va_ref, qb_ref, kb_ref, vb_ref, sink_ref, oa_ref, ob_ref):
    for i in range(qa_ref.shape[0]):
        oa_ref[i] = _attend(qa_ref[i], [ka_ref[i]], [_with_ones(va_ref[i])], None).astype(oa_ref.dtype)
        ob_ref[i] = _attend(qb_ref[i], [kb_ref[i]], [_with_ones(vb_ref[i])], sink_ref).astype(ob_ref.dtype)


def _dense_attn(q, kvs, name):
    b, n, w = q.shape
    tq = min(ATTN_Q_TILE, n)
    in_specs = [pl.BlockSpec((1, tq, w), lambda i, j: (i, j, 0))]
    args = [q]
    for k, v in kvs:
        for t in (k, v):
            in_specs.append(pl.BlockSpec((1,) + t.shape[1:], lambda i, j: (i, 0, 0)))
            args.append(t)
    return pl.pallas_call(
        functools.partial(_dense_attn_kernel, n_kv=len(kvs)),
        out_shape=jax.ShapeDtypeStruct((b, n, w), BF16),
        grid=(b, n // tq),
        in_specs=in_specs,
        out_specs=pl.BlockSpec((1, tq, w), lambda i, j: (i, j, 0)),
        compiler_params=_params("parallel", "parallel"),
        name=name,
    )(*args)


def _ctx_attn(qa, ka, va, qb, kb, vb, sink):
    b = qa.shape[0]
    bt = CTX_ATTN_BATCH if b % CTX_ATTN_BATCH == 0 else 1
    spec = lambda t: pl.BlockSpec((bt,) + t.shape[1:], lambda i: (i, 0, 0))
    args = (qa, ka, va, qb, kb, vb)
    return pl.pallas_call(
        _ctx_attn_kernel,
        out_shape=[jax.ShapeDtypeStruct(qa.shape, BF16), jax.ShapeDtypeStruct(qb.shape, BF16)],
        grid=(b // bt,),
        in_specs=[spec(t) for t in args] + [pl.BlockSpec(memory_space=pltpu.SMEM)],
        out_specs=[spec(qa), spec(qb)],
        compiler_params=_params("parallel"),
        name="ctx_attn",
    )(*args, sink)


def _win_attn_kernel(q_ref, k_ref, v_ref, kc_ref, vc_ref, sink_ref, o_ref):
    n = q_ref.shape[1]
    qb = WINDOW
    band = 3 * qb
    rows = WIN_GROUP * qb
    n_iter = n // rows
    low = lax.broadcasted_iota(jnp.int32, (qb, LANES), 1) < HEAD_DIM
    row_minus_col = (lax.broadcasted_iota(jnp.int32, (qb, band), 0)
                     - lax.broadcasted_iota(jnp.int32, (qb, band), 1))

    def band_bias(first_key_offset):
        bias = jnp.where(jnp.abs(row_minus_col + first_key_offset) <= WINDOW, 0.0, NEG_INF)
        return jnp.concatenate([bias.astype(F32)] * 4, axis=0)

    bias_first, bias_mid, bias_last = band_bias(0), band_bias(qb), band_bias(2 * qb)
    kc = kc_ref[0]
    vc = _with_ones(vc_ref[0])
    ones_band = jnp.ones((band, LANES), BF16)
    sink_blk = jnp.concatenate([jnp.full((qb, LANES), sink_ref[h], F32) for h in (0, 2, 1, 3)], axis=0)
    per_chain = WIN_GROUP // WIN_CHAINS
    sink = jnp.concatenate([sink_blk] * per_chain, axis=0)

    def chain(it, t0, q, blocks):
        qs_blk = [_stack_heads(q[g * qb:(g + 1) * qb], low) for g in blocks]
        starts = [pl.multiple_of(jnp.clip(t0 + (g - 1) * qb, 0, n - band), qb) for g in blocks]
        s_ctx = lax.dot_general(jnp.concatenate(qs_blk, axis=0), kc, _NT, preferred_element_type=F32)
        s_band = []
        for i, g in enumerate(blocks):
            s = lax.dot_general(qs_blk[i], k_ref[0, pl.ds(starts[i], band), :], _NT,
                                preferred_element_type=F32)
            bias = bias_mid
            if g == 0:
                bias = jnp.where(it == 0, bias_first, bias)
            if g == WIN_GROUP - 1:
                bias = jnp.where(it == n_iter - 1, bias_last, bias)
            s_band.append(s + bias)
        s = jnp.concatenate([jnp.concatenate(s_band, axis=0), s_ctx], axis=-1)
        m = jnp.maximum(jnp.broadcast_to(jnp.max(s, axis=-1, keepdims=True), sink.shape), sink)
        p = jnp.exp(s - jnp.concatenate([m] * (s.shape[1] // LANES), axis=-1)).astype(BF16)
        acc_band = [jnp.dot(p[i * 4 * qb:(i + 1) * 4 * qb, :band],
                            jnp.concatenate([v_ref[0, pl.ds(starts[i], band), :], ones_band], axis=-1),
                            preferred_element_type=F32) for i in range(len(blocks))]
        acc = jnp.concatenate(acc_band, axis=0) + jnp.dot(p[:, band:], vc, preferred_element_type=F32)
        out = acc[:, :LANES] / (acc[:, LANES:] + jnp.exp(sink - m))
        return [_unstack_heads(out[i * 4 * qb:(i + 1) * 4 * qb], qb, low) for i in range(len(blocks))]

    def body(it, carry):
        t0 = pl.multiple_of(it * rows, rows)
        q = q_ref[0, pl.ds(t0, rows), :]
        outs = []
        for c in range(WIN_CHAINS):
            outs.extend(chain(it, t0, q, range(c * per_chain, (c + 1) * per_chain)))
        o_ref[0, pl.ds(t0, rows), :] = jnp.concatenate(outs, axis=0).astype(o_ref.dtype)
        return carry

    lax.fori_loop(0, n_iter, body, 0)


def _win_attn(q, k, v, kc, vc, sink):
    b, n, w = q.shape
    full = lambda t: pl.BlockSpec((1,) + t.shape[1:], lambda i: (i, 0, 0))
    return pl.pallas_call(
        _win_attn_kernel,
        out_shape=jax.ShapeDtypeStruct((b, n, w), BF16),
        grid=(b,),
        in_specs=[full(q), full(k), full(v), full(kc), full(vc),
                  pl.BlockSpec(memory_space=pltpu.SMEM)],
        out_specs=full(q),
        compiler_params=_params("parallel"),
        name="window_attn",
    )(q, k, v, kc, vc, sink)


def _gelu_tanh(x):
    c = (2.0 / jnp.pi) ** 0.5
    half_x = 0.5 * x
    return half_x + half_x * jnp.tanh(x * (c + (c * 0.044715) * (x * x)))


def _scan_block(a_scr, u_scr, h_scr, base, col, c_in, row8, reverse):
    order = range(SUBLANES - 1, -1, -1) if reverse else range(SUBLANES)
    a_cum, h_loc = [], []
    for j in order:
        aj = a_scr[col, base + j * SUBLANES:base + (j + 1) * SUBLANES, :]
        uj = u_scr[col, base + j * SUBLANES:base + (j + 1) * SUBLANES, :]
        if a_cum:
            h_loc.append(aj * h_loc[-1] + uj)
            a_cum.append(aj * a_cum[-1])
        else:
            h_loc.append(uj)
            a_cum.append(aj)
    p, q = a_cum[-1], h_loc[-1]
    for s in (1, 2, 4):
        shift = SUBLANES - s if reverse else s
        valid = (row8 < SUBLANES - s) if reverse else (row8 >= s)
        q = jnp.where(valid, p * pltpu.roll(q, shift, 0) + q, q)
        p = jnp.where(valid, p * pltpu.roll(p, shift, 0), p)
    end = q + p * c_in
    if reverse:
        c_grp = jnp.where(row8 < SUBLANES - 1, pltpu.roll(end, SUBLANES - 1, 0), c_in)
        last = end[0:1]
    else:
        c_grp = jnp.where(row8 >= 1, pltpu.roll(end, 1, 0), c_in)
        last = end[SUBLANES - 1:SUBLANES]
    for idx, j in enumerate(order):
        h_scr[col, pl.ds(base + j, SUBLANES, stride=SUBLANES), :] = h_loc[idx] + a_cum[idx] * c_grp
    return jnp.broadcast_to(last, (SUBLANES, LANES))


def _conv_block_order(src_ref, t0, rows, cw_ref, cb_ref, xt_scr, row8):
    length = src_ref.shape[1]
    blk = SCAN_BLOCK
    n_blk = rows // blk
    n_col = LRU_WIDTH // LANES
    lo = jnp.maximum(t0 - blk, 0)
    hi = jnp.minimum(t0 + rows, length - blk)
    prev = jnp.where(t0 > 0, src_ref[0, pl.ds(pl.multiple_of(lo, blk), blk), :], 0.0)
    nxt = jnp.where(t0 + rows < length, src_ref[0, pl.ds(pl.multiple_of(hi, blk), blk), :], 0.0)
    win = jnp.concatenate([prev, src_ref[0, pl.ds(t0, rows), :], nxt], axis=0)
    for col in range(n_col):
        for i in range(win.shape[0] // SUBLANES):
            b, g = divmod(i, SUBLANES)
            xt_scr[col, pl.ds(b * blk + g, SUBLANES, stride=SUBLANES), :] = (
                win[i * SUBLANES:(i + 1) * SUBLANES, col * LANES:(col + 1) * LANES])

    def x_at(b, j, col):
        base = (b + 1) * blk + j * SUBLANES
        return xt_scr[col, base:base + SUBLANES, :]

    out_rows = []
    for b in range(n_blk):
        per_j = [[] for _ in range(SUBLANES)]
        for col in range(n_col):
            lanes = slice(col * LANES, (col + 1) * LANES)
            xs = [x_at(b, j, col) for j in range(SUBLANES)]
            before = [jnp.where(row8 >= 1, pltpu.roll(xs[j], 1, 0), pltpu.roll(x_at(b - 1, j, col), 1, 0))
                      for j in (SUBLANES - 2, SUBLANES - 1)]
            after = jnp.where(row8 < SUBLANES - 1, pltpu.roll(xs[0], SUBLANES - 1, 0),
                              pltpu.roll(x_at(b + 1, 0, col), SUBLANES - 1, 0))
            ext = before + xs + [after]
            for j in range(SUBLANES):
                acc = cb_ref[:, lanes]
                for o in range(CONV_W):
                    acc = acc + ext[j + o] * cw_ref[o:o + 1, lanes]
                per_j[j].append(acc)
        out_rows.extend(jnp.concatenate(vs, axis=-1) for vs in per_j)
    return jnp.concatenate(out_rows, axis=0)


def _lru_chunk(src_ref, conv_ref, t0, rows, d, carry, reverse, cw_ref, cb_ref, wg_ref, bg_ref,
               quarter_nsp, xt_scr, a_scr, u_scr, h_scr, row8):
    if reverse:
        xc = conv_ref[pl.ds(t0, rows), :]
    else:
        xc = _conv_block_order(src_ref, t0, rows, cw_ref, cb_ref, xt_scr, row8)
        conv_ref[pl.ds(t0, rows), :] = xc
    xb = xc.astype(BF16)
    half = LRU_WIDTH // 2
    y0 = jnp.dot(xb[:, :half], wg_ref[d, 0], preferred_element_type=F32)
    y1 = jnp.dot(xb[:, half:], wg_ref[d, 1], preferred_element_type=F32)
    ya = jnp.concatenate([y0[:, :half], y1[:, :half]], axis=-1)
    yi = jnp.concatenate([y0[:, half:], y1[:, half:]], axis=-1)
    t_r = jnp.tanh(ya + bg_ref[2 * d:2 * d + 1, :])
    t_i = jnp.tanh(yi + bg_ref[2 * d + 1:2 * d + 2, :])
    t = jnp.tanh(t_r * quarter_nsp + quarter_nsp)
    inv = 1.0 / (1.0 - t)
    a = (1.0 + t) * inv
    neg_t = -t
    root = jnp.where(neg_t > 0.0, neg_t * lax.rsqrt(neg_t), 0.0)
    u = root * inv * ((t_i + 1.0) * xc)
    for col in range(LRU_WIDTH // LANES):
        a_scr[col, 0:rows, :] = a[:, col * LANES:(col + 1) * LANES]
        u_scr[col, 0:rows, :] = u[:, col * LANES:(col + 1) * LANES]
    n_blk = rows // SCAN_BLOCK
    blocks = range(n_blk - 1, -1, -1) if reverse else range(n_blk)
    carry = list(carry)
    for blk in blocks:
        for col in range(LRU_WIDTH // LANES):
            carry[col] = _scan_block(a_scr, u_scr, h_scr, blk * SCAN_BLOCK, col, carry[col],
                                     row8, reverse)
    return tuple(carry)


def _lru_kernel(xc_ref, gc_ref, x_ref, g_ref, cw_ref, cb_ref, wg_ref, bg_ref, lam_ref, *rest,
                with_ctx):
    if with_ctx:
        o_ref, oc_ref, xconv, xcconv, xt_scr, a_scr, u_scr, h_scr, hf, hfc = rest
    else:
        o_ref, xconv, xcconv, xt_scr, a_scr, u_scr, h_scr, hf = rest
    n = x_ref.shape[1]
    m = xc_ref.shape[1]

    neg_lam = -lam_ref[...]
    softplus = jnp.maximum(neg_lam, 0.0) + jnp.log1p(jnp.exp(-jnp.abs(neg_lam)))
    quarter_nsp = (-0.25 * LRU_C) * softplus
    row8 = lax.broadcasted_iota(jnp.int32, (SUBLANES, LANES), 0)
    zero_carry = tuple(jnp.zeros((SUBLANES, LANES), F32) for _ in range(LRU_WIDTH // LANES))
    n_chunks = n // LRU_CHUNK

    def chunk(ctx_part, t0, rows, d, carry):
        src_ref, conv_ref = (xc_ref, xcconv) if ctx_part else (x_ref, xconv)
        return _lru_chunk(src_ref, conv_ref, t0, rows, d, carry, d == 1, cw_ref, cb_ref, wg_ref, bg_ref,
                          quarter_nsp[d:d + 1, :], xt_scr, a_scr, u_scr, h_scr, row8)

    def read_h(rows):
        return jnp.concatenate([h_scr[col, 0:rows, :] for col in range(LRU_WIDTH // LANES)], axis=-1)

    carry = chunk(True, 0, m, 0, zero_carry)
    if with_ctx:
        hfc[...] = read_h(m)

    def fwd_body(ci, carry):
        t0 = pl.multiple_of(ci * LRU_CHUNK, LRU_CHUNK)
        carry = chunk(False, t0, LRU_CHUNK, 0, carry)
        hf[pl.ds(t0, LRU_CHUNK), :] = read_h(LRU_CHUNK)
        return carry

    lax.fori_loop(0, n_chunks, fwd_body, carry)

    carry = chunk(True, 0, m, 1, zero_carry)
    if with_ctx:
        oc_ref[0] = ((hfc[...] + read_h(m)) * _gelu_tanh(gc_ref[0])).astype(oc_ref.dtype)

    def bwd_body(ci, carry):
        t0 = pl.multiple_of((n_chunks - 1 - ci) * LRU_CHUNK, LRU_CHUNK)
        carry = chunk(False, t0, LRU_CHUNK, 1, carry)
        hf[pl.ds(t0, LRU_CHUNK), :] = hf[pl.ds(t0, LRU_CHUNK), :] + read_h(LRU_CHUNK)
        return carry

    lax.fori_loop(0, n_chunks, bwd_body, carry)

    def gate_body(ci, _):
        t0 = pl.multiple_of(ci * LRU_CHUNK, LRU_CHUNK)
        gate = _gelu_tanh(g_ref[0, pl.ds(t0, LRU_CHUNK), :])
        o_ref[0, pl.ds(t0, LRU_CHUNK), :] = (hf[pl.ds(t0, LRU_CHUNK), :] * gate).astype(o_ref.dtype)
        return 0

    lax.fori_loop(0, n_chunks, gate_body, 0)


def _lru(xc, gc, x, g, cw, cb, wg, bg, lam, *, with_ctx):
    b, n, w = x.shape
    m = xc.shape[1]
    full = lambda t: pl.BlockSpec((1,) + t.shape[1:], lambda i: (i, 0, 0))
    out_shape = [jax.ShapeDtypeStruct((b, n, w), BF16)]
    out_specs = [pl.BlockSpec((1, n, w), lambda i: (i, 0, 0))]
    scratch = [pltpu.VMEM((n, w), F32), pltpu.VMEM((m, w), F32),
               pltpu.VMEM((w // LANES, LRU_CHUNK + 2 * SCAN_BLOCK, LANES), F32),
               *[pltpu.VMEM((w // LANES, LRU_CHUNK, LANES), F32) for _ in range(3)],
               pltpu.VMEM((n, w), F32)]
    if with_ctx:
        out_shape.append(jax.ShapeDtypeStruct((b, m, w), BF16))
        out_specs.append(pl.BlockSpec((1, m, w), lambda i: (i, 0, 0)))
        scratch.append(pltpu.VMEM((m, w), F32))
    outs = pl.pallas_call(
        functools.partial(_lru_kernel, with_ctx=with_ctx),
        out_shape=out_shape,
        grid=(b,),
        in_specs=[full(xc), full(gc), full(x), full(g), _const_spec(cw.shape), _const_spec(cb.shape),
                  _const_spec(wg.shape), _const_spec(bg.shape), _const_spec(lam.shape)],
        out_specs=out_specs,
        scratch_shapes=scratch,
        compiler_params=_params("parallel"),
        name="rglru_ctx" if with_ctx else "rglru",
    )(xc, gc, x, g, cw, cb, wg, bg, lam)
    return (outs[0], outs[1]) if with_ctx else (outs[0], None)


def _post_kernel(fa_ref, fb_ref, fc_ref, x_ref, mod_ref, wo_ref, gpost_ref, gpre_ref, w1_ref, w2_ref,
                 gpm_ref, o_ref):
    mod = mod_ref[0]
    tm = x_ref.shape[1]
    sub = min(POST_SUB_TILE, tm)
    d_ff = w1_ref.shape[1]
    ff_chunk = d_ff // 4
    pre = []
    for r0 in range(0, tm, sub):
        rows = slice(r0, r0 + sub)
        y = (jnp.dot(fa_ref[0, rows, :], wo_ref[0:A_Q, :], preferred_element_type=F32)
             + jnp.dot(fb_ref[0, rows, :], wo_ref[A_Q:A_Q + B_Q, :], preferred_element_type=F32)
             + jnp.dot(fc_ref[0, rows, :], wo_ref[A_Q + B_Q:, :], preferred_element_type=F32))
        x1 = x_ref[0, rows, :] + mod[2:3] * _rms(y, gpost_ref[...])
        h2 = (_rms(x1, gpre_ref[...]) * (1.0 + mod[4:5]) + mod[3:4]).astype(BF16)
        pre.append((rows, x1, h2))
    for rows, x1, h2 in pre:
        acc = None
        for k in range(d_ff // ff_chunk):
            hk = jnp.dot(h2, w1_ref[:, k * ff_chunk:(k + 1) * ff_chunk], preferred_element_type=F32)
            hk = jnp.square(jnp.maximum(hk, 0.0)).astype(BF16)
            part = jnp.dot(hk, w2_ref[k * ff_chunk:(k + 1) * ff_chunk, :], preferred_element_type=F32)
            acc = part if acc is None else acc + part
        o_ref[0, rows, :] = x1 + mod[5:6] * _rms(acc, gpm_ref[...])


def _post(fa, fb, fc, x, mod, wo, gpost, gpre, w1, w2, gpm, name):
    b, n, d = x.shape
    tm = min(ROW_TILE, n)
    row_spec = lambda wd: pl.BlockSpec((1, tm, wd), lambda i, j: (i, j, 0))
    return pl.pallas_call(
        _post_kernel,
        out_shape=jax.ShapeDtypeStruct((b, n, d), F32),
        grid=(b, n // tm),
        in_specs=[row_spec(fa.shape[2]), row_spec(fb.shape[2]), row_spec(fc.shape[2]), row_spec(d),
                  pl.BlockSpec((1, 6, d), lambda i, j: (i, 0, 0)),
                  _const_spec(wo.shape), _const_spec(gpost.shape), _const_spec(gpre.shape),
                  _const_spec(w1.shape), _const_spec(w2.shape), _const_spec(gpm.shape)],
        out_specs=row_spec(d),
        compiler_params=_params("parallel", "parallel"),
        name=name,
    )(fa, fb, fc, x, mod, wo, gpost, gpre, w1, w2, gpm)


def _rope_tables(n):
    rows = n // GRID_W
    row = jnp.repeat(jnp.arange(rows, dtype=F32), GRID_W)
    col = jnp.tile(jnp.arange(GRID_W, dtype=F32), rows)
    half = HEAD_DIM // 2
    inv_freq = ROPE_BASE ** (-jnp.arange(0, half, 2, dtype=F32) / half)
    ang_r = row[:, None] * inv_freq
    ang_c = col[:, None] * inv_freq
    cr, sr, cc, sc = jnp.cos(ang_r), jnp.sin(ang_r), jnp.cos(ang_c), jnp.sin(ang_c)
    cos = jnp.concatenate([cr, cr, cc, cc], axis=-1)
    sin = jnp.concatenate([-sr, sr, -sc, sc], axis=-1)
    reps = LANES // HEAD_DIM
    return jnp.tile(cos, (1, reps)), jnp.tile(sin, (1, reps))


def _permute_heads(w, axis, bases):
    pieces, pos = [], 0
    for base in bases:
        pieces.append(lax.slice_in_dim(w, pos, base, axis=axis))
        for h in (0, 2, 1, 3):
            pieces.append(lax.slice_in_dim(w, base + h * HEAD_DIM, base + (h + 1) * HEAD_DIM, axis=axis))
        pos = base + 4 * HEAD_DIM
    pieces.append(lax.slice_in_dim(w, pos, w.shape[axis], axis=axis))
    return jnp.concatenate([p for p in pieces if p.shape[axis]], axis=axis)


def _block_diag(w):
    nb, hb, _ = w.shape
    per = nb // 2
    eye = jnp.eye(per, dtype=w.dtype)
    blocks = w.reshape(2, per, hb, 1, hb) * eye[None, :, None, :, None]
    return blocks.reshape(2, per * hb, per * hb)


def kernel(x, c, ctx, c_ctx, w_mod, b_mod, g_pre_mix, g_post_mix, g_pre_mlp, g_post_mlp, w_in, g_q_a,
           g_k_a, sink_b, conv_w, conv_b, lru_w_a, lru_b_a, lru_w_i, lru_b_i, lru_lambda, w_out,
           w_mlp_in, w_mlp_out):
    b, n, d = x.shape
    depth = w_mod.shape[0]
    cos, sin = _rope_tables(n)
    seg = jnp.arange(A_Q) // HEAD_DIM
    bd = jnp.where(seg[:, None] == seg[None, :], 1.0 / HEAD_DIM, 0.0).astype(BF16)

    mod_rows = -(-(b + 1) // SUBLANES) * SUBLANES
    cc = jnp.zeros((mod_rows, d), F32).at[:b].set(c).at[b].set(c_ctx)

    mod_all = _modulation(cc, w_mod, b_mod[:, None, :])
    m = ctx.shape[1]
    flat = lambda t: t.reshape(1, b * m, t.shape[-1])
    unflat = lambda t: t.reshape(b, m, t.shape[-1])

    for l in range(depth):
        last = l == depth - 1
        mod_lat = mod_all[l, :b].reshape(b, 6, d)
        mod_ctx = mod_all[l, b].reshape(1, 6, d)

        w_in_l = _permute_heads(w_in[l], 1, (0, A_Q + 2 * A_KV)).astype(BF16)
        gq = jnp.tile(g_q_a[l], A_Q_HEADS)[None, :]
        gk = jnp.tile(g_k_a[l], A_KV_HEADS)[None, :]
        gpre = g_pre_mix[l][None, :]
        qa, ka, va, qb, kb, vb, xr, gr = _inproj(x, mod_lat, gpre, w_in_l, gq, gk, bd, cos, sin, rope=True)
        qa_c, ka_c, va_c, qb_c, kb_c, vb_c, xr_c, gr_c = map(unflat, _inproj(
            flat(ctx), mod_ctx, gpre, w_in_l, gq, gk, bd, cos, sin, rope=False))

        feat_a = _dense_attn(qa, [(ka_c, va_c), (ka, va)], "global_attn")
        feat_b = _win_attn(qb, kb, vb, kb_c, vb_c, sink_b[l])

        wg = (0.5 * jnp.stack([jnp.concatenate([_block_diag(lru_w_a[l, dd]), _block_diag(lru_w_i[l, dd])],
                                               axis=-1) for dd in range(2)])).astype(BF16)
        bg = 0.5 * jnp.stack([lru_b_a[l, 0], lru_b_i[l, 0], lru_b_a[l, 1], lru_b_i[l, 1]])
        feat_c, feat_cc = _lru(xr_c, gr_c, xr, gr, conv_w[l], conv_b[l][None, :], wg, bg, lru_lambda[l],
                               with_ctx=not last)

        wo = _permute_heads(w_out[l], 0, (0, A_Q)).astype(BF16)
        post_w = (wo, g_post_mix[l][None, :], g_pre_mlp[l][None, :], w_mlp_in[l].astype(BF16),
                  w_mlp_out[l].astype(BF16), g_post_mlp[l][None, :])
        x = _post(feat_a, feat_b, feat_c, x, mod_lat, *post_w, name="post_mlp")
        if not last:
            feat_ac, feat_bc = _ctx_attn(qa_c, ka_c, va_c, qb_c, kb_c, vb_c, sink_b[l])
            ctx = unflat(_post(flat(feat_ac), flat(feat_bc), flat(feat_cc), flat(ctx), mod_ctx, *post_w,
                               name="post_mlp_ctx"))
    return x
```

```python
import functools

import jax
import jax.numpy as jnp
from jax import lax
from jax.experimental import pallas as pl
from jax.experimental.pallas import tpu as pltpu

F32 = jnp.float32
BF16 = jnp.bfloat16

GRID_W = 64
HEAD_DIM = 64
A_Q_HEADS = 4
A_KV_HEADS = 2
B_Q_HEADS = 4
B_KV_HEADS = 2
WINDOW = 128
LRU_WIDTH = 512
LRU_BLOCKS = 8
CONV_W = 4
CONV_LEFT = CONV_W // 2
LRU_C = 8.0
ROPE_BASE = 10000.0
EPS = 1e-6
NEG_INF = -1e30

A_Q = A_Q_HEADS * HEAD_DIM
A_KV = A_KV_HEADS * HEAD_DIM
B_Q = B_Q_HEADS * HEAD_DIM
B_KV = B_KV_HEADS * HEAD_DIM
Q_SCALE = HEAD_DIM ** -0.5

LANES = 128
SUBLANES = 8
VMEM_LIMIT = 56 * 1024 * 1024

ROW_TILE = 1024
INPROJ_TILE = 1024
SUB_TILE = 256
POST_SUB_TILE = 512
ATTN_Q_TILE = 1024
CTX_ATTN_BATCH = 4
ATTN_CHAIN = 512
WIN_GROUP = 16
WIN_CHAINS = 4
LRU_CHUNK = 1024
SCAN_BLOCK = SUBLANES * SUBLANES

_NT = (((1,), (1,)), ((), ()))


def _const_spec(shape):
    return pl.BlockSpec(shape, lambda *_: (0,) * len(shape), pipeline_mode=pl.Buffered(1))


def _params(*sem):
    return pltpu.CompilerParams(dimension_semantics=sem, vmem_limit_bytes=VMEM_LIMIT)


def _rms(x, g):
    return x * lax.rsqrt(jnp.mean(x * x, axis=-1, keepdims=True) + EPS) * g


def _gelu_tanh(x):
    c = (2.0 / jnp.pi) ** 0.5
    half_x = 0.5 * x
    return half_x + half_x * jnp.tanh(x * (c + (c * 0.044715) * (x * x)))


def _mod_kernel(c_ref, w_ref, b_ref, o_ref):
    c = c_ref[...]
    act = (c * jax.nn.sigmoid(c)).astype(BF16)
    o_ref[...] = jnp.dot(act, w_ref[...].astype(BF16), preferred_element_type=F32) + b_ref[...]


def _modulation(cc, w, b):
    rows, d = cc.shape
    depth, _, n_out = w.shape
    tn = n_out // 4
    return pl.pallas_call(
        _mod_kernel,
        out_shape=jax.ShapeDtypeStruct((depth, rows, n_out), F32),
        grid=(depth, n_out // tn),
        in_specs=[
            pl.BlockSpec((rows, d), lambda l, j: (0, 0)),
            pl.BlockSpec((None, d, tn), lambda l, j: (l, 0, j)),
            pl.BlockSpec((None, 1, tn), lambda l, j: (l, 0, j)),
        ],
        out_specs=pl.BlockSpec((None, rows, tn), lambda l, j: (l, 0, j)),
        compiler_params=_params("arbitrary", "arbitrary"),
        name="modulation",
    )(cc, w, b)


def _head_rms(t, g, bd):
    ms = jnp.dot((t * t).astype(BF16), bd, preferred_element_type=F32)
    return t * lax.rsqrt(ms + EPS) * g


def _rope(t, cos, sin, low_half):
    outs = []
    for j in range(t.shape[1] // LANES):
        tc = t[:, j * LANES:(j + 1) * LANES]
        partner = jnp.where(low_half, pltpu.roll(tc, LANES - 16, 1), pltpu.roll(tc, 16, 1))
        outs.append(tc * cos + partner * sin)
    return outs[0] if len(outs) == 1 else jnp.concatenate(outs, axis=-1)


def _inproj_kernel(x_ref, mod_ref, gpre_ref, w_ref, gq_ref, gk_ref, bd_ref, cos_ref, sin_ref,
                   qa_ref, ka_ref, va_ref, qb_ref, kb_ref, vb_ref, xr_ref, gr_ref, *, rope):
    mod = mod_ref[0]
    bd = bd_ref[...]
    tm = x_ref.shape[1]
    sub = min(SUB_TILE, tm)
    pre = []
    for r0 in range(0, tm, sub):
        rows = slice(r0, r0 + sub)
        h = _rms(x_ref[0, rows, :], gpre_ref[...]) * (1.0 + mod[1:2]) + mod[0:1]
        pre.append((rows, h.astype(BF16)))
    o2 = A_Q + 2 * A_KV + B_Q + 2 * B_KV
    for rows, hb in pre:
        z = jnp.dot(hb, w_ref[:, :o2], preferred_element_type=F32)
        qa = _head_rms(z[:, 0:A_Q], gq_ref[...], bd)
        ka = _head_rms(z[:, A_Q:A_Q + A_KV], gk_ref[...], bd[:A_KV, :A_KV])
        o = A_Q + 2 * A_KV
        qb = z[:, o:o + B_Q]
        kb = z[:, o + B_Q:o + B_Q + B_KV]
        if rope:
            cos = cos_ref[rows, :]
            sin = sin_ref[rows, :]
            lane = lax.broadcasted_iota(jnp.int32, cos.shape, 1)
            low_half = (lane % 32) < 16
            qa = _rope(qa, cos, sin, low_half)
            ka = _rope(ka, cos, sin, low_half)
            qb = _rope(qb, cos, sin, low_half)
            kb = _rope(kb, cos, sin, low_half)
        qa_ref[0, rows, :] = (qa * Q_SCALE).astype(BF16)
        ka_ref[0, rows, :] = ka.astype(BF16)
        va_ref[0, rows, :] = z[:, A_Q + A_KV:o].astype(BF16)
        qb_ref[0, rows, :] = (qb * Q_SCALE).astype(BF16)
        kb_ref[0, rows, :] = kb.astype(BF16)
        vb_ref[0, rows, :] = z[:, o + B_Q + B_KV:o + B_Q + 2 * B_KV].astype(BF16)
    for rows, hb in pre:
        zr = jnp.dot(hb, w_ref[:, o2:], preferred_element_type=F32)
        xr_ref[0, rows, :] = zr[:, :LRU_WIDTH]
        gr_ref[0, rows, :] = _gelu_tanh(zr[:, LRU_WIDTH:])


def _inproj(x, mod, gpre, w, gq, gk, bd, cos, sin, *, rope):
    b, n, d = x.shape
    tm = min(INPROJ_TILE, n)
    nt = n // tm
    widths = (A_Q, A_KV, A_KV, B_Q, B_KV, B_KV, LRU_WIDTH, LRU_WIDTH)
    dtypes = (BF16,) * 6 + (F32, F32)
    row_spec = lambda wd: pl.BlockSpec((1, tm, wd), lambda i, j: (i, j, 0))
    table_spec = pl.BlockSpec((tm, LANES), (lambda i, j: (j, 0)) if rope else (lambda i, j: (0, 0)))
    return pl.pallas_call(
        functools.partial(_inproj_kernel, rope=rope),
        out_shape=[jax.ShapeDtypeStruct((b, n, wd), dt) for wd, dt in zip(widths, dtypes)],
        grid=(b, nt),
        in_specs=[
            row_spec(d),
            pl.BlockSpec((1, 6, d), lambda i, j: (i, 0, 0)),
            _const_spec(gpre.shape),
            _const_spec(w.shape),
            _const_spec(gq.shape),
            _const_spec(gk.shape),
            _const_spec(bd.shape),
            table_spec,
            table_spec,
        ],
        out_specs=[row_spec(wd) for wd in widths],
        compiler_params=_params("parallel", "parallel"),
        name="inproj_rope" if rope else "inproj_ctx",
    )(x, mod, gpre, w, gq, gk, bd, cos, sin)


def _split_heads(qc, low):
    zero = jnp.zeros_like(qc)
    return jnp.where(low, qc, zero), jnp.where(low, zero, qc)


def _stack_heads(q, low):
    parts = []
    for j in range(q.shape[1] // LANES):
        parts.extend(_split_heads(q[:, j * LANES:(j + 1) * LANES], low))
    return jnp.concatenate(parts, axis=0)


def _unstack_heads(o, t, low):
    return jnp.concatenate([jnp.where(low, o[0:t], o[t:2 * t]),
                            jnp.where(low, o[2 * t:3 * t], o[3 * t:4 * t])], axis=-1)


def _with_ones(v):
    return jnp.concatenate([v, jnp.ones_like(v)], axis=-1)


def _chain_sizes(total):
    assert total % ATTN_CHAIN == 0
    half = ATTN_CHAIN // 2
    return (half,) + (ATTN_CHAIN,) * (total // ATTN_CHAIN - 1) + (half,)


def _attend(q, keys, vals, sink_ref):
    tq = q.shape[0]
    has_sink = sink_ref is not None
    low = lax.broadcasted_iota(jnp.int32, (tq, LANES), 1) < HEAD_DIM
    qs_all = _stack_heads(q, low)
    if has_sink:
        sink_all = jnp.concatenate([jnp.full((tq, LANES), sink_ref[h], F32) for h in (0, 2, 1, 3)], axis=0)
    outs, r0 = [], 0
    for rows in _chain_sizes(4 * tq):
        qs = qs_all[r0:r0 + rows]
        scores = [lax.dot_general(qs, k, _NT, preferred_element_type=F32) for k in keys]
        s = scores[0] if len(keys) == 1 else jnp.concatenate(scores, axis=-1)
        m = jnp.max(s, axis=-1, keepdims=True)
        if has_sink:
            sink = sink_all[r0:r0 + rows]
            m_wide = jnp.maximum(jnp.broadcast_to(m, sink.shape), sink)
            p = jnp.exp(s - jnp.concatenate([m_wide] * (s.shape[1] // LANES), axis=-1)).astype(BF16)
        else:
            p = jnp.exp(s - m).astype(BF16)
        acc, c0 = None, 0
        for k, v in zip(keys, vals):
            pv = jnp.dot(p[:, c0:c0 + k.shape[0]], v, preferred_element_type=F32)
            acc = pv if acc is None else acc + pv
            c0 += k.shape[0]
        denom = acc[:, LANES:]
        if has_sink:
            denom = denom + jnp.exp(sink - m_wide)
        outs.append(acc[:, :LANES] / denom)
        r0 += rows
    return _unstack_heads(jnp.concatenate(outs, axis=0), tq, low)


def _dense_attn_kernel(*refs, n_kv):
    q_ref, kv_refs, o_ref = refs[0], refs[1:1 + 2 * n_kv], refs[-1]
    keys = [kv_refs[2 * i][0] for i in range(n_kv)]
    vals = [_with_ones(kv_refs[2 * i + 1][0]) for i in range(n_kv)]
    o_ref[0] = _attend(q_ref[0], keys, vals, None).astype(o_ref.dtype)


def _ctx_attn_kernel(qa_ref, ka_ref, va_ref, qb_ref, kb_ref, vb_ref, sink_ref, oa_ref, ob_ref):
    for i in range(qa_ref.shape[0]):
        oa_ref[i] = _attend(qa_ref[i], [ka_ref[i]], [_with_ones(va_ref[i])], None).astype(oa_ref.dtype)
        ob_ref[i] = _attend(qb_ref[i], [kb_ref[i]], [_with_ones(vb_ref[i])], sink_ref).astype(ob_ref.dtype)


def _dense_attn(q, kvs, name):
    b, n, w = q.shape
    tq = min(ATTN_Q_TILE, n)
    in_specs = [pl.BlockSpec((1, tq, w), lambda i, j: (i, j, 0))]
    args = [q]
    for k, v in kvs:
        for t in (k, v):
            in_specs.append(pl.BlockSpec((1,) + t.shape[1:], lambda i, j: (i, 0, 0)))
            args.append(t)
    return pl.pallas_call(
        functools.partial(_dense_attn_kernel, n_kv=len(kvs)),
        out_shape=jax.ShapeDtypeStruct((b, n, w), BF16),
        grid=(b, n // tq),
        in_specs=in_specs,
        out_specs=pl.BlockSpec((1, tq, w), lambda i, j: (i, j, 0)),
        compiler_params=_params("parallel", "parallel"),
        name=name,
    )(*args)


def _ctx_attn(qa, ka, va, qb, kb, vb, sink):
    b = qa.shape[0]
    bt = CTX_ATTN_BATCH if b % CTX_ATTN_BATCH == 0 else 1
    spec = lambda t: pl.BlockSpec((bt,) + t.shape[1:], lambda i: (i, 0, 0))
    args = (qa, ka, va, qb, kb, vb)
    return pl.pallas_call(
        _ctx_attn_kernel,
        out_shape=[jax.ShapeDtypeStruct(qa.shape, BF16), jax.ShapeDtypeStruct(qb.shape, BF16)],
        grid=(b // bt,),
        in_specs=[spec(t) for t in args] + [pl.BlockSpec(memory_space=pltpu.SMEM)],
        out_specs=[spec(qa), spec(qb)],
        compiler_params=_params("parallel"),
        name="ctx_attn",
    )(*args, sink)


def _win_attn_kernel(q_ref, k_ref, v_ref, kc_ref, vc_ref, sink_ref, o_ref):
    n = q_ref.shape[1]
    qb = WINDOW
    band = 3 * qb
    rows = WIN_GROUP * qb
    n_iter = n // rows
    low = lax.broadcasted_iota(jnp.int32, (qb, LANES), 1) < HEAD_DIM
    row_minus_col = (lax.broadcasted_iota(jnp.int32, (qb, band), 0)
                     - lax.broadcasted_iota(jnp.int32, (qb, band), 1))

    def band_bias(first_key_offset):
        bias = jnp.where(jnp.abs(row_minus_col + first_key_offset) <= WINDOW, 0.0, NEG_INF)
        return jnp.concatenate([bias.astype(F32)] * 4, axis=0)

    bias_first, bias_mid, bias_last = band_bias(0), band_bias(qb), band_bias(2 * qb)
    kc = kc_ref[0]
    vc = _with_ones(vc_ref[0])
    ones_band = jnp.ones((band, LANES), BF16)
    sink_blk = jnp.concatenate([jnp.full((qb, LANES), sink_ref[h], F32) for h in (0, 2, 1, 3)], axis=0)
    per_chain = WIN_GROUP // WIN_CHAINS
    sink = jnp.concatenate([sink_blk] * per_chain, axis=0)

    def chain(it, t0, q, blocks):
        qs_blk = [_stack_heads(q[g * qb:(g + 1) * qb], low) for g in blocks]
        starts = [pl.multiple_of(jnp.clip(t0 + (g - 1) * qb, 0, n - band), qb) for g in blocks]
        s_ctx = lax.dot_general(jnp.concatenate(qs_blk, axis=0), kc, _NT, preferred_element_type=F32)
        s_band = []
        for i, g in enumerate(blocks):
            s = lax.dot_general(qs_blk[i], k_ref[0, pl.ds(starts[i], band), :], _NT,
                                preferred_element_type=F32)
            bias = bias_mid
            if g == 0:
                bias = jnp.where(it == 0, bias_first, bias)
            if g == WIN_GROUP - 1:
                bias = jnp.where(it == n_iter - 1, bias_last, bias)
            s_band.append(s + bias)
        s = jnp.concatenate([jnp.concatenate(s_band, axis=0), s_ctx], axis=-1)
        m = jnp.maximum(jnp.broadcast_to(jnp.max(s, axis=-1, keepdims=True), sink.shape), sink)
        p = jnp.exp(s - jnp.concatenate([m] * (s.shape[1] // LANES), axis=-1)).astype(BF16)
        acc_band = [jnp.dot(p[i * 4 * qb:(i + 1) * 4 * qb, :band],
                            jnp.concatenate([v_ref[0, pl.ds(starts[i], band), :], ones_band], axis=-1),
                            preferred_element_type=F32) for i in range(len(blocks))]
        acc = jnp.concatenate(acc_band, axis=0) + jnp.dot(p[:, band:], vc, preferred_element_type=F32)
        out = acc[:, :LANES] / (acc[:, LANES:] + jnp.exp(sink - m))
        return [_unstack_heads(out[i * 4 * qb:(i + 1) * 4 * qb], qb, low) for i in range(len(blocks))]

    def body(it, carry):
        t0 = pl.multiple_of(it * rows, rows)
        q = q_ref[0, pl.ds(t0, rows), :]
        outs = []
        for c in range(WIN_CHAINS):
            outs.extend(chain(it, t0, q, range(c * per_chain, (c + 1) * per_chain)))
        o_ref[0, pl.ds(t0, rows), :] = jnp.concatenate(outs, axis=0).astype(o_ref.dtype)
        return carry

    lax.fori_loop(0, n_iter, body, 0)


def _win_attn(q, k, v, kc, vc, sink):
    b, n, w = q.shape
    full = lambda t: pl.BlockSpec((1,) + t.shape[1:], lambda i: (i, 0, 0))
    return pl.pallas_call(
        _win_attn_kernel,
        out_shape=jax.ShapeDtypeStruct((b, n, w), BF16),
        grid=(b,),
        in_specs=[full(q), full(k), full(v), full(kc), full(vc),
                  pl.BlockSpec(memory_space=pltpu.SMEM)],
        out_specs=full(q),
        compiler_params=_params("parallel"),
        name="window_attn",
    )(q, k, v, kc, vc, sink)


def _scan_block(a_scr, u_scr, h_scr, base, col, c_in, row8, reverse):
    order = range(SUBLANES - 1, -1, -1) if reverse else range(SUBLANES)
    a_cum, h_loc = [], []
    for j in order:
        aj = a_scr[col, base + j * SUBLANES:base + (j + 1) * SUBLANES, :]
        uj = u_scr[col, base + j * SUBLANES:base + (j + 1) * SUBLANES, :]
        if a_cum:
            h_loc.append(aj * h_loc[-1] + uj)
            a_cum.append(aj * a_cum[-1])
        else:
            h_loc.append(uj)
            a_cum.append(aj)
    p, q = a_cum[-1], h_loc[-1]
    for s in (1, 2, 4):
        shift = SUBLANES - s if reverse else s
        valid = (row8 < SUBLANES - s) if reverse else (row8 >= s)
        q = jnp.where(valid, p * pltpu.roll(q, shift, 0) + q, q)
        p = jnp.where(valid, p * pltpu.roll(p, shift, 0), p)
    end = q + p * c_in
    if reverse:
        c_grp = jnp.where(row8 < SUBLANES - 1, pltpu.roll(end, SUBLANES - 1, 0), c_in)
        last = end[0:1]
    else:
        c_grp = jnp.where(row8 >= 1, pltpu.roll(end, 1, 0), c_in)
        last = end[SUBLANES - 1:SUBLANES]
    for idx, j in enumerate(order):
        h_scr[col, pl.ds(base + j, SUBLANES, stride=SUBLANES), :] = h_loc[idx] + a_cum[idx] * c_grp
    return jnp.broadcast_to(last, (SUBLANES, LANES))


def _conv_block_order(src_ref, t0, rows, cw_ref, cb_ref, xt_scr, row8):
    length = src_ref.shape[1]
    blk = SCAN_BLOCK
    n_blk = rows // blk
    n_col = LRU_WIDTH // LANES
    lo = jnp.maximum(t0 - blk, 0)
    hi = jnp.minimum(t0 + rows, length - blk)
    prev = jnp.where(t0 > 0, src_ref[0, pl.ds(pl.multiple_of(lo, blk), blk), :], 0.0)
    nxt = jnp.where(t0 + rows < length, src_ref[0, pl.ds(pl.multiple_of(hi, blk), blk), :], 0.0)
    win = jnp.concatenate([prev, src_ref[0, pl.ds(t0, rows), :], nxt], axis=0)
    for col in range(n_col):
        for i in range(win.shape[0] // SUBLANES):
            b, g = divmod(i, SUBLANES)
            xt_scr[col, pl.ds(b * blk + g, SUBLANES, stride=SUBLANES), :] = (
                win[i * SUBLANES:(i + 1) * SUBLANES, col * LANES:(col + 1) * LANES])

    def x_at(b, j, col):
        base = (b + 1) * blk + j * SUBLANES
        return xt_scr[col, base:base + SUBLANES, :]

    out_rows = []
    for b in range(n_blk):
        per_j = [[] for _ in range(SUBLANES)]
        for col in range(n_col):
            lanes = slice(col * LANES, (col + 1) * LANES)
            xs = [x_at(b, j, col) for j in range(SUBLANES)]
            before = [jnp.where(row8 >= 1, pltpu.roll(xs[j], 1, 0), pltpu.roll(x_at(b - 1, j, col), 1, 0))
                      for j in (SUBLANES - 2, SUBLANES - 1)]
            after = jnp.where(row8 < SUBLANES - 1, pltpu.roll(xs[0], SUBLANES - 1, 0),
                              pltpu.roll(x_at(b + 1, 0, col), SUBLANES - 1, 0))
            ext = before + xs + [after]
            for j in range(SUBLANES):
                acc = cb_ref[:, lanes]
                for o in range(CONV_W):
                    acc = acc + ext[j + o] * cw_ref[o:o + 1, lanes]
                per_j[j].append(acc)
        out_rows.extend(jnp.concatenate(vs, axis=-1) for vs in per_j)
    return jnp.concatenate(out_rows, axis=0)


def _lru_chunk(src_ref, conv_ref, t0, rows, d, carry, reverse, cw_ref, cb_ref, wg_ref, bg_ref,
               quarter_nsp, xt_scr, a_scr, u_scr, h_scr, row8):
    if reverse:
        xc = conv_ref[pl.ds(t0, rows), :]
    else:
        xc = _conv_block_order(src_ref, t0, rows, cw_ref, cb_ref, xt_scr, row8)
        conv_ref[pl.ds(t0, rows), :] = xc
    xb = xc.astype(BF16)
    half = LRU_WIDTH // 2
    y0 = jnp.dot(xb[:, :half], wg_ref[d, 0], preferred_element_type=F32)
    y1 = jnp.dot(xb[:, half:], wg_ref[d, 1], preferred_element_type=F32)
    ya = jnp.concatenate([y0[:, :half], y1[:, :half]], axis=-1)
    yi = jnp.concatenate([y0[:, half:], y1[:, half:]], axis=-1)
    t_r = jnp.tanh(ya + bg_ref[2 * d:2 * d + 1, :])
    t_i = jnp.tanh(yi + bg_ref[2 * d + 1:2 * d + 2, :])
    t = jnp.tanh(t_r * quarter_nsp + quarter_nsp)
    inv = 1.0 / (1.0 - t)
    a = (1.0 + t) * inv
    neg_t = -t
    root = jnp.where(neg_t > 0.0, neg_t * lax.rsqrt(neg_t), 0.0)
    u = root * inv * ((t_i + 1.0) * xc)
    for col in range(LRU_WIDTH // LANES):
        a_scr[col, 0:rows, :] = a[:, col * LANES:(col + 1) * LANES]
        u_scr[col, 0:rows, :] = u[:, col * LANES:(col + 1) * LANES]
    n_blk = rows // SCAN_BLOCK
    blocks = range(n_blk - 1, -1, -1) if reverse else range(n_blk)
    carry = list(carry)
    for blk in blocks:
        for col in range(LRU_WIDTH // LANES):
            carry[col] = _scan_block(a_scr, u_scr, h_scr, blk * SCAN_BLOCK, col, carry[col],
                                     row8, reverse)
    return tuple(carry)


def _lru_kernel(xc_ref, gc_ref, x_ref, g_ref, cw_ref, cb_ref, wg_ref, bg_ref, lam_ref, *rest,
                with_ctx):
    if with_ctx:
        o_ref, oc_ref, xconv, xcconv, xt_scr, a_scr, u_scr, h_scr, hf, hfc = rest
    else:
        o_ref, xconv, xcconv, xt_scr, a_scr, u_scr, h_scr, hf = rest
    n = x_ref.shape[1]
    m = xc_ref.shape[1]

    neg_lam = -lam_ref[...]
    softplus = jnp.maximum(neg_lam, 0.0) + jnp.log1p(jnp.exp(-jnp.abs(neg_lam)))
    quarter_nsp = (-0.25 * LRU_C) * softplus
    row8 = lax.broadcasted_iota(jnp.int32, (SUBLANES, LANES), 0)
    zero_carry = tuple(jnp.zeros((SUBLANES, LANES), F32) for _ in range(LRU_WIDTH // LANES))
    n_chunks = n // LRU_CHUNK

    def chunk(ctx_part, t0, rows, d, carry):
        src_ref, conv_ref = (xc_ref, xcconv) if ctx_part else (x_ref, xconv)
        return _lru_chunk(src_ref, conv_ref, t0, rows, d, carry, d == 1, cw_ref, cb_ref, wg_ref, bg_ref,
                          quarter_nsp[d:d + 1, :], xt_scr, a_scr, u_scr, h_scr, row8)

    def read_h(rows):
        return jnp.concatenate([h_scr[col, 0:rows, :] for col in range(LRU_WIDTH // LANES)], axis=-1)

    carry = chunk(True, 0, m, 0, zero_carry)
    if with_ctx:
        hfc[...] = read_h(m)

    def fwd_body(ci, carry):
        t0 = pl.multiple_of(ci * LRU_CHUNK, LRU_CHUNK)
        carry = chunk(False, t0, LRU_CHUNK, 0, carry)
        hf[pl.ds(t0, LRU_CHUNK), :] = read_h(LRU_CHUNK)
        return carry

    lax.fori_loop(0, n_chunks, fwd_body, carry)

    carry = chunk(True, 0, m, 1, zero_carry)
    if with_ctx:
        oc_ref[0] = ((hfc[...] + read_h(m)) * gc_ref[0]).astype(oc_ref.dtype)

    def bwd_body(ci, carry):
        t0 = pl.multiple_of((n_chunks - 1 - ci) * LRU_CHUNK, LRU_CHUNK)
        carry = chunk(False, t0, LRU_CHUNK, 1, carry)
        hf[pl.ds(t0, LRU_CHUNK), :] = hf[pl.ds(t0, LRU_CHUNK), :] + read_h(LRU_CHUNK)
        return carry

    lax.fori_loop(0, n_chunks, bwd_body, carry)

    def gate_body(ci, _):
        t0 = pl.multiple_of(ci * LRU_CHUNK, LRU_CHUNK)
        gate = g_ref[0, pl.ds(t0, LRU_CHUNK), :]
        o_ref[0, pl.ds(t0, LRU_CHUNK), :] = (hf[pl.ds(t0, LRU_CHUNK), :] * gate).astype(o_ref.dtype)
        return 0

    lax.fori_loop(0, n_chunks, gate_body, 0)


def _lru(xc, gc, x, g, cw, cb, wg, bg, lam, *, with_ctx):
    b, n, w = x.shape
    m = xc.shape[1]
    full = lambda t: pl.BlockSpec((1,) + t.shape[1:], lambda i: (i, 0, 0))
    out_shape = [jax.ShapeDtypeStruct((b, n, w), BF16)]
    out_specs = [pl.BlockSpec((1, n, w), lambda i: (i, 0, 0))]
    scratch = [pltpu.VMEM((n, w), F32), pltpu.VMEM((m, w), F32),
               pltpu.VMEM((w // LANES, LRU_CHUNK + 2 * SCAN_BLOCK, LANES), F32),
               *[pltpu.VMEM((w // LANES, LRU_CHUNK, LANES), F32) for _ in range(3)],
               pltpu.VMEM((n, w), F32)]
    if with_ctx:
        out_shape.append(jax.ShapeDtypeStruct((b, m, w), BF16))
        out_specs.append(pl.BlockSpec((1, m, w), lambda i: (i, 0, 0)))
        scratch.append(pltpu.VMEM((m, w), F32))
    outs = pl.pallas_call(
        functools.partial(_lru_kernel, with_ctx=with_ctx),
        out_shape=out_shape,
        grid=(b,),
        in_specs=[full(xc), full(gc), full(x), full(g), _const_spec(cw.shape), _const_spec(cb.shape),
                  _const_spec(wg.shape), _const_spec(bg.shape), _const_spec(lam.shape)],
        out_specs=out_specs,
        scratch_shapes=scratch,
        compiler_params=_params("parallel"),
        name="rglru_ctx" if with_ctx else "rglru",
    )(xc, gc, x, g, cw, cb, wg, bg, lam)
    return (outs[0], outs[1]) if with_ctx else (outs[0], None)


def _post_kernel(fa_ref, fb_ref, fc_ref, x_ref, mod_ref, wo_ref, gpost_ref, gpre_ref, w1_ref, w2_ref,
                 gpm_ref, o_ref):
    mod = mod_ref[0]
    tm = x_ref.shape[1]
    sub = min(POST_SUB_TILE, tm)
    d_ff = w1_ref.shape[1]
    ff_chunk = d_ff // 4
    pre = []
    for r0 in range(0, tm, sub):
        rows = slice(r0, r0 + sub)
        y = (jnp.dot(fa_ref[0, rows, :], wo_ref[0:A_Q, :], preferred_element_type=F32)
             + jnp.dot(fb_ref[0, rows, :], wo_ref[A_Q:A_Q + B_Q, :], preferred_element_type=F32)
             + jnp.dot(fc_ref[0, rows, :], wo_ref[A_Q + B_Q:, :], preferred_element_type=F32))
        x1 = x_ref[0, rows, :] + mod[2:3] * _rms(y, gpost_ref[...])
        h2 = (_rms(x1, gpre_ref[...]) * (1.0 + mod[4:5]) + mod[3:4]).astype(BF16)
        pre.append((rows, x1, h2))
    for rows, x1, h2 in pre:
        acc = None
        for k in range(d_ff // ff_chunk):
            hk = jnp.dot(h2, w1_ref[:, k * ff_chunk:(k + 1) * ff_chunk], preferred_element_type=F32)
            hk = jnp.square(jnp.maximum(hk, 0.0)).astype(BF16)
            part = jnp.dot(hk, w2_ref[k * ff_chunk:(k + 1) * ff_chunk, :], preferred_element_type=F32)
            acc = part if acc is None else acc + part
        o_ref[0, rows, :] = x1 + mod[5:6] * _rms(acc, gpm_ref[...])


def _post(fa, fb, fc, x, mod, wo, gpost, gpre, w1, w2, gpm, name):
    b, n, d = x.shape
    tm = min(ROW_TILE, n)
    row_spec = lambda wd: pl.BlockSpec((1, tm, wd), lambda i, j: (i, j, 0))
    return pl.pallas_call(
        _post_kernel,
        out_shape=jax.ShapeDtypeStruct((b, n, d), F32),
        grid=(b, n // tm),
        in_specs=[row_spec(fa.shape[2]), row_spec(fb.shape[2]), row_spec(fc.shape[2]), row_spec(d),
                  pl.BlockSpec((1, 6, d), lambda i, j: (i, 0, 0)),
                  _const_spec(wo.shape), _const_spec(gpost.shape), _const_spec(gpre.shape),
                  _const_spec(w1.shape), _const_spec(w2.shape), _const_spec(gpm.shape)],
        out_specs=row_spec(d),
        compiler_params=_params("parallel", "parallel"),
        name=name,
    )(fa, fb, fc, x, mod, wo, gpost, gpre, w1, w2, gpm)


def _rope_tables(n):
    rows = n // GRID_W
    row = jnp.repeat(jnp.arange(rows, dtype=F32), GRID_W)
    col = jnp.tile(jnp.arange(GRID_W, dtype=F32), rows)
    half = HEAD_DIM // 2
    inv_freq = ROPE_BASE ** (-jnp.arange(0, half, 2, dtype=F32) / half)
    ang_r = row[:, None] * inv_freq
    ang_c = col[:, None] * inv_freq
    cr, sr, cc, sc = jnp.cos(ang_r), jnp.sin(ang_r), jnp.cos(ang_c), jnp.sin(ang_c)
    cos = jnp.concatenate([cr, cr, cc, cc], axis=-1)
    sin = jnp.concatenate([-sr, sr, -sc, sc], axis=-1)
    reps = LANES // HEAD_DIM
    return jnp.tile(cos, (1, reps)), jnp.tile(sin, (1, reps))


def _permute_heads(w, axis, bases):
    pieces, pos = [], 0
    for base in bases:
        pieces.append(lax.slice_in_dim(w, pos, base, axis=axis))
        for h in (0, 2, 1, 3):
            pieces.append(lax.slice_in_dim(w, base + h * HEAD_DIM, base + (h + 1) * HEAD_DIM, axis=axis))
        pos = base + 4 * HEAD_DIM
    pieces.append(lax.slice_in_dim(w, pos, w.shape[axis], axis=axis))
    return jnp.concatenate([p for p in pieces if p.shape[axis]], axis=axis)


def _block_diag(w):
    nb, hb, _ = w.shape
    per = nb // 2
    eye = jnp.eye(per, dtype=w.dtype)
    blocks = w.reshape(2, per, hb, 1, hb) * eye[None, :, None, :, None]
    return blocks.reshape(2, per * hb, per * hb)


def kernel(x, c, ctx, c_ctx, w_mod, b_mod, g_pre_mix, g_post_mix, g_pre_mlp, g_post_mlp, w_in, g_q_a,
           g_k_a, sink_b, conv_w, conv_b, lru_w_a, lru_b_a, lru_w_i, lru_b_i, lru_lambda, w_out,
           w_mlp_in, w_mlp_out):
    b, n, d = x.shape
    depth = w_mod.shape[0]
    cos, sin = _rope_tables(n)
    seg = jnp.arange(A_Q) // HEAD_DIM
    bd = jnp.where(seg[:, None] == seg[None, :], 1.0 / HEAD_DIM, 0.0).astype(BF16)

    mod_rows = -(-(b + 1) // SUBLANES) * SUBLANES
    cc = jnp.zeros((mod_rows, d), F32).at[:b].set(c).at[b].set(c_ctx)

    mod_all = _modulation(cc, w_mod, b_mod[:, None, :])
    m = ctx.shape[1]
    flat = lambda t: t.reshape(1, b * m, t.shape[-1])
    unflat = lambda t: t.reshape(b, m, t.shape[-1])

    for l in range(depth):
        last = l == depth - 1
        mod_lat = mod_all[l, :b].reshape(b, 6, d)
        mod_ctx = mod_all[l, b].reshape(1, 6, d)

        w_in_l = _permute_heads(w_in[l], 1, (0, A_Q + 2 * A_KV)).astype(BF16)
        gq = jnp.tile(g_q_a[l], A_Q_HEADS)[None, :]
        gk = jnp.tile(g_k_a[l], A_KV_HEADS)[None, :]
        gpre = g_pre_mix[l][None, :]
        qa, ka, va, qb, kb, vb, xr, gr = _inproj(x, mod_lat, gpre, w_in_l, gq, gk, bd, cos, sin, rope=True)
        qa_c, ka_c, va_c, qb_c, kb_c, vb_c, xr_c, gr_c = map(unflat, _inproj(
            flat(ctx), mod_ctx, gpre, w_in_l, gq, gk, bd, cos, sin, rope=False))

        feat_a = _dense_attn(qa, [(ka_c, va_c), (ka, va)], "global_attn")
        feat_b = _win_attn(qb, kb, vb, kb_c, vb_c, sink_b[l])

        wg = (0.5 * jnp.stack([jnp.concatenate([_block_diag(lru_w_a[l, dd]), _block_diag(lru_w_i[l, dd])],
                                               axis=-1) for dd in range(2)])).astype(BF16)
        bg = 0.5 * jnp.stack([lru_b_a[l, 0], lru_b_i[l, 0], lru_b_a[l, 1], lru_b_i[l, 1]])
        feat_c, feat_cc = _lru(xr_c, gr_c, xr, gr, conv_w[l], conv_b[l][None, :], wg, bg, lru_lambda[l],
                               with_ctx=not last)

        wo = _permute_heads(w_out[l], 0, (0, A_Q)).astype(BF16)
        post_w = (wo, g_post_mix[l][None, :], g_pre_mlp[l][None, :], w_mlp_in[l].astype(BF16),
                  w_mlp_out[l].astype(BF16), g_post_mlp[l][None, :])
        x = _post(feat_a, feat_b, feat_c, x, mod_lat, *post_w, name="post_mlp")
        if not last:
            feat_ac, feat_bc = _ctx_attn(qa_c, ka_c, va_c, qb_c, kb_c, vb_c, sink_b[l])
            ctx = unflat(_post(flat(feat_ac), flat(feat_bc), flat(feat_cc), flat(ctx), mod_ctx, *post_w,
                               name="post_mlp_ctx"))
    return x
```

```python
import functools

import jax
import jax.numpy as jnp
from jax import lax
from jax.experimental import pallas as pl
from jax.experimental.pallas import tpu as pltpu

F32 = jnp.float32
BF16 = jnp.bfloat16

GRID_W = 64
HEAD_DIM = 64
A_Q_HEADS = 4
A_KV_HEADS = 2
B_Q_HEADS = 4
B_KV_HEADS = 2
WINDOW = 128
LRU_WIDTH = 512
CONV_W = 4
CONV_LEFT = CONV_W // 2
LRU_C = 8.0
ROPE_BASE = 10000.0
EPS = 1e-6
NEG_INF = -1e30
N_MOD = 6

A_Q = A_Q_HEADS * HEAD_DIM
A_KV = A_KV_HEADS * HEAD_DIM
B_Q = B_Q_HEADS * HEAD_DIM
B_KV = B_KV_HEADS * HEAD_DIM
Q_SCALE = HEAD_DIM ** -0.5
ROT_HALF = HEAD_DIM // 2
ROT_PAIR = HEAD_DIM // 4

LANES = 128
SUBLANES = 8
VMEM_LIMIT = 56 * 1024 * 1024

MOD_COL_TILES = 4
FF_CHUNKS = 4
ROW_TILE = 1024
INPROJ_TILE = 1024
SUB_TILE = 256
POST_SUB_TILE = 512
ATTN_Q_TILE = 1024
CTX_ATTN_BATCH = 4
ATTN_CHAIN = 512
WIN_GROUP = 16
WIN_CHAINS = 4
LRU_CHUNK = 1024
SCAN_BLOCK = SUBLANES * SUBLANES

_NT = (((1,), (1,)), ((), ()))

assert A_Q == B_Q == 2 * LANES and A_KV == B_KV == LANES and LRU_WIDTH % (2 * LANES) == 0


def _const_spec(shape):
    return pl.BlockSpec(shape, lambda *_: (0,) * len(shape), pipeline_mode=pl.Buffered(1))


def _params(*sem):
    return pltpu.CompilerParams(dimension_semantics=sem, vmem_limit_bytes=VMEM_LIMIT)


def _rms(x, g):
    return x * lax.rsqrt(jnp.mean(x * x, axis=-1, keepdims=True) + EPS) * g


def _gelu_tanh(x):
    c = (2.0 / jnp.pi) ** 0.5
    half_x = 0.5 * x
    return half_x + half_x * jnp.tanh(x * (c + (c * 0.044715) * (x * x)))


def _mod_kernel(c_ref, w_ref, b_ref, o_ref):
    c = c_ref[...]
    act = (c * jax.nn.sigmoid(c)).astype(BF16)
    o_ref[...] = jnp.dot(act, w_ref[...].astype(BF16), preferred_element_type=F32) + b_ref[...]


def _modulation(cc, w, b):
    rows, d = cc.shape
    depth, _, n_out = w.shape
    assert n_out % (MOD_COL_TILES * LANES) == 0
    tn = n_out // MOD_COL_TILES
    return pl.pallas_call(
        _mod_kernel,
        out_shape=jax.ShapeDtypeStruct((depth, rows, n_out), F32),
        grid=(depth, n_out // tn),
        in_specs=[
            pl.BlockSpec((rows, d), lambda l, j: (0, 0)),
            pl.BlockSpec((None, d, tn), lambda l, j: (l, 0, j)),
            pl.BlockSpec((None, 1, tn), lambda l, j: (l, 0, j)),
        ],
        out_specs=pl.BlockSpec((None, rows, tn), lambda l, j: (l, 0, j)),
        compiler_params=_params("arbitrary", "arbitrary"),
        name="modulation",
    )(cc, w, b)


def _head_rms(t, g, bd):
    ms = jnp.dot((t * t).astype(BF16), bd, preferred_element_type=F32)
    return t * lax.rsqrt(ms + EPS) * g


def _rope(t, cos, sin, low_half):
    outs = []
    for j in range(t.shape[1] // LANES):
        tc = t[:, j * LANES:(j + 1) * LANES]
        partner = jnp.where(low_half, pltpu.roll(tc, LANES - ROT_PAIR, 1), pltpu.roll(tc, ROT_PAIR, 1))
        outs.append(tc * cos + partner * sin)
    return outs[0] if len(outs) == 1 else jnp.concatenate(outs, axis=-1)


def _inproj_kernel(x_ref, mod_ref, gpre_ref, w_ref, gq_ref, gk_ref, bd_ref, cos_ref, sin_ref,
                   qa_ref, ka_ref, va_ref, qb_ref, kb_ref, vb_ref, xr_ref, gr_ref, *, rope):
    mod = mod_ref[0]
    bd = bd_ref[...]
    tm = x_ref.shape[1]
    sub = min(SUB_TILE, tm)
    pre = []
    for r0 in range(0, tm, sub):
        rows = slice(r0, r0 + sub)
        h = _rms(x_ref[0, rows, :], gpre_ref[...]) * (1.0 + mod[1:2]) + mod[0:1]
        pre.append((rows, h.astype(BF16)))
    o2 = A_Q + 2 * A_KV + B_Q + 2 * B_KV
    for rows, hb in pre:
        z = jnp.dot(hb, w_ref[:, :o2], preferred_element_type=F32)
        qa = _head_rms(z[:, 0:A_Q], gq_ref[...], bd)
        ka = _head_rms(z[:, A_Q:A_Q + A_KV], gk_ref[...], bd[:A_KV, :A_KV])
        o = A_Q + 2 * A_KV
        qb = z[:, o:o + B_Q]
        kb = z[:, o + B_Q:o + B_Q + B_KV]
        if rope:
            cos = cos_ref[rows, :]
            sin = sin_ref[rows, :]
            lane = lax.broadcasted_iota(jnp.int32, cos.shape, 1)
            low_half = (lane % ROT_HALF) < ROT_PAIR
            qa = _rope(qa, cos, sin, low_half)
            ka = _rope(ka, cos, sin, low_half)
            qb = _rope(qb, cos, sin, low_half)
            kb = _rope(kb, cos, sin, low_half)
        qa_ref[0, rows, :] = (qa * Q_SCALE).astype(BF16)
        ka_ref[0, rows, :] = ka.astype(BF16)
        va_ref[0, rows, :] = z[:, A_Q + A_KV:o].astype(BF16)
        qb_ref[0, rows, :] = (qb * Q_SCALE).astype(BF16)
        kb_ref[0, rows, :] = kb.astype(BF16)
        vb_ref[0, rows, :] = z[:, o + B_Q + B_KV:o + B_Q + 2 * B_KV].astype(BF16)
    for rows, hb in pre:
        zr = jnp.dot(hb, w_ref[:, o2:], preferred_element_type=F32)
        xr_ref[0, rows, :] = zr[:, :LRU_WIDTH]
        gr_ref[0, rows, :] = _gelu_tanh(zr[:, LRU_WIDTH:])


def _inproj(x, mod, gpre, w, gq, gk, bd, cos, sin, *, rope):
    b, n, d = x.shape
    tm = min(INPROJ_TILE, n)
    assert n % tm == 0 and tm % min(SUB_TILE, tm) == 0
    assert w.shape == (d, A_Q + 2 * A_KV + B_Q + 2 * B_KV + 2 * LRU_WIDTH)
    nt = n // tm
    widths = (A_Q, A_KV, A_KV, B_Q, B_KV, B_KV, LRU_WIDTH, LRU_WIDTH)
    dtypes = (BF16,) * 6 + (F32, F32)
    row_spec = lambda wd: pl.BlockSpec((1, tm, wd), lambda i, j: (i, j, 0))
    table_spec = pl.BlockSpec((tm, LANES), (lambda i, j: (j, 0)) if rope else (lambda i, j: (0, 0)))
    return pl.pallas_call(
        functools.partial(_inproj_kernel, rope=rope),
        out_shape=[jax.ShapeDtypeStruct((b, n, wd), dt) for wd, dt in zip(widths, dtypes)],
        grid=(b, nt),
        in_specs=[
            row_spec(d),
            pl.BlockSpec((1, N_MOD, d), lambda i, j: (i, 0, 0)),
            _const_spec(gpre.shape),
            _const_spec(w.shape),
            _const_spec(gq.shape),
            _const_spec(gk.shape),
            _const_spec(bd.shape),
            table_spec,
            table_spec,
        ],
        out_specs=[row_spec(wd) for wd in widths],
        compiler_params=_params("parallel", "parallel"),
        name="inproj_rope" if rope else "inproj_ctx",
    )(x, mod, gpre, w, gq, gk, bd, cos, sin)


def _split_heads(qc, low):
    zero = jnp.zeros_like(qc)
    return jnp.where(low, qc, zero), jnp.where(low, zero, qc)


def _stack_heads(q, low):
    parts = []
    for j in range(q.shape[1] // LANES):
        parts.extend(_split_heads(q[:, j * LANES:(j + 1) * LANES], low))
    return jnp.concatenate(parts, axis=0)


def _unstack_heads(o, t, low):
    return jnp.concatenate([jnp.where(low, o[0:t], o[t:2 * t]),
                            jnp.where(low, o[2 * t:3 * t], o[3 * t:4 * t])], axis=-1)


def _with_ones(v):
    return jnp.concatenate([v, jnp.ones_like(v)], axis=-1)


def _chain_sizes(total):
    assert total % ATTN_CHAIN == 0
    half = ATTN_CHAIN // 2
    return (half,) + (ATTN_CHAIN,) * (total // ATTN_CHAIN - 1) + (half,)


def _attend(q, keys, vals, sink_ref):
    tq = q.shape[0]
    has_sink = sink_ref is not None
    low = lax.broadcasted_iota(jnp.int32, (tq, LANES), 1) < HEAD_DIM
    qs_all = _stack_heads(q, low)
    if has_sink:
        sink_all = jnp.concatenate([jnp.full((tq, LANES), sink_ref[h], F32) for h in (0, 2, 1, 3)], axis=0)
    outs, r0 = [], 0
    for rows in _chain_sizes(4 * tq):
        qs = qs_all[r0:r0 + rows]
        scores = [lax.dot_general(qs, k, _NT, preferred_element_type=F32) for k in keys]
        s = scores[0] if len(keys) == 1 else jnp.concatenate(scores, axis=-1)
        m = jnp.max(s, axis=-1, keepdims=True)
        if has_sink:
            sink = sink_all[r0:r0 + rows]
            m_wide = jnp.maximum(jnp.broadcast_to(m, sink.shape), sink)
            p = jnp.exp(s - jnp.concatenate([m_wide] * (s.shape[1] // LANES), axis=-1)).astype(BF16)
        else:
            p = jnp.exp(s - m).astype(BF16)
        acc, c0 = None, 0
        for k, v in zip(keys, vals):
            pv = jnp.dot(p[:, c0:c0 + k.shape[0]], v, preferred_element_type=F32)
            acc = pv if acc is None else acc + pv
            c0 += k.shape[0]
        denom = acc[:, LANES:]
        if has_sink:
            denom = denom + jnp.exp(sink - m_wide)
        outs.append(acc[:, :LANES] / denom)
        r0 += rows
    return _unstack_heads(jnp.concatenate(outs, axis=0), tq, low)


def _dense_attn_kernel(*refs, n_kv):
    q_ref, kv_refs, o_ref = refs[0], refs[1:1 + 2 * n_kv], refs[-1]
    keys = [kv_refs[2 * i][0] for i in range(n_kv)]
    vals = [_with_ones(kv_refs[2 * i + 1][0]) for i in range(n_kv)]
    o_ref[0] = _attend(q_ref[0], keys, vals, None).astype(o_ref.dtype)


def _ctx_attn_kernel(qa_ref, ka_ref, va_ref, qb_ref, kb_ref, vb_ref, sink_ref, oa_ref, ob_ref):
    for i in range(qa_ref.shape[0]):
        oa_ref[i] = _attend(qa_ref[i], [ka_ref[i]], [_with_ones(va_ref[i])], None).astype(oa_ref.dtype)
        ob_ref[i] = _attend(qb_ref[i], [kb_ref[i]], [_with_ones(vb_ref[i])], sink_ref).astype(ob_ref.dtype)


def _dense_attn(q, kvs, name):
    b, n, w = q.shape
    tq = min(ATTN_Q_TILE, n)
    assert n % tq == 0
    in_specs = [pl.BlockSpec((1, tq, w), lambda i, j: (i, j, 0))]
    args = [q]
    for k, v in kvs:
        for t in (k, v):
            in_specs.append(pl.BlockSpec((1,) + t.shape[1:], lambda i, j: (i, 0, 0)))
            args.append(t)
    return pl.pallas_call(
        functools.partial(_dense_attn_kernel, n_kv=len(kvs)),
        out_shape=jax.ShapeDtypeStruct((b, n, w), BF16),
        grid=(b, n // tq),
        in_specs=in_specs,
        out_specs=pl.BlockSpec((1, tq, w), lambda i, j: (i, j, 0)),
        compiler_params=_params("parallel", "parallel"),
        name=name,
    )(*args)


def _ctx_attn(qa, ka, va, qb, kb, vb, sink):
    b = qa.shape[0]
    bt = CTX_ATTN_BATCH if b % CTX_ATTN_BATCH == 0 else 1
    spec = lambda t: pl.BlockSpec((bt,) + t.shape[1:], lambda i: (i, 0, 0))
    args = (qa, ka, va, qb, kb, vb)
    return pl.pallas_call(
        _ctx_attn_kernel,
        out_shape=[jax.ShapeDtypeStruct(qa.shape, BF16), jax.ShapeDtypeStruct(qb.shape, BF16)],
        grid=(b // bt,),
        in_specs=[spec(t) for t in args] + [pl.BlockSpec(memory_space=pltpu.SMEM)],
        out_specs=[spec(qa), spec(qb)],
        compiler_params=_params("parallel"),
        name="ctx_attn",
    )(*args, sink)


def _win_attn_kernel(q_ref, k_ref, v_ref, kc_ref, vc_ref, sink_ref, o_ref):
    n = q_ref.shape[1]
    qb = WINDOW
    band = 3 * qb
    rows = WIN_GROUP * qb
    n_iter = n // rows
    low = lax.broadcasted_iota(jnp.int32, (qb, LANES), 1) < HEAD_DIM
    row_minus_col = (lax.broadcasted_iota(jnp.int32, (qb, band), 0)
                     - lax.broadcasted_iota(jnp.int32, (qb, band), 1))

    def band_bias(first_key_offset):
        bias = jnp.where(jnp.abs(row_minus_col + first_key_offset) <= WINDOW, 0.0, NEG_INF)
        return jnp.concatenate([bias.astype(F32)] * 4, axis=0)

    bias_first, bias_mid, bias_last = band_bias(0), band_bias(qb), band_bias(2 * qb)
    kc = kc_ref[0]
    vc = _with_ones(vc_ref[0])
    ones_band = jnp.ones((band, LANES), BF16)
    sink_blk = jnp.concatenate([jnp.full((qb, LANES), sink_ref[h], F32) for h in (0, 2, 1, 3)], axis=0)
    per_chain = WIN_GROUP // WIN_CHAINS
    sink = jnp.concatenate([sink_blk] * per_chain, axis=0)

    def chain(it, t0, q, blocks):
        qs_blk = [_stack_heads(q[g * qb:(g + 1) * qb], low) for g in blocks]
        starts = [pl.multiple_of(jnp.clip(t0 + (g - 1) * qb, 0, n - band), qb) for g in blocks]
        s_ctx = lax.dot_general(jnp.concatenate(qs_blk, axis=0), kc, _NT, preferred_element_type=F32)
        s_band = []
        for i, g in enumerate(blocks):
            s = lax.dot_general(qs_blk[i], k_ref[0, pl.ds(starts[i], band), :], _NT,
                                preferred_element_type=F32)
            bias = bias_mid
            if g == 0:
                bias = jnp.where(it == 0, bias_first, bias)
            if g == WIN_GROUP - 1:
                bias = jnp.where(it == n_iter - 1, bias_last, bias)
            s_band.append(s + bias)
        s = jnp.concatenate([jnp.concatenate(s_band, axis=0), s_ctx], axis=-1)
        m = jnp.maximum(jnp.broadcast_to(jnp.max(s, axis=-1, keepdims=True), sink.shape), sink)
        p = jnp.exp(s - jnp.concatenate([m] * (s.shape[1] // LANES), axis=-1)).astype(BF16)
        acc_band = [jnp.dot(p[i * 4 * qb:(i + 1) * 4 * qb, :band],
                            jnp.concatenate([v_ref[0, pl.ds(starts[i], band), :], ones_band], axis=-1),
                            preferred_element_type=F32) for i in range(len(blocks))]
        acc = jnp.concatenate(acc_band, axis=0) + jnp.dot(p[:, band:], vc, preferred_element_type=F32)
        out = acc[:, :LANES] / (acc[:, LANES:] + jnp.exp(sink - m))
        return [_unstack_heads(out[i * 4 * qb:(i + 1) * 4 * qb], qb, low) for i in range(len(blocks))]

    def body(it, carry):
        t0 = pl.multiple_of(it * rows, rows)
        q = q_ref[0, pl.ds(t0, rows), :]
        outs = []
        for c in range(WIN_CHAINS):
            outs.extend(chain(it, t0, q, range(c * per_chain, (c + 1) * per_chain)))
        o_ref[0, pl.ds(t0, rows), :] = jnp.concatenate(outs, axis=0).astype(o_ref.dtype)
        return carry

    lax.fori_loop(0, n_iter, body, 0)


def _win_attn(q, k, v, kc, vc, sink):
    b, n, w = q.shape
    assert n % (WIN_GROUP * WINDOW) == 0 and WIN_GROUP % WIN_CHAINS == 0
    full = lambda t: pl.BlockSpec((1,) + t.shape[1:], lambda i: (i, 0, 0))
    return pl.pallas_call(
        _win_attn_kernel,
        out_shape=jax.ShapeDtypeStruct((b, n, w), BF16),
        grid=(b,),
        in_specs=[full(q), full(k), full(v), full(kc), full(vc),
                  pl.BlockSpec(memory_space=pltpu.SMEM)],
        out_specs=full(q),
        compiler_params=_params("parallel"),
        name="window_attn",
    )(q, k, v, kc, vc, sink)


def _scan_block(a_scr, u_scr, h_scr, base, col, c_in, row8, reverse):
    order = range(SUBLANES - 1, -1, -1) if reverse else range(SUBLANES)
    a_cum, h_loc = [], []
    for j in order:
        aj = a_scr[col, base + j * SUBLANES:base + (j + 1) * SUBLANES, :]
        uj = u_scr[col, base + j * SUBLANES:base + (j + 1) * SUBLANES, :]
        if a_cum:
            h_loc.append(aj * h_loc[-1] + uj)
            a_cum.append(aj * a_cum[-1])
        else:
            h_loc.append(uj)
            a_cum.append(aj)
    p, q = a_cum[-1], h_loc[-1]
    for s in (1, 2, 4):
        shift = SUBLANES - s if reverse else s
        valid = (row8 < SUBLANES - s) if reverse else (row8 >= s)
        q = jnp.where(valid, p * pltpu.roll(q, shift, 0) + q, q)
        p = jnp.where(valid, p * pltpu.roll(p, shift, 0), p)
    end = q + p * c_in
    if reverse:
        c_grp = jnp.where(row8 < SUBLANES - 1, pltpu.roll(end, SUBLANES - 1, 0), c_in)
        last = end[0:1]
    else:
        c_grp = jnp.where(row8 >= 1, pltpu.roll(end, 1, 0), c_in)
        last = end[SUBLANES - 1:SUBLANES]
    for idx, j in enumerate(order):
        h_scr[col, pl.ds(base + j, SUBLANES, stride=SUBLANES), :] = h_loc[idx] + a_cum[idx] * c_grp
    return jnp.broadcast_to(last, (SUBLANES, LANES))


def _conv_block_order(src_ref, t0, rows, cw_ref, cb_ref, xt_scr, row8):
    length = src_ref.shape[1]
    blk = SCAN_BLOCK
    n_blk = rows // blk
    n_col = LRU_WIDTH // LANES
    lo = jnp.maximum(t0 - blk, 0)
    hi = jnp.minimum(t0 + rows, length - blk)
    prev = jnp.where(t0 > 0, src_ref[0, pl.ds(pl.multiple_of(lo, blk), blk), :], 0.0)
    nxt = jnp.where(t0 + rows < length, src_ref[0, pl.ds(pl.multiple_of(hi, blk), blk), :], 0.0)
    win = jnp.concatenate([prev, src_ref[0, pl.ds(t0, rows), :], nxt], axis=0)
    for col in range(n_col):
        for i in range(win.shape[0] // SUBLANES):
            b, g = divmod(i, SUBLANES)
            xt_scr[col, pl.ds(b * blk + g, SUBLANES, stride=SUBLANES), :] = (
                win[i * SUBLANES:(i + 1) * SUBLANES, col * LANES:(col + 1) * LANES])

    def x_at(b, j, col):
        base = (b + 1) * blk + j * SUBLANES
        return xt_scr[col, base:base + SUBLANES, :]

    out_rows = []
    for b in range(n_blk):
        per_j = [[] for _ in range(SUBLANES)]
        for col in range(n_col):
            lanes = slice(col * LANES, (col + 1) * LANES)
            xs = [x_at(b, j, col) for j in range(SUBLANES)]
            before = [jnp.where(row8 >= 1, pltpu.roll(xs[j], 1, 0), pltpu.roll(x_at(b - 1, j, col), 1, 0))
                      for j in (SUBLANES - 2, SUBLANES - 1)]
            after = jnp.where(row8 < SUBLANES - 1, pltpu.roll(xs[0], SUBLANES - 1, 0),
                              pltpu.roll(x_at(b + 1, 0, col), SUBLANES - 1, 0))
            ext = before + xs + [after]
            for j in range(SUBLANES):
                acc = cb_ref[:, lanes]
                for o in range(CONV_W):
                    acc = acc + ext[j + o] * cw_ref[o:o + 1, lanes]
                per_j[j].append(acc)
        out_rows.extend(jnp.concatenate(vs, axis=-1) for vs in per_j)
    return jnp.concatenate(out_rows, axis=0)


def _lru_chunk(src_ref, conv_ref, t0, rows, d, carry, reverse, cw_ref, cb_ref, wg_ref, bg_ref,
               quarter_nsp, xt_scr, a_scr, u_scr, h_scr, row8):
    if reverse:
        xc = conv_ref[pl.ds(t0, rows), :]
    else:
        xc = _conv_block_order(src_ref, t0, rows, cw_ref, cb_ref, xt_scr, row8)
        conv_ref[pl.ds(t0, rows), :] = xc
    xb = xc.astype(BF16)
    half = LRU_WIDTH // 2
    y0 = jnp.dot(xb[:, :half], wg_ref[d, 0], preferred_element_type=F32)
    y1 = jnp.dot(xb[:, half:], wg_ref[d, 1], preferred_element_type=F32)
    ya = jnp.concatenate([y0[:, :half], y1[:, :half]], axis=-1)
    yi = jnp.concatenate([y0[:, half:], y1[:, half:]], axis=-1)
    t_r = jnp.tanh(ya + bg_ref[2 * d:2 * d + 1, :])
    t_i = jnp.tanh(yi + bg_ref[2 * d + 1:2 * d + 2, :])
    t = jnp.tanh(t_r * quarter_nsp + quarter_nsp)
    inv = 1.0 / (1.0 - t)
    a = (1.0 + t) * inv
    neg_t = -t
    root = jnp.where(neg_t > 0.0, neg_t * lax.rsqrt(neg_t), 0.0)
    u = root * inv * ((t_i + 1.0) * xc)
    for col in range(LRU_WIDTH // LANES):
        a_scr[col, 0:rows, :] = a[:, col * LANES:(col + 1) * LANES]
        u_scr[col, 0:rows, :] = u[:, col * LANES:(col + 1) * LANES]
    n_blk = rows // SCAN_BLOCK
    blocks = range(n_blk - 1, -1, -1) if reverse else range(n_blk)
    carry = list(carry)
    for blk in blocks:
        for col in range(LRU_WIDTH // LANES):
            carry[col] = _scan_block(a_scr, u_scr, h_scr, blk * SCAN_BLOCK, col, carry[col],
                                     row8, reverse)
    return tuple(carry)


def _lru_kernel(xc_ref, gc_ref, x_ref, g_ref, cw_ref, cb_ref, wg_ref, bg_ref, lam_ref, *rest,
                with_ctx):
    if with_ctx:
        o_ref, oc_ref, xconv, xcconv, xt_scr, a_scr, u_scr, h_scr, hf, hfc = rest
    else:
        o_ref, xconv, xcconv, xt_scr, a_scr, u_scr, h_scr, hf = rest
    n = x_ref.shape[1]
    m = xc_ref.shape[1]

    neg_lam = -lam_ref[...]
    softplus = jnp.maximum(neg_lam, 0.0) + jnp.log1p(jnp.exp(-jnp.abs(neg_lam)))
    quarter_nsp = (-0.25 * LRU_C) * softplus
    row8 = lax.broadcasted_iota(jnp.int32, (SUBLANES, LANES), 0)
    zero_carry = tuple(jnp.zeros((SUBLANES, LANES), F32) for _ in range(LRU_WIDTH // LANES))
    n_chunks = n // LRU_CHUNK

    def chunk(ctx_part, t0, rows, d, carry):
        src_ref, conv_ref = (xc_ref, xcconv) if ctx_part else (x_ref, xconv)
        return _lru_chunk(src_ref, conv_ref, t0, rows, d, carry, d == 1, cw_ref, cb_ref, wg_ref, bg_ref,
                          quarter_nsp[d:d + 1, :], xt_scr, a_scr, u_scr, h_scr, row8)

    def read_h(rows):
        return jnp.concatenate([h_scr[col, 0:rows, :] for col in range(LRU_WIDTH // LANES)], axis=-1)

    carry = chunk(True, 0, m, 0, zero_carry)
    if with_ctx:
        hfc[...] = read_h(m)

    def fwd_body(ci, carry):
        t0 = pl.multiple_of(ci * LRU_CHUNK, LRU_CHUNK)
        carry = chunk(False, t0, LRU_CHUNK, 0, carry)
        hf[pl.ds(t0, LRU_CHUNK), :] = read_h(LRU_CHUNK)
        return carry

    lax.fori_loop(0, n_chunks, fwd_body, carry)

    carry = chunk(True, 0, m, 1, zero_carry)
    if with_ctx:
        oc_ref[0] = ((hfc[...] + read_h(m)) * gc_ref[0]).astype(oc_ref.dtype)

    def bwd_body(ci, carry):
        t0 = pl.multiple_of((n_chunks - 1 - ci) * LRU_CHUNK, LRU_CHUNK)
        carry = chunk(False, t0, LRU_CHUNK, 1, carry)
        hf[pl.ds(t0, LRU_CHUNK), :] = hf[pl.ds(t0, LRU_CHUNK), :] + read_h(LRU_CHUNK)
        return carry

    lax.fori_loop(0, n_chunks, bwd_body, carry)

    def gate_body(ci, _):
        t0 = pl.multiple_of(ci * LRU_CHUNK, LRU_CHUNK)
        gate = g_ref[0, pl.ds(t0, LRU_CHUNK), :]
        o_ref[0, pl.ds(t0, LRU_CHUNK), :] = (hf[pl.ds(t0, LRU_CHUNK), :] * gate).astype(o_ref.dtype)
        return 0

    lax.fori_loop(0, n_chunks, gate_body, 0)


def _lru(xc, gc, x, g, cw, cb, wg, bg, lam, *, with_ctx):
    b, n, w = x.shape
    m = xc.shape[1]
    assert n % LRU_CHUNK == 0 and m % SCAN_BLOCK == 0 and m <= LRU_CHUNK and w == LRU_WIDTH
    full = lambda t: pl.BlockSpec((1,) + t.shape[1:], lambda i: (i, 0, 0))
    out_shape = [jax.ShapeDtypeStruct((b, n, w), BF16)]
    out_specs = [pl.BlockSpec((1, n, w), lambda i: (i, 0, 0))]
    scratch = [pltpu.VMEM((n, w), F32), pltpu.VMEM((m, w), F32),
               pltpu.VMEM((w // LANES, LRU_CHUNK + 2 * SCAN_BLOCK, LANES), F32),
               *[pltpu.VMEM((w // LANES, LRU_CHUNK, LANES), F32) for _ in range(3)],
               pltpu.VMEM((n, w), F32)]
    if with_ctx:
        out_shape.append(jax.ShapeDtypeStruct((b, m, w), BF16))
        out_specs.append(pl.BlockSpec((1, m, w), lambda i: (i, 0, 0)))
        scratch.append(pltpu.VMEM((m, w), F32))
    outs = pl.pallas_call(
        functools.partial(_lru_kernel, with_ctx=with_ctx),
        out_shape=out_shape,
        grid=(b,),
        in_specs=[full(xc), full(gc), full(x), full(g), _const_spec(cw.shape), _const_spec(cb.shape),
                  _const_spec(wg.shape), _const_spec(bg.shape), _const_spec(lam.shape)],
        out_specs=out_specs,
        scratch_shapes=scratch,
        compiler_params=_params("parallel"),
        name="rglru_ctx" if with_ctx else "rglru",
    )(xc, gc, x, g, cw, cb, wg, bg, lam)
    return (outs[0], outs[1]) if with_ctx else (outs[0], None)


def _post_kernel(fa_ref, fb_ref, fc_ref, x_ref, mod_ref, wo_ref, gpost_ref, gpre_ref, w1_ref, w2_ref,
                 gpm_ref, o_ref):
    mod = mod_ref[0]
    tm = x_ref.shape[1]
    sub = min(POST_SUB_TILE, tm)
    d_ff = w1_ref.shape[1]
    ff_chunk = d_ff // FF_CHUNKS
    pre = []
    for r0 in range(0, tm, sub):
        rows = slice(r0, r0 + sub)
        y = (jnp.dot(fa_ref[0, rows, :], wo_ref[0:A_Q, :], preferred_element_type=F32)
             + jnp.dot(fb_ref[0, rows, :], wo_ref[A_Q:A_Q + B_Q, :], preferred_element_type=F32)
             + jnp.dot(fc_ref[0, rows, :], wo_ref[A_Q + B_Q:, :], preferred_element_type=F32))
        x1 = x_ref[0, rows, :] + mod[2:3] * _rms(y, gpost_ref[...])
        h2 = (_rms(x1, gpre_ref[...]) * (1.0 + mod[4:5]) + mod[3:4]).astype(BF16)
        pre.append((rows, x1, h2))
    for rows, x1, h2 in pre:
        acc = None
        for k in range(FF_CHUNKS):
            hk = jnp.dot(h2, w1_ref[:, k * ff_chunk:(k + 1) * ff_chunk], preferred_element_type=F32)
            hk = jnp.square(jnp.maximum(hk, 0.0)).astype(BF16)
            part = jnp.dot(hk, w2_ref[k * ff_chunk:(k + 1) * ff_chunk, :], preferred_element_type=F32)
            acc = part if acc is None else acc + part
        o_ref[0, rows, :] = x1 + mod[5:6] * _rms(acc, gpm_ref[...])


def _post(fa, fb, fc, x, mod, wo, gpost, gpre, w1, w2, gpm, name):
    b, n, d = x.shape
    tm = min(ROW_TILE, n)
    assert n % tm == 0 and tm % min(POST_SUB_TILE, tm) == 0 and w1.shape[1] % (FF_CHUNKS * LANES) == 0
    row_spec = lambda wd: pl.BlockSpec((1, tm, wd), lambda i, j: (i, j, 0))
    return pl.pallas_call(
        _post_kernel,
        out_shape=jax.ShapeDtypeStruct((b, n, d), F32),
        grid=(b, n // tm),
        in_specs=[row_spec(fa.shape[2]), row_spec(fb.shape[2]), row_spec(fc.shape[2]), row_spec(d),
                  pl.BlockSpec((1, N_MOD, d), lambda i, j: (i, 0, 0)),
                  _const_spec(wo.shape), _const_spec(gpost.shape), _const_spec(gpre.shape),
                  _const_spec(w1.shape), _const_spec(w2.shape), _const_spec(gpm.shape)],
        out_specs=row_spec(d),
        compiler_params=_params("parallel", "parallel"),
        name=name,
    )(fa, fb, fc, x, mod, wo, gpost, gpre, w1, w2, gpm)


def _rope_tables(n):
    rows = n // GRID_W
    row = jnp.repeat(jnp.arange(rows, dtype=F32), GRID_W)
    col = jnp.tile(jnp.arange(GRID_W, dtype=F32), rows)
    half = HEAD_DIM // 2
    inv_freq = ROPE_BASE ** (-jnp.arange(0, half, 2, dtype=F32) / half)
    ang_r = row[:, None] * inv_freq
    ang_c = col[:, None] * inv_freq
    cr, sr, cc, sc = jnp.cos(ang_r), jnp.sin(ang_r), jnp.cos(ang_c), jnp.sin(ang_c)
    cos = jnp.concatenate([cr, cr, cc, cc], axis=-1)
    sin = jnp.concatenate([-sr, sr, -sc, sc], axis=-1)
    reps = LANES // HEAD_DIM
    return jnp.tile(cos, (1, reps)), jnp.tile(sin, (1, reps))


def _permute_heads(w, axis, bases):
    pieces, pos = [], 0
    for base in bases:
        pieces.append(lax.slice_in_dim(w, pos, base, axis=axis))
        for h in (0, 2, 1, 3):
            pieces.append(lax.slice_in_dim(w, base + h * HEAD_DIM, base + (h + 1) * HEAD_DIM, axis=axis))
        pos = base + 4 * HEAD_DIM
    pieces.append(lax.slice_in_dim(w, pos, w.shape[axis], axis=axis))
    return jnp.concatenate([p for p in pieces if p.shape[axis]], axis=axis)


def _block_diag(w):
    nb, hb, _ = w.shape
    per = nb // 2
    eye = jnp.eye(per, dtype=w.dtype)
    blocks = w.reshape(2, per, hb, 1, hb) * eye[None, :, None, :, None]
    return blocks.reshape(2, per * hb, per * hb)


def kernel(x, c, ctx, c_ctx, w_mod, b_mod, g_pre_mix, g_post_mix, g_pre_mlp, g_post_mlp, w_in, g_q_a,
           g_k_a, sink_b, conv_w, conv_b, lru_w_a, lru_b_a, lru_w_i, lru_b_i, lru_lambda, w_out,
           w_mlp_in, w_mlp_out):
    b, n, d = x.shape
    depth = w_mod.shape[0]
    cos, sin = _rope_tables(n)
    seg = jnp.arange(A_Q) // HEAD_DIM
    bd = jnp.where(seg[:, None] == seg[None, :], 1.0 / HEAD_DIM, 0.0).astype(BF16)

    mod_rows = -(-(b + 1) // SUBLANES) * SUBLANES
    cc = jnp.zeros((mod_rows, d), F32).at[:b].set(c).at[b].set(c_ctx)

    mod_all = _modulation(cc, w_mod, b_mod[:, None, :])
    m = ctx.shape[1]
    flat = lambda t: t.reshape(1, b * m, t.shape[-1])
    unflat = lambda t: t.reshape(b, m, t.shape[-1])

    for l in range(depth):
        last = l == depth - 1
        mod_lat = mod_all[l, :b].reshape(b, N_MOD, d)
        mod_ctx = mod_all[l, b].reshape(1, N_MOD, d)

        w_in_l = _permute_heads(w_in[l], 1, (0, A_Q + 2 * A_KV)).astype(BF16)
        gq = jnp.tile(g_q_a[l], A_Q_HEADS)[None, :]
        gk = jnp.tile(g_k_a[l], A_KV_HEADS)[None, :]
        gpre = g_pre_mix[l][None, :]
        qa, ka, va, qb, kb, vb, xr, gr = _inproj(x, mod_lat, gpre, w_in_l, gq, gk, bd, cos, sin, rope=True)
        qa_c, ka_c, va_c, qb_c, kb_c, vb_c, xr_c, gr_c = map(unflat, _inproj(
            flat(ctx), mod_ctx, gpre, w_in_l, gq, gk, bd, cos, sin, rope=False))

        feat_a = _dense_attn(qa, [(ka_c, va_c), (ka, va)], "global_attn")
        feat_b = _win_attn(qb, kb, vb, kb_c, vb_c, sink_b[l])

        wg = (0.5 * jnp.stack([jnp.concatenate([_block_diag(lru_w_a[l, dd]), _block_diag(lru_w_i[l, dd])],
                                               axis=-1) for dd in range(2)])).astype(BF16)
        bg = 0.5 * jnp.stack([lru_b_a[l, 0], lru_b_i[l, 0], lru_b_a[l, 1], lru_b_i[l, 1]])
        feat_c, feat_cc = _lru(xr_c, gr_c, xr, gr, conv_w[l], conv_b[l][None, :], wg, bg, lru_lambda[l],
                               with_ctx=not last)

        wo = _permute_heads(w_out[l], 0, (0, A_Q)).astype(BF16)
        post_w = (wo, g_post_mix[l][None, :], g_pre_mlp[l][None, :], w_mlp_in[l].astype(BF16),
                  w_mlp_out[l].astype(BF16), g_post_mlp[l][None, :])
        x = _post(feat_a, feat_b, feat_c, x, mod_lat, *post_w, name="post_mlp")
        if not last:
            feat_ac, feat_bc = _ctx_attn(qa_c, ka_c, va_c, qb_c, kb_c, vb_c, sink_b[l])
            ctx = unflat(_post(flat(feat_ac), flat(feat_bc), flat(feat_cc), flat(ctx), mod_ctx, *post_w,
                               name="post_mlp_ctx"))
    return x
```

```python
import functools

import jax
import jax.numpy as jnp
from jax import lax
from jax.experimental import pallas as pl
from jax.experimental.pallas import tpu as pltpu

F32 = jnp.float32
BF16 = jnp.bfloat16

GRID_W = 64
HEAD_DIM = 64
A_Q_HEADS = 4
A_KV_HEADS = 2
B_Q_HEADS = 4
B_KV_HEADS = 2
WINDOW = 128
LRU_WIDTH = 512
CONV_W = 4
CONV_LEFT = CONV_W // 2
LRU_C = 8.0
ROPE_BASE = 10000.0
EPS = 1e-6
NEG_INF = -1e30
TINY = 1.17549435e-38
N_MOD = 6

A_Q = A_Q_HEADS * HEAD_DIM
A_KV = A_KV_HEADS * HEAD_DIM
B_Q = B_Q_HEADS * HEAD_DIM
B_KV = B_KV_HEADS * HEAD_DIM
Q_SCALE = HEAD_DIM ** -0.5
ROT_HALF = HEAD_DIM // 2
ROT_PAIR = HEAD_DIM // 4

LANES = 128
SUBLANES = 8
VMEM_LIMIT = 56 * 1024 * 1024

MOD_COL_TILES = 4
FF_CHUNKS = 4
ROW_TILE = 1024
INPROJ_TILE = 1024
SUB_TILE = 256
POST_SUB_TILE = 512
ATTN_Q_TILE = 1024
CTX_ATTN_BATCH = 4
ATTN_CHAIN = 512
WIN_GROUP = 16
WIN_CHAINS = 4
LRU_CHUNK = 1024
SCAN_BLOCK = SUBLANES * SUBLANES

_NT = (((1,), (1,)), ((), ()))

assert A_Q == B_Q == 2 * LANES and A_KV == B_KV == LANES and LRU_WIDTH % (2 * LANES) == 0


def _const_spec(shape):
    return pl.BlockSpec(shape, lambda *_: (0,) * len(shape), pipeline_mode=pl.Buffered(1))


def _params(*sem):
    return pltpu.CompilerParams(dimension_semantics=sem, vmem_limit_bytes=VMEM_LIMIT)


def _rms(x, g):
    return x * lax.rsqrt(jnp.mean(x * x, axis=-1, keepdims=True) + EPS) * g


def _gelu_tanh(x):
    c = (2.0 / jnp.pi) ** 0.5
    half_x = 0.5 * x
    return half_x + half_x * jnp.tanh(x * (c + (c * 0.044715) * (x * x)))


def _mod_kernel(c_ref, w_ref, b_ref, o_ref):
    c = c_ref[...]
    act = (c * jax.nn.sigmoid(c)).astype(BF16)
    o_ref[...] = jnp.dot(act, w_ref[...].astype(BF16), preferred_element_type=F32) + b_ref[...]


def _modulation(cc, w, b):
    rows, d = cc.shape
    depth, _, n_out = w.shape
    assert n_out % (MOD_COL_TILES * LANES) == 0
    tn = n_out // MOD_COL_TILES
    return pl.pallas_call(
        _mod_kernel,
        out_shape=jax.ShapeDtypeStruct((depth, rows, n_out), F32),
        grid=(depth, n_out // tn),
        in_specs=[
            pl.BlockSpec((rows, d), lambda l, j: (0, 0)),
            pl.BlockSpec((None, d, tn), lambda l, j: (l, 0, j)),
            pl.BlockSpec((None, 1, tn), lambda l, j: (l, 0, j)),
        ],
        out_specs=pl.BlockSpec((None, rows, tn), lambda l, j: (l, 0, j)),
        compiler_params=_params("arbitrary", "arbitrary"),
        name="modulation",
    )(cc, w, b)


def _head_rms(t, g, bd):
    ms = jnp.dot((t * t).astype(BF16), bd, preferred_element_type=F32)
    return t * lax.rsqrt(ms + EPS) * g


def _rope(t, cos, sin, low_half):
    outs = []
    for j in range(t.shape[1] // LANES):
        tc = t[:, j * LANES:(j + 1) * LANES]
        partner = jnp.where(low_half, pltpu.roll(tc, LANES - ROT_PAIR, 1), pltpu.roll(tc, ROT_PAIR, 1))
        outs.append(tc * cos + partner * sin)
    return outs[0] if len(outs) == 1 else jnp.concatenate(outs, axis=-1)


def _inproj_kernel(x_ref, mod_ref, gpre_ref, w_ref, gq_ref, gk_ref, bd_ref, cos_ref, sin_ref,
                   qa_ref, ka_ref, va_ref, qb_ref, kb_ref, vb_ref, xr_ref, gr_ref, *, rope):
    mod = mod_ref[0]
    bd = bd_ref[...]
    tm = x_ref.shape[1]
    sub = min(SUB_TILE, tm)
    pre = []
    for r0 in range(0, tm, sub):
        rows = slice(r0, r0 + sub)
        h = _rms(x_ref[0, rows, :], gpre_ref[...]) * (1.0 + mod[1:2]) + mod[0:1]
        pre.append((rows, h.astype(BF16)))
    o2 = A_Q + 2 * A_KV + B_Q + 2 * B_KV
    for rows, hb in pre:
        z = jnp.dot(hb, w_ref[:, :o2], preferred_element_type=F32)
        qa = _head_rms(z[:, 0:A_Q], gq_ref[...], bd)
        ka = _head_rms(z[:, A_Q:A_Q + A_KV], gk_ref[...], bd[:A_KV, :A_KV])
        o = A_Q + 2 * A_KV
        qb = z[:, o:o + B_Q]
        kb = z[:, o + B_Q:o + B_Q + B_KV]
        if rope:
            cos = cos_ref[rows, :]
            sin = sin_ref[rows, :]
            lane = lax.broadcasted_iota(jnp.int32, cos.shape, 1)
            low_half = (lane % ROT_HALF) < ROT_PAIR
            qa = _rope(qa, cos, sin, low_half)
            ka = _rope(ka, cos, sin, low_half)
            qb = _rope(qb, cos, sin, low_half)
            kb = _rope(kb, cos, sin, low_half)
        qa_ref[0, rows, :] = (qa * Q_SCALE).astype(BF16)
        ka_ref[0, rows, :] = ka.astype(BF16)
        va_ref[0, rows, :] = z[:, A_Q + A_KV:o].astype(BF16)
        qb_ref[0, rows, :] = (qb * Q_SCALE).astype(BF16)
        kb_ref[0, rows, :] = kb.astype(BF16)
        vb_ref[0, rows, :] = z[:, o + B_Q + B_KV:o + B_Q + 2 * B_KV].astype(BF16)
    for rows, hb in pre:
        zr = jnp.dot(hb, w_ref[:, o2:], preferred_element_type=F32)
        xr_ref[0, rows, :] = zr[:, :LRU_WIDTH]
        gr_ref[0, rows, :] = _gelu_tanh(zr[:, LRU_WIDTH:])


def _inproj(x, mod, gpre, w, gq, gk, bd, cos, sin, *, rope):
    b, n, d = x.shape
    tm = min(INPROJ_TILE, n)
    assert n % tm == 0 and tm % min(SUB_TILE, tm) == 0
    assert w.shape == (d, A_Q + 2 * A_KV + B_Q + 2 * B_KV + 2 * LRU_WIDTH)
    nt = n // tm
    widths = (A_Q, A_KV, A_KV, B_Q, B_KV, B_KV, LRU_WIDTH, LRU_WIDTH)
    dtypes = (BF16,) * 6 + (F32, F32)
    row_spec = lambda wd: pl.BlockSpec((1, tm, wd), lambda i, j: (i, j, 0))
    table_spec = pl.BlockSpec((tm, LANES), (lambda i, j: (j, 0)) if rope else (lambda i, j: (0, 0)))
    return pl.pallas_call(
        functools.partial(_inproj_kernel, rope=rope),
        out_shape=[jax.ShapeDtypeStruct((b, n, wd), dt) for wd, dt in zip(widths, dtypes)],
        grid=(b, nt),
        in_specs=[
            row_spec(d),
            pl.BlockSpec((1, N_MOD, d), lambda i, j: (i, 0, 0)),
            _const_spec(gpre.shape),
            _const_spec(w.shape),
            _const_spec(gq.shape),
            _const_spec(gk.shape),
            _const_spec(bd.shape),
            table_spec,
            table_spec,
        ],
        out_specs=[row_spec(wd) for wd in widths],
        compiler_params=_params("parallel", "parallel"),
        name="inproj_rope" if rope else "inproj_ctx",
    )(x, mod, gpre, w, gq, gk, bd, cos, sin)


def _split_heads(qc, low):
    zero = jnp.zeros_like(qc)
    return jnp.where(low, qc, zero), jnp.where(low, zero, qc)


def _stack_heads(q, low):
    parts = []
    for j in range(q.shape[1] // LANES):
        parts.extend(_split_heads(q[:, j * LANES:(j + 1) * LANES], low))
    return jnp.concatenate(parts, axis=0)


def _unstack_heads(o, t, low):
    return jnp.concatenate([jnp.where(low, o[0:t], o[t:2 * t]),
                            jnp.where(low, o[2 * t:3 * t], o[3 * t:4 * t])], axis=-1)


def _with_ones(v):
    return jnp.concatenate([v, jnp.ones_like(v)], axis=-1)


def _chain_sizes(total):
    assert total % ATTN_CHAIN == 0
    half = ATTN_CHAIN // 2
    return (half,) + (ATTN_CHAIN,) * (total // ATTN_CHAIN - 1) + (half,)


def _attend(q, keys, vals, sink_ref):
    tq = q.shape[0]
    has_sink = sink_ref is not None
    low = lax.broadcasted_iota(jnp.int32, (tq, LANES), 1) < HEAD_DIM
    qs_all = _stack_heads(q, low)
    if has_sink:
        sink_all = jnp.concatenate([jnp.full((tq, LANES), sink_ref[h], F32) for h in (0, 2, 1, 3)], axis=0)
    outs, r0 = [], 0
    for rows in _chain_sizes(4 * tq):
        qs = qs_all[r0:r0 + rows]
        scores = [lax.dot_general(qs, k, _NT, preferred_element_type=F32) for k in keys]
        s = scores[0] if len(keys) == 1 else jnp.concatenate(scores, axis=-1)
        m = jnp.max(s, axis=-1, keepdims=True)
        if has_sink:
            sink = sink_all[r0:r0 + rows]
            m_wide = jnp.maximum(jnp.broadcast_to(m, sink.shape), sink)
            p = jnp.exp(s - jnp.concatenate([m_wide] * (s.shape[1] // LANES), axis=-1)).astype(BF16)
        else:
            p = jnp.exp(s - m).astype(BF16)
        acc, c0 = None, 0
        for k, v in zip(keys, vals):
            pv = jnp.dot(p[:, c0:c0 + k.shape[0]], v, preferred_element_type=F32)
            acc = pv if acc is None else acc + pv
            c0 += k.shape[0]
        denom = acc[:, LANES:]
        if has_sink:
            denom = denom + jnp.exp(sink - m_wide)
        outs.append(acc[:, :LANES] / denom)
        r0 += rows
    return _unstack_heads(jnp.concatenate(outs, axis=0), tq, low)


def _dense_attn_kernel(*refs, n_kv):
    q_ref, kv_refs, o_ref = refs[0], refs[1:1 + 2 * n_kv], refs[-1]
    keys = [kv_refs[2 * i][0] for i in range(n_kv)]
    vals = [_with_ones(kv_refs[2 * i + 1][0]) for i in range(n_kv)]
    o_ref[0] = _attend(q_ref[0], keys, vals, None).astype(o_ref.dtype)


def _ctx_attn_kernel(qa_ref, ka_ref, va_ref, qb_ref, kb_ref, vb_ref, sink_ref, oa_ref, ob_ref):
    for i in range(qa_ref.shape[0]):
        oa_ref[i] = _attend(qa_ref[i], [ka_ref[i]], [_with_ones(va_ref[i])], None).astype(oa_ref.dtype)
        ob_ref[i] = _attend(qb_ref[i], [kb_ref[i]], [_with_ones(vb_ref[i])], sink_ref).astype(ob_ref.dtype)


def _dense_attn(q, kvs, name):
    b, n, w = q.shape
    tq = min(ATTN_Q_TILE, n)
    assert n % tq == 0
    in_specs = [pl.BlockSpec((1, tq, w), lambda i, j: (i, j, 0))]
    args = [q]
    for k, v in kvs:
        for t in (k, v):
            in_specs.append(pl.BlockSpec((1,) + t.shape[1:], lambda i, j: (i, 0, 0)))
            args.append(t)
    return pl.pallas_call(
        functools.partial(_dense_attn_kernel, n_kv=len(kvs)),
        out_shape=jax.ShapeDtypeStruct((b, n, w), BF16),
        grid=(b, n // tq),
        in_specs=in_specs,
        out_specs=pl.BlockSpec((1, tq, w), lambda i, j: (i, j, 0)),
        compiler_params=_params("parallel", "parallel"),
        name=name,
    )(*args)


def _ctx_attn(qa, ka, va, qb, kb, vb, sink):
    b = qa.shape[0]
    bt = CTX_ATTN_BATCH if b % CTX_ATTN_BATCH == 0 else 1
    spec = lambda t: pl.BlockSpec((bt,) + t.shape[1:], lambda i: (i, 0, 0))
    args = (qa, ka, va, qb, kb, vb)
    return pl.pallas_call(
        _ctx_attn_kernel,
        out_shape=[jax.ShapeDtypeStruct(qa.shape, BF16), jax.ShapeDtypeStruct(qb.shape, BF16)],
        grid=(b // bt,),
        in_specs=[spec(t) for t in args] + [pl.BlockSpec(memory_space=pltpu.SMEM)],
        out_specs=[spec(qa), spec(qb)],
        compiler_params=_params("parallel"),
        name="ctx_attn",
    )(*args, sink)


def _win_attn_kernel(q_ref, k_ref, v_ref, kc_ref, vc_ref, sink_ref, o_ref):
    n = q_ref.shape[1]
    qb = WINDOW
    band = 3 * qb
    rows = WIN_GROUP * qb
    n_iter = n // rows
    low = lax.broadcasted_iota(jnp.int32, (qb, LANES), 1) < HEAD_DIM
    row_minus_col = (lax.broadcasted_iota(jnp.int32, (qb, band), 0)
                     - lax.broadcasted_iota(jnp.int32, (qb, band), 1))

    def band_bias(first_key_offset):
        bias = jnp.where(jnp.abs(row_minus_col + first_key_offset) <= WINDOW, 0.0, NEG_INF)
        return jnp.concatenate([bias.astype(F32)] * 4, axis=0)

    bias_first, bias_mid, bias_last = band_bias(0), band_bias(qb), band_bias(2 * qb)
    kc = kc_ref[0]
    vc = _with_ones(vc_ref[0])
    ones_band = jnp.ones((band, LANES), BF16)
    sink_blk = jnp.concatenate([jnp.full((qb, LANES), sink_ref[h], F32) for h in (0, 2, 1, 3)], axis=0)
    per_chain = WIN_GROUP // WIN_CHAINS
    sink = jnp.concatenate([sink_blk] * per_chain, axis=0)

    def chain(it, t0, q, blocks):
        qs_blk = [_stack_heads(q[g * qb:(g + 1) * qb], low) for g in blocks]
        starts = [pl.multiple_of(jnp.clip(t0 + (g - 1) * qb, 0, n - band), qb) for g in blocks]
        s_ctx = lax.dot_general(jnp.concatenate(qs_blk, axis=0), kc, _NT, preferred_element_type=F32)
        s_band = []
        for i, g in enumerate(blocks):
            s = lax.dot_general(qs_blk[i], k_ref[0, pl.ds(starts[i], band), :], _NT,
                                preferred_element_type=F32)
            bias = bias_mid
            if g == 0:
                bias = jnp.where(it == 0, bias_first, bias)
            if g == WIN_GROUP - 1:
                bias = jnp.where(it == n_iter - 1, bias_last, bias)
            s_band.append(s + bias)
        s = jnp.concatenate([jnp.concatenate(s_band, axis=0), s_ctx], axis=-1)
        m = jnp.maximum(jnp.broadcast_to(jnp.max(s, axis=-1, keepdims=True), sink.shape), sink)
        p = jnp.exp(s - jnp.concatenate([m] * (s.shape[1] // LANES), axis=-1)).astype(BF16)
        acc_band = [jnp.dot(p[i * 4 * qb:(i + 1) * 4 * qb, :band],
                            jnp.concatenate([v_ref[0, pl.ds(starts[i], band), :], ones_band], axis=-1),
                            preferred_element_type=F32) for i in range(len(blocks))]
        acc = jnp.concatenate(acc_band, axis=0) + jnp.dot(p[:, band:], vc, preferred_element_type=F32)
        out = acc[:, :LANES] / (acc[:, LANES:] + jnp.exp(sink - m))
        return [_unstack_heads(out[i * 4 * qb:(i + 1) * 4 * qb], qb, low) for i in range(len(blocks))]

    def body(it, carry):
        t0 = pl.multiple_of(it * rows, rows)
        q = q_ref[0, pl.ds(t0, rows), :]
        outs = []
        for c in range(WIN_CHAINS):
            outs.extend(chain(it, t0, q, range(c * per_chain, (c + 1) * per_chain)))
        o_ref[0, pl.ds(t0, rows), :] = jnp.concatenate(outs, axis=0).astype(o_ref.dtype)
        return carry

    lax.fori_loop(0, n_iter, body, 0)


def _win_attn(q, k, v, kc, vc, sink):
    b, n, w = q.shape
    assert n % (WIN_GROUP * WINDOW) == 0 and WIN_GROUP % WIN_CHAINS == 0
    full = lambda t: pl.BlockSpec((1,) + t.shape[1:], lambda i: (i, 0, 0))
    return pl.pallas_call(
        _win_attn_kernel,
        out_shape=jax.ShapeDtypeStruct((b, n, w), BF16),
        grid=(b,),
        in_specs=[full(q), full(k), full(v), full(kc), full(vc),
                  pl.BlockSpec(memory_space=pltpu.SMEM)],
        out_specs=full(q),
        compiler_params=_params("parallel"),
        name="window_attn",
    )(q, k, v, kc, vc, sink)


def _scan_block(a_scr, u_scr, h_scr, base, col, c_in, row8, reverse):
    order = range(SUBLANES - 1, -1, -1) if reverse else range(SUBLANES)
    a_cum, h_loc = [], []
    for j in order:
        aj = a_scr[col, base + j * SUBLANES:base + (j + 1) * SUBLANES, :]
        uj = u_scr[col, base + j * SUBLANES:base + (j + 1) * SUBLANES, :]
        if a_cum:
            h_loc.append(aj * h_loc[-1] + uj)
            a_cum.append(aj * a_cum[-1])
        else:
            h_loc.append(uj)
            a_cum.append(aj)
    p, q = a_cum[-1], h_loc[-1]
    for s in (1, 2, 4):
        shift = SUBLANES - s if reverse else s
        valid = (row8 < SUBLANES - s) if reverse else (row8 >= s)
        q = jnp.where(valid, p * pltpu.roll(q, shift, 0) + q, q)
        p = jnp.where(valid, p * pltpu.roll(p, shift, 0), p)
    end = q + p * c_in
    if reverse:
        c_grp = jnp.where(row8 < SUBLANES - 1, pltpu.roll(end, SUBLANES - 1, 0), c_in)
        last = end[0:1]
    else:
        c_grp = jnp.where(row8 >= 1, pltpu.roll(end, 1, 0), c_in)
        last = end[SUBLANES - 1:SUBLANES]
    for idx, j in enumerate(order):
        h_scr[col, pl.ds(base + j, SUBLANES, stride=SUBLANES), :] = h_loc[idx] + a_cum[idx] * c_grp
    return jnp.broadcast_to(last, (SUBLANES, LANES))


def _conv_block_order(src_ref, t0, rows, cw_ref, cb_ref, xt_scr, row8):
    length = src_ref.shape[1]
    blk = SCAN_BLOCK
    n_blk = rows // blk
    n_col = LRU_WIDTH // LANES
    lo = jnp.maximum(t0 - blk, 0)
    hi = jnp.minimum(t0 + rows, length - blk)
    prev = jnp.where(t0 > 0, src_ref[0, pl.ds(pl.multiple_of(lo, blk), blk), :], 0.0)
    nxt = jnp.where(t0 + rows < length, src_ref[0, pl.ds(pl.multiple_of(hi, blk), blk), :], 0.0)
    win = jnp.concatenate([prev, src_ref[0, pl.ds(t0, rows), :], nxt], axis=0)
    for col in range(n_col):
        for i in range(win.shape[0] // SUBLANES):
            b, g = divmod(i, SUBLANES)
            xt_scr[col, pl.ds(b * blk + g, SUBLANES, stride=SUBLANES), :] = (
                win[i * SUBLANES:(i + 1) * SUBLANES, col * LANES:(col + 1) * LANES])

    def x_at(b, j, col):
        base = (b + 1) * blk + j * SUBLANES
        return xt_scr[col, base:base + SUBLANES, :]

    out_rows = []
    for b in range(n_blk):
        per_j = [[] for _ in range(SUBLANES)]
        for col in range(n_col):
            lanes = slice(col * LANES, (col + 1) * LANES)
            xs = [x_at(b, j, col) for j in range(SUBLANES)]
            before = [jnp.where(row8 >= 1, pltpu.roll(xs[j], 1, 0), pltpu.roll(x_at(b - 1, j, col), 1, 0))
                      for j in (SUBLANES - 2, SUBLANES - 1)]
            after = jnp.where(row8 < SUBLANES - 1, pltpu.roll(xs[0], SUBLANES - 1, 0),
                              pltpu.roll(x_at(b + 1, 0, col), SUBLANES - 1, 0))
            ext = before + xs + [after]
            for j in range(SUBLANES):
                acc = cb_ref[:, lanes]
                for o in range(CONV_W):
                    acc = acc + ext[j + o] * cw_ref[o:o + 1, lanes]
                per_j[j].append(acc)
        out_rows.extend(jnp.concatenate(vs, axis=-1) for vs in per_j)
    return jnp.concatenate(out_rows, axis=0)


def _lru_chunk(src_ref, conv_ref, t0, rows, d, carry, reverse, cw_ref, cb_ref, wg_ref, bg_ref,
               neg_quarter_nsp, xt_scr, a_scr, u_scr, h_scr, row8):
    if reverse:
        xc = conv_ref[pl.ds(t0, rows), :]
    else:
        xc = _conv_block_order(src_ref, t0, rows, cw_ref, cb_ref, xt_scr, row8)
        conv_ref[pl.ds(t0, rows), :] = xc
    xb = xc.astype(BF16)
    half = LRU_WIDTH // 2
    y0 = jnp.dot(xb[:, :half], wg_ref[d, 0], preferred_element_type=F32)
    y1 = jnp.dot(xb[:, half:], wg_ref[d, 1], preferred_element_type=F32)
    ya = jnp.concatenate([y0[:, :half], y1[:, :half]], axis=-1)
    yi = jnp.concatenate([y0[:, half:], y1[:, half:]], axis=-1)
    t_r = jnp.tanh(ya + bg_ref[2 * d:2 * d + 1, :])
    t_i = jnp.tanh(yi + bg_ref[2 * d + 1:2 * d + 2, :])
    s = jnp.tanh(t_r * neg_quarter_nsp + neg_quarter_nsp)
    inv = 1.0 / (1.0 + s)
    a = (1.0 - s) * inv
    root = s * lax.rsqrt(jnp.maximum(s, TINY))
    u = root * inv * ((t_i + 1.0) * xc)
    for col in range(LRU_WIDTH // LANES):
        a_scr[col, 0:rows, :] = a[:, col * LANES:(col + 1) * LANES]
        u_scr[col, 0:rows, :] = u[:, col * LANES:(col + 1) * LANES]
    n_blk = rows // SCAN_BLOCK
    blocks = range(n_blk - 1, -1, -1) if reverse else range(n_blk)
    carry = list(carry)
    for blk in blocks:
        for col in range(LRU_WIDTH // LANES):
            carry[col] = _scan_block(a_scr, u_scr, h_scr, blk * SCAN_BLOCK, col, carry[col],
                                     row8, reverse)
    return tuple(carry)


def _lru_kernel(xc_ref, gc_ref, x_ref, g_ref, cw_ref, cb_ref, wg_ref, bg_ref, lam_ref, *rest,
                with_ctx):
    if with_ctx:
        o_ref, oc_ref, xconv, xcconv, xt_scr, a_scr, u_scr, h_scr, hf, hfc = rest
    else:
        o_ref, xconv, xcconv, xt_scr, a_scr, u_scr, h_scr, hf = rest
    n = x_ref.shape[1]
    m = xc_ref.shape[1]

    neg_lam = -lam_ref[...]
    softplus = jnp.maximum(neg_lam, 0.0) + jnp.log1p(jnp.exp(-jnp.abs(neg_lam)))
    neg_quarter_nsp = (0.25 * LRU_C) * softplus
    row8 = lax.broadcasted_iota(jnp.int32, (SUBLANES, LANES), 0)
    zero_carry = tuple(jnp.zeros((SUBLANES, LANES), F32) for _ in range(LRU_WIDTH // LANES))
    n_chunks = n // LRU_CHUNK

    def chunk(ctx_part, t0, rows, d, carry):
        src_ref, conv_ref = (xc_ref, xcconv) if ctx_part else (x_ref, xconv)
        return _lru_chunk(src_ref, conv_ref, t0, rows, d, carry, d == 1, cw_ref, cb_ref, wg_ref, bg_ref,
                          neg_quarter_nsp[d:d + 1, :], xt_scr, a_scr, u_scr, h_scr, row8)

    def read_h(rows):
        return jnp.concatenate([h_scr[col, 0:rows, :] for col in range(LRU_WIDTH // LANES)], axis=-1)

    carry = chunk(True, 0, m, 0, zero_carry)
    if with_ctx:
        hfc[...] = read_h(m)

    def fwd_body(ci, carry):
        t0 = pl.multiple_of(ci * LRU_CHUNK, LRU_CHUNK)
        carry = chunk(False, t0, LRU_CHUNK, 0, carry)
        hf[pl.ds(t0, LRU_CHUNK), :] = read_h(LRU_CHUNK)
        return carry

    lax.fori_loop(0, n_chunks, fwd_body, carry)

    carry = chunk(True, 0, m, 1, zero_carry)
    if with_ctx:
        oc_ref[0] = ((hfc[...] + read_h(m)) * gc_ref[0]).astype(oc_ref.dtype)

    def bwd_body(ci, carry):
        t0 = pl.multiple_of((n_chunks - 1 - ci) * LRU_CHUNK, LRU_CHUNK)
        carry = chunk(False, t0, LRU_CHUNK, 1, carry)
        hf[pl.ds(t0, LRU_CHUNK), :] = hf[pl.ds(t0, LRU_CHUNK), :] + read_h(LRU_CHUNK)
        return carry

    lax.fori_loop(0, n_chunks, bwd_body, carry)

    def gate_body(ci, _):
        t0 = pl.multiple_of(ci * LRU_CHUNK, LRU_CHUNK)
        gate = g_ref[0, pl.ds(t0, LRU_CHUNK), :]
        o_ref[0, pl.ds(t0, LRU_CHUNK), :] = (hf[pl.ds(t0, LRU_CHUNK), :] * gate).astype(o_ref.dtype)
        return 0

    lax.fori_loop(0, n_chunks, gate_body, 0)


def _lru(xc, gc, x, g, cw, cb, wg, bg, lam, *, with_ctx):
    b, n, w = x.shape
    m = xc.shape[1]
    assert n % LRU_CHUNK == 0 and m % SCAN_BLOCK == 0 and m <= LRU_CHUNK and w == LRU_WIDTH
    full = lambda t: pl.BlockSpec((1,) + t.shape[1:], lambda i: (i, 0, 0))
    out_shape = [jax.ShapeDtypeStruct((b, n, w), BF16)]
    out_specs = [pl.BlockSpec((1, n, w), lambda i: (i, 0, 0))]
    scratch = [pltpu.VMEM((n, w), F32), pltpu.VMEM((m, w), F32),
               pltpu.VMEM((w // LANES, LRU_CHUNK + 2 * SCAN_BLOCK, LANES), F32),
               *[pltpu.VMEM((w // LANES, LRU_CHUNK, LANES), F32) for _ in range(3)],
               pltpu.VMEM((n, w), F32)]
    if with_ctx:
        out_shape.append(jax.ShapeDtypeStruct((b, m, w), BF16))
        out_specs.append(pl.BlockSpec((1, m, w), lambda i: (i, 0, 0)))
        scratch.append(pltpu.VMEM((m, w), F32))
    outs = pl.pallas_call(
        functools.partial(_lru_kernel, with_ctx=with_ctx),
        out_shape=out_shape,
        grid=(b,),
        in_specs=[full(xc), full(gc), full(x), full(g), _const_spec(cw.shape), _const_spec(cb.shape),
                  _const_spec(wg.shape), _const_spec(bg.shape), _const_spec(lam.shape)],
        out_specs=out_specs,
        scratch_shapes=scratch,
        compiler_params=_params("parallel"),
        name="rglru_ctx" if with_ctx else "rglru",
    )(xc, gc, x, g, cw, cb, wg, bg, lam)
    return (outs[0], outs[1]) if with_ctx else (outs[0], None)


def _post_kernel(fa_ref, fb_ref, fc_ref, x_ref, mod_ref, wo_ref, gpost_ref, gpre_ref, w1_ref, w2_ref,
                 gpm_ref, o_ref):
    mod = mod_ref[0]
    tm = x_ref.shape[1]
    sub = min(POST_SUB_TILE, tm)
    d_ff = w1_ref.shape[1]
    ff_chunk = d_ff // FF_CHUNKS
    pre = []
    for r0 in range(0, tm, sub):
        rows = slice(r0, r0 + sub)
        y = (jnp.dot(fa_ref[0, rows, :], wo_ref[0:A_Q, :], preferred_element_type=F32)
             + jnp.dot(fb_ref[0, rows, :], wo_ref[A_Q:A_Q + B_Q, :], preferred_element_type=F32)
             + jnp.dot(fc_ref[0, rows, :], wo_ref[A_Q + B_Q:, :], preferred_element_type=F32))
        x1 = x_ref[0, rows, :] + mod[2:3] * _rms(y, gpost_ref[...])
        h2 = (_rms(x1, gpre_ref[...]) * (1.0 + mod[4:5]) + mod[3:4]).astype(BF16)
        pre.append((rows, x1, h2))
    for rows, x1, h2 in pre:
        acc = None
        for k in range(FF_CHUNKS):
            hk = jnp.dot(h2, w1_ref[:, k * ff_chunk:(k + 1) * ff_chunk], preferred_element_type=F32)
            hk = jnp.square(jnp.maximum(hk, 0.0)).astype(BF16)
            part = jnp.dot(hk, w2_ref[k * ff_chunk:(k + 1) * ff_chunk, :], preferred_element_type=F32)
            acc = part if acc is None else acc + part
        o_ref[0, rows, :] = x1 + mod[5:6] * _rms(acc, gpm_ref[...])


def _post(fa, fb, fc, x, mod, wo, gpost, gpre, w1, w2, gpm, name):
    b, n, d = x.shape
    tm = min(ROW_TILE, n)
    assert n % tm == 0 and tm % min(POST_SUB_TILE, tm) == 0 and w1.shape[1] % (FF_CHUNKS * LANES) == 0
    row_spec = lambda wd: pl.BlockSpec((1, tm, wd), lambda i, j: (i, j, 0))
    return pl.pallas_call(
        _post_kernel,
        out_shape=jax.ShapeDtypeStruct((b, n, d), F32),
        grid=(b, n // tm),
        in_specs=[row_spec(fa.shape[2]), row_spec(fb.shape[2]), row_spec(fc.shape[2]), row_spec(d),
                  pl.BlockSpec((1, N_MOD, d), lambda i, j: (i, 0, 0)),
                  _const_spec(wo.shape), _const_spec(gpost.shape), _const_spec(gpre.shape),
                  _const_spec(w1.shape), _const_spec(w2.shape), _const_spec(gpm.shape)],
        out_specs=row_spec(d),
        compiler_params=_params("parallel", "parallel"),
        name=name,
    )(fa, fb, fc, x, mod, wo, gpost, gpre, w1, w2, gpm)


def _rope_tables(n):
    rows = n // GRID_W
    row = jnp.repeat(jnp.arange(rows, dtype=F32), GRID_W)
    col = jnp.tile(jnp.arange(GRID_W, dtype=F32), rows)
    half = HEAD_DIM // 2
    inv_freq = ROPE_BASE ** (-jnp.arange(0, half, 2, dtype=F32) / half)
    ang_r = row[:, None] * inv_freq
    ang_c = col[:, None] * inv_freq
    cr, sr, cc, sc = jnp.cos(ang_r), jnp.sin(ang_r), jnp.cos(ang_c), jnp.sin(ang_c)
    cos = jnp.concatenate([cr, cr, cc, cc], axis=-1)
    sin = jnp.concatenate([-sr, sr, -sc, sc], axis=-1)
    reps = LANES // HEAD_DIM
    return jnp.tile(cos, (1, reps)), jnp.tile(sin, (1, reps))


def _permute_heads(w, axis, bases):
    pieces, pos = [], 0
    for base in bases:
        pieces.append(lax.slice_in_dim(w, pos, base, axis=axis))
        for h in (0, 2, 1, 3):
            pieces.append(lax.slice_in_dim(w, base + h * HEAD_DIM, base + (h + 1) * HEAD_DIM, axis=axis))
        pos = base + 4 * HEAD_DIM
    pieces.append(lax.slice_in_dim(w, pos, w.shape[axis], axis=axis))
    return jnp.concatenate([p for p in pieces if p.shape[axis]], axis=axis)


def _block_diag(w):
    nb, hb, _ = w.shape
    per = nb // 2
    eye = jnp.eye(per, dtype=w.dtype)
    blocks = w.reshape(2, per, hb, 1, hb) * eye[None, :, None, :, None]
    return blocks.reshape(2, per * hb, per * hb)


def kernel(x, c, ctx, c_ctx, w_mod, b_mod, g_pre_mix, g_post_mix, g_pre_mlp, g_post_mlp, w_in, g_q_a,
           g_k_a, sink_b, conv_w, conv_b, lru_w_a, lru_b_a, lru_w_i, lru_b_i, lru_lambda, w_out,
           w_mlp_in, w_mlp_out):
    b, n, d = x.shape
    depth = w_mod.shape[0]
    cos, sin = _rope_tables(n)
    seg = jnp.arange(A_Q) // HEAD_DIM
    bd = jnp.where(seg[:, None] == seg[None, :], 1.0 / HEAD_DIM, 0.0).astype(BF16)

    mod_rows = -(-(b + 1) // SUBLANES) * SUBLANES
    cc = jnp.zeros((mod_rows, d), F32).at[:b].set(c).at[b].set(c_ctx)

    mod_all = _modulation(cc, w_mod, b_mod[:, None, :])
    m = ctx.shape[1]
    flat = lambda t: t.reshape(1, b * m, t.shape[-1])
    unflat = lambda t: t.reshape(b, m, t.shape[-1])

    for l in range(depth):
        last = l == depth - 1
        mod_lat = mod_all[l, :b].reshape(b, N_MOD, d)
        mod_ctx = mod_all[l, b].reshape(1, N_MOD, d)

        w_in_l = _permute_heads(w_in[l], 1, (0, A_Q + 2 * A_KV)).astype(BF16)
        gq = jnp.tile(g_q_a[l], A_Q_HEADS)[None, :]
        gk = jnp.tile(g_k_a[l], A_KV_HEADS)[None, :]
        gpre = g_pre_mix[l][None, :]
        qa, ka, va, qb, kb, vb, xr, gr = _inproj(x, mod_lat, gpre, w_in_l, gq, gk, bd, cos, sin, rope=True)
        qa_c, ka_c, va_c, qb_c, kb_c, vb_c, xr_c, gr_c = map(unflat, _inproj(
            flat(ctx), mod_ctx, gpre, w_in_l, gq, gk, bd, cos, sin, rope=False))

        feat_a = _dense_attn(qa, [(ka_c, va_c), (ka, va)], "global_attn")
        feat_b = _win_attn(qb, kb, vb, kb_c, vb_c, sink_b[l])

        wg = (0.5 * jnp.stack([jnp.concatenate([_block_diag(lru_w_a[l, dd]), _block_diag(lru_w_i[l, dd])],
                                               axis=-1) for dd in range(2)])).astype(BF16)
        bg = 0.5 * jnp.stack([lru_b_a[l, 0], lru_b_i[l, 0], lru_b_a[l, 1], lru_b_i[l, 1]])
        feat_c, feat_cc = _lru(xr_c, gr_c, xr, gr, conv_w[l], conv_b[l][None, :], wg, bg, lru_lambda[l],
                               with_ctx=not last)

        wo = _permute_heads(w_out[l], 0, (0, A_Q)).astype(BF16)
        post_w = (wo, g_post_mix[l][None, :], g_pre_mlp[l][None, :], w_mlp_in[l].astype(BF16),
                  w_mlp_out[l].astype(BF16), g_post_mlp[l][None, :])
        x = _post(feat_a, feat_b, feat_c, x, mod_lat, *post_w, name="post_mlp")
        if not last:
            feat_ac, feat_bc = _ctx_attn(qa_c, ka_c, va_c, qb_c, kb_c, vb_c, sink_b[l])
            ctx = unflat(_post(flat(feat_ac), flat(feat_bc), flat(feat_cc), flat(ctx), mod_ctx, *post_w,
                               name="post_mlp_ctx"))
    return x
```
